```python
import jax, jax.numpy as jnp
from jax import lax
import numpy as np

D_MODEL = 1024
BATCH = 1
SEQ = 16384
DEPTH = 1
DEC_BATCH = 16
DEC_SEQ = 32
PAST_LEN = 1024

CHUNK = 64
Q_BLOCK = 128

A_HEADS = 8
A_HEAD_DIM = 64
D_A = A_HEADS * A_HEAD_DIM
DECAY_LORA = 64
AAA_LORA = 64
GATE_LORA = 128
A_COLS = 3 * D_A + DECAY_LORA + AAA_LORA + GATE_LORA
A_SPLITS = (D_A, 2 * D_A, 3 * D_A, 3 * D_A + DECAY_LORA, 3 * D_A + DECAY_LORA + AAA_LORA)
LNX_EPS = A_HEAD_DIM * 1e-5

B_HEADS = 8
Q_LORA = 256
KV_LORA = 128
NOPE_DIM = 64
ROPE_DIM = 32
V_DIM = 64
D_B = B_HEADS * V_DIM
B_COLS = Q_LORA + KV_LORA + ROPE_DIM
ROPE_BASE = 10000.0
ATTN_SCALE = (NOPE_DIM + ROPE_DIM) ** -0.5
RMS_EPS = 1e-6

GATE_COLS = 2 * D_MODEL
N_IN = A_COLS + B_COLS + GATE_COLS

D_FF = -(-8 * D_MODEL // (3 * 256)) * 256
LN_EPS = 1e-5
DN_ALPHA = (2 * DEPTH) ** 0.25
DN_BETA = (8 * DEPTH) ** -0.25

kernel_name = "rwkv7_mla_gated_hybrid_stream_step"


def layer_norm(x, g, b):
    xf = x.astype(jnp.float32)
    mu = jnp.mean(xf, -1, keepdims=True)
    var = jnp.mean(jnp.square(xf - mu), -1, keepdims=True)
    return ((xf - mu) * lax.rsqrt(var + LN_EPS)).astype(x.dtype) * g + b


def rms_norm(x, g):
    xf = x.astype(jnp.float32)
    return (xf * lax.rsqrt(jnp.mean(xf * xf, -1, keepdims=True) + RMS_EPS)).astype(x.dtype) * g


def apply_rope(x, pos):
    half = ROPE_DIM // 2
    inv = ROPE_BASE ** (-jnp.arange(half, dtype=jnp.float32) / half)
    ang = pos.astype(jnp.float32)[:, None] * inv
    shape = (ang.shape[0],) + (1,) * (x.ndim - 3) + (half,)
    cos, sin = jnp.cos(ang).reshape(shape), jnp.sin(ang).reshape(shape)
    x1 = x[..., :half].astype(jnp.float32)
    x2 = x[..., half:].astype(jnp.float32)
    return jnp.concatenate([x1 * cos - x2 * sin, x1 * sin + x2 * cos], -1).astype(x.dtype)


def in_projection(x, w_in):
    p = x @ w_in
    return p[..., :A_COLS], p[..., A_COLS:A_COLS + B_COLS], p[..., A_COLS + B_COLS:]


def wkv_scan(r, w, k, v, aa, bb, s0):
    def step(S, inp):
        r_t, w_t, k_t, v_t, a_t, b_t = inp
        sa = jnp.einsum('bhvk,bhk->bhv', S, a_t)
        S = S * w_t[:, :, None, :] + sa[..., None] * b_t[:, :, None, :] + v_t[..., None] * k_t[:, :, None, :]
        return S, jnp.einsum('bhvk,bhk->bhv', S, r_t)
    xs = tuple(jnp.moveaxis(t, 1, 0) for t in (r, w, k, v, aa, bb))
    s_last, y = lax.scan(step, s0, xs)
    return s_last, jnp.moveaxis(y, 0, 1)


def rwkv_branch(p_a, shift_prev, wkv_prev, mu_shift, w0, w_w2, a0, w_a2, w_g2, k_k, k_a, r_k, lnx_g, lnx_b):
    B, T, _ = p_a.shape
    f32 = jnp.float32
    prev = jnp.concatenate([shift_prev.astype(p_a.dtype), p_a[:, :-1]], axis=1)
    xs = p_a + (prev - p_a) * mu_shift
    r, k, v, wd, ad, gd = jnp.split(xs, A_SPLITS, axis=-1)
    logw = -jax.nn.softplus(-(w0 + jnp.tanh(wd) @ w_w2).astype(f32)) - 0.5
    decay = jnp.exp(-jnp.exp(logw))
    a = jax.nn.sigmoid((a0 + ad @ w_a2).astype(f32))
    g = jax.nn.sigmoid(gd) @ w_g2
    heads = lambda t: t.astype(f32).reshape(B, T, A_HEADS, A_HEAD_DIM)
    kk = heads(k * k_k)
    kk = kk / jnp.maximum(jnp.linalg.norm(kk, axis=-1, keepdims=True), 1e-12)
    a_h = heads(a)
    k_h = heads(k.astype(f32) * (1.0 + (a - 1.0) * k_a.astype(f32)))
    r_h, v_h = heads(r), heads(v)
    s_last, y = wkv_scan(r_h, heads(decay), k_h, v_h, -kk, kk * a_h, wkv_prev.astype(f32))
    mu = jnp.mean(y, -1, keepdims=True)
    var = jnp.mean(jnp.square(y - mu), -1, keepdims=True)
    yn = ((y - mu) * lax.rsqrt(var + LNX_EPS)).reshape(B, T, D_A).astype(p_a.dtype) * lnx_g + lnx_b
    bonus = (jnp.sum(r_h * k_h * r_k.astype(f32), -1, keepdims=True) * v_h).reshape(B, T, D_A)
    out = (yn + bonus.astype(p_a.dtype)) * g
    return out, p_a[:, -1:], s_last.astype(wkv_prev.dtype)


def mla_project(p_b, pos, q_norm_g, w_uq, kv_norm_g):
    B, T, _ = p_b.shape
    cq = rms_norm(p_b[..., :Q_LORA], q_norm_g)
    ckv = rms_norm(p_b[..., Q_LORA:Q_LORA + KV_LORA], kv_norm_g)
    kpe = apply_rope(p_b[..., Q_LORA + KV_LORA:], pos)
    q = (cq @ w_uq).reshape(B, T, B_HEADS, NOPE_DIM + ROPE_DIM)
    return q[..., :NOPE_DIM], apply_rope(q[..., NOPE_DIM:], pos), ckv, kpe


def mla_expand(ckv, w_ukv):
    B, S, _ = ckv.shape
    kv = (ckv @ w_ukv).reshape(B, S, B_HEADS, NOPE_DIM + V_DIM)
    return kv[..., :NOPE_DIM], kv[..., NOPE_DIM:]


def attend(q_nope, q_pe, k_nope, k_pe, v, allowed):
    s = jnp.einsum('bqhd,bkhd->bhqk', q_nope, k_nope) + jnp.einsum('bqhr,bkr->bhqk', q_pe, k_pe)
    s = s.astype(jnp.float32) * ATTN_SCALE
    if allowed is not None:
        s = jnp.where(allowed, s, -jnp.inf)
    p = jax.nn.softmax(s, axis=-1).astype(v.dtype)
    return jnp.einsum('bhqk,bkhd->bqhd', p, v)


def mla_prompt_attention(q_nope, q_pe, k_nope, k_pe, v):
    B, S = q_nope.shape[:2]
    nb = S // Q_BLOCK
    blocks = lambda t: jnp.moveaxis(t.reshape((B, nb, Q_BLOCK) + t.shape[2:]), 1, 0)
    key_chunk = jnp.arange(S) // CHUNK

    def one_block(args):
        i, qn, qp = args
        q_chunk = (i * Q_BLOCK + jnp.arange(Q_BLOCK)) // CHUNK
        allowed = key_chunk[None, :] <= q_chunk[:, None]
        return attend(qn, qp, k_nope, k_pe, v, allowed)

    out = lax.map(one_block, (jnp.arange(nb), blocks(q_nope), blocks(q_pe)))
    return jnp.moveaxis(out, 0, 1).reshape(B, S, D_B)


def merge_and_channel_mix(x, y_a, y_b, g_cols, b_gate, w_o, ln1_g, ln1_b, w_gu, w_down, ln2_g, ln2_b):
    gates = jax.nn.sigmoid(g_cols + b_gate)
    m = gates[..., :D_MODEL] * y_a + gates[..., D_MODEL:] * y_b
    h = layer_norm(DN_ALPHA * x + m @ w_o, ln1_g, ln1_b)
    gu = h @ w_gu
    f = (jax.nn.silu(gu[..., :D_FF]) * gu[..., D_FF:]) @ w_down
    return layer_norm(DN_ALPHA * h + f, ln2_g, ln2_b)


def setup_inputs(seed: int = 0) -> dict:
    key = jax.random.key(seed)
    ks = iter(jax.random.split(key, 48))
    nrm = lambda shape, scale: jax.random.normal(next(ks), shape, jnp.float32) * scale
    return {
        "x_prompt": nrm((BATCH, SEQ, D_MODEL), 1.0),
        "x_sample": nrm((DEC_BATCH, DEC_SEQ, D_MODEL), 1.0),
        "cache_ckv": nrm((DEC_BATCH, PAST_LEN, KV_LORA), 1.0),
        "cache_kpe": nrm((DEC_BATCH, PAST_LEN, ROPE_DIM), 1.0),
        "state_wkv": nrm((DEC_BATCH, A_HEADS, A_HEAD_DIM, A_HEAD_DIM), 0.5),
        "state_shift": nrm((DEC_BATCH, 1, A_COLS), 1.0),
        "w_in": nrm((D_MODEL, N_IN), D_MODEL ** -0.5),
        "mu_shift": jax.random.uniform(next(ks), (A_COLS,), jnp.float32),
        "w0": nrm((D_A,), 0.5) - 0.5,
        "w_w2": nrm((DECAY_LORA, D_A), 0.1 * DECAY_LORA ** -0.5),
        "a0": nrm((D_A,), 0.5),
        "w_a2": nrm((AAA_LORA, D_A), 0.1 * AAA_LORA ** -0.5),
        "w_g2": nrm((GATE_LORA, D_A), GATE_LORA ** -0.5),
        "k_k": 0.85 + nrm((D_A,), 0.05),
        "k_a": 1.0 + nrm((D_A,), 0.05),
        "r_k": nrm((A_HEADS, A_HEAD_DIM), 0.1),
        "lnx_g": 1.0 + nrm((D_A,), 0.05),
        "lnx_b": nrm((D_A,), 0.02),
        "w_pa": nrm((D_A, D_MODEL), D_A ** -0.5),
        "q_norm_g": 1.0 + nrm((Q_LORA,), 0.05),
        "w_uq": nrm((Q_LORA, B_HEADS * (NOPE_DIM + ROPE_DIM)), Q_LORA ** -0.5),
        "kv_norm_g": 1.0 + nrm((KV_LORA,), 0.05),
        "w_ukv": nrm((KV_LORA, B_HEADS * (NOPE_DIM + V_DIM)), KV_LORA ** -0.5),
        "w_pb": nrm((D_B, D_MODEL), D_B ** -0.5),
        "b_gate": nrm((GATE_COLS,), 0.01),
        "w_o": nrm((D_MODEL, D_MODEL), DN_BETA * D_MODEL ** -0.5),
        "ln1_g": 1.0 + nrm((D_MODEL,), 0.05),
        "ln1_b": nrm((D_MODEL,), 0.02),
        "w_gu": nrm((D_MODEL, 2 * D_FF), D_MODEL ** -0.5),
        "w_down": nrm((D_FF, D_MODEL), DN_BETA * D_FF ** -0.5),
        "ln2_g": 1.0 + nrm((D_MODEL,), 0.05),
        "ln2_b": nrm((D_MODEL,), 0.02),
    }


def reference(x_prompt, x_sample, cache_ckv, cache_kpe, state_wkv, state_shift,
              w_in, mu_shift, w0, w_w2, a0, w_a2, w_g2, k_k, k_a, r_k, lnx_g, lnx_b, w_pa,
              q_norm_g, w_uq, kv_norm_g, w_ukv, w_pb,
              b_gate, w_o, ln1_g, ln1_b, w_gu, w_down, ln2_g, ln2_b):
    rwkv_w = (mu_shift, w0, w_w2, a0, w_a2, w_g2, k_k, k_a, r_k, lnx_g, lnx_b)
    tail_w = (b_gate, w_o, ln1_g, ln1_b, w_gu, w_down, ln2_g, ln2_b)

    y_prompt = x_prompt
    ckv_p = kpe_p = wkv_p = shift_p = None
    for _ in range(DEPTH):
        Bp, Tp, _ = y_prompt.shape
        pa, pb, pg = in_projection(y_prompt, w_in)
        shift0 = jnp.zeros((Bp, 1, A_COLS), y_prompt.dtype)
        wkv0 = jnp.zeros((Bp, A_HEADS, A_HEAD_DIM, A_HEAD_DIM), y_prompt.dtype)
        ya_p, shift_p, wkv_p = rwkv_branch(pa, shift0, wkv0, *rwkv_w)
        pos_p = jnp.arange(Tp)
        qn, qp, ckv_p, kpe_p = mla_project(pb, pos_p, q_norm_g, w_uq, kv_norm_g)
        kn, vv = mla_expand(ckv_p, w_ukv)
        yb_p = mla_prompt_attention(qn, qp, kn, kpe_p, vv)
        y_prompt = merge_and_channel_mix(y_prompt, ya_p @ w_pa, yb_p @ w_pb, pg, *tail_w)

    y_sample = x_sample
    ckv_s = kpe_s = wkv_s = shift_s = None
    for _ in range(DEPTH):
        Bs, Ts, _ = y_sample.shape
        pa, pb, pg = in_projection(y_sample, w_in)
        ya_s, shift_s, wkv_s = rwkv_branch(pa, state_shift, state_wkv, *rwkv_w)
        pos_s = PAST_LEN + jnp.arange(Ts)
        qn, qp, ckv_s, kpe_s = mla_project(pb, pos_s, q_norm_g, w_uq, kv_norm_g)
        ckv_all = jnp.concatenate([cache_ckv.astype(ckv_s.dtype), ckv_s], axis=1)
        kpe_all = jnp.concatenate([cache_kpe.astype(kpe_s.dtype), kpe_s], axis=1)
        kn, vv = mla_expand(ckv_all, w_ukv)
        yb_s = attend(qn, qp, kn, kpe_all, vv, None).reshape(Bs, Ts, D_B)
        y_sample = merge_and_channel_mix(y_sample, ya_s @ w_pa, yb_s @ w_pb, pg, *tail_w)

    return (y_prompt, y_sample, ckv_p, kpe_p, wkv_p, shift_p, ckv_s, kpe_s, wkv_s, shift_s)
```

```python
import functools

import numpy as np
import jax
import jax.numpy as jnp
from jax import lax
from jax.experimental import pallas as pl
from jax.experimental.pallas import tpu as pltpu

D_MODEL = 1024
CHUNK = 64
A_HEADS = 8
A_HEAD_DIM = 64
D_A = 512
DECAY_LORA = 64
AAA_LORA = 64
GATE_LORA = 128
A_COLS = 3 * D_A + DECAY_LORA + AAA_LORA + GATE_LORA
LNX_EPS = A_HEAD_DIM * 1e-5
B_HEADS = 8
Q_LORA = 256
KV_LORA = 128
NOPE_DIM = 64
ROPE_DIM = 32
V_DIM = 64
D_B = 512
B_COLS = Q_LORA + KV_LORA + ROPE_DIM
ROPE_BASE = 10000.0
ATTN_SCALE = (NOPE_DIM + ROPE_DIM) ** -0.5
RMS_EPS = 1e-6
D_FF = 2816
LN_EPS = 1e-5
DN_ALPHA = 2.0 ** 0.25

LANES = 128
QK_PAD = 128
PROJ_COLS = A_COLS + Q_LORA + KV_LORA + LANES
FF_CHUNK = 256
N_FF = D_FF // FF_CHUNK
VMEM_LIMIT = 56 * 1024 * 1024

F32 = jnp.float32
BF16 = jnp.bfloat16


def _dot(a, b):
    return jnp.dot(a, b, preferred_element_type=F32)


def _dot_nt(a, b):
    return lax.dot_general(a, b, (((1,), (1,)), ((), ())), preferred_element_type=F32)


def _bf(x):
    return x.astype(BF16)


def _split2(x):
    hi = x.astype(BF16)
    lo = (x - hi.astype(F32)).astype(BF16)
    return hi, lo


def _split3(x):
    h1 = x.astype(BF16)
    r1 = x - h1.astype(F32)
    h2 = r1.astype(BF16)
    h3 = (r1 - h2.astype(F32)).astype(BF16)
    return h1, h2, h3


def _sigmoid(z):
    return 1.0 / (1.0 + jnp.exp(-z))


def _full(shape):
    n = len(shape)
    return pl.BlockSpec(shape, lambda *_: (0,) * n)


def _proj_kernel(x_ref, shift0_ref, wall_ref, mu_ref, w0_ref, a0_ref, kk_ref, ka_ref, rk_ref,
                 ww2_ref, wa2_ref, wg2_ref, esum_ref, ltri_ref, qg_ref, kvg_ref, wqa_ref, wqb_ref,
                 cq_ref, sq_ref, ck_ref,
                 rt_ref, kt_ref, bt_ref, at_ref, v_ref, g_ref, bonus_ref, wc_ref, shift_ref,
                 q_ref, ckv_ref, kpe_ref, carry_ref, *, rows, chunk):
    b = pl.program_id(1)
    xb = _bf(x_ref[...])

    pa = _dot(xb, wall_ref[:, :A_COLS])
    first = jnp.where(b == 0, shift0_ref[0], carry_ref[...])
    row = lax.broadcasted_iota(jnp.int32, (rows, 1), 0)
    prev = jnp.where(row == 0, first, pltpu.roll(pa, 1, axis=0))
    last = pa[rows - 1:rows, :]
    carry_ref[...] = last
    shift_ref[0] = last
    xs = pa + (prev - pa) * mu_ref[...]

    r = xs[:, :D_A]
    k = xs[:, D_A:2 * D_A]
    v = xs[:, 2 * D_A:3 * D_A]
    wa = xs[:, 3 * D_A:3 * D_A + LANES]
    gd = xs[:, 3 * D_A + LANES:]

    z = w0_ref[...] + _dot(_bf(jnp.tanh(wa)), ww2_ref[...])
    ld = -np.float32(np.exp(-0.5)) * _sigmoid(z)
    a = _sigmoid(a0_ref[...] + _dot(_bf(wa), wa2_ref[...]))
    g_ref[...] = _dot(_bf(_sigmoid(gd)), wg2_ref[...])

    esum = esum_ref[...]

    def headsum(t):
        hi, lo = _split2(t)
        return _dot(hi, esum) + _dot(lo, esum)

    kkr = k * kk_ref[...]
    kk = kkr / jnp.maximum(jnp.sqrt(headsum(kkr * kkr)), 1e-12)
    kh = k * (1.0 + (a - 1.0) * ka_ref[...])
    bonus_ref[...] = headsum(r * kh * rk_ref[...]) * v
    v_ref[...] = v

    ltri = ltri_ref[...]
    h1, h2, h3 = _split3(ld)
    cum = _dot(ltri, h1) + _dot(ltri, h2) + _dot(ltri, h3)
    ep = jnp.exp(cum)
    em = jnp.exp(-cum)
    rt_ref[...] = r * ep
    kt_ref[...] = kh * em
    bt_ref[...] = (kk * a) * em
    at_ref[...] = -kk * jnp.exp(cum - ld)
    for c in range(rows // chunk):
        wc_ref[c * 8:(c + 1) * 8, :] = jnp.broadcast_to(ep[(c + 1) * chunk - 1:(c + 1) * chunk, :], (8, D_A))

    pq = _dot(xb, wall_ref[:, A_COLS:A_COLS + Q_LORA])
    cqn = _bf(pq * lax.rsqrt(jnp.mean(pq * pq, axis=-1, keepdims=True) + RMS_EPS) * qg_ref[...])
    qa = _dot(cqn, wqa_ref[...])
    qb = _dot(cqn, wqb_ref[...])
    cq = cq_ref[...]
    sq = sq_ref[...]
    for h in range(B_HEADS):
        sl = slice(h * QK_PAD, (h + 1) * QK_PAD)
        q_ref[:, sl] = _bf(qa[:, sl] * cq + qb[:, sl] * sq)

    pkv = _dot(xb, wall_ref[:, A_COLS + Q_LORA:A_COLS + Q_LORA + KV_LORA])
    ckv_ref[...] = pkv * lax.rsqrt(jnp.mean(pkv * pkv, axis=-1, keepdims=True) + RMS_EPS) * kvg_ref[...]

    ppe = _dot(xb, wall_ref[:, A_COLS + Q_LORA + KV_LORA:]) * ck_ref[...]
    kpe_ref[...] = ppe[:, :ROPE_DIM] + ppe[:, ROPE_DIM:2 * ROPE_DIM]


def _proj(x2, shift0, w, tabs, *, nstreams, bps, rows, chunk):
    total = nstreams * bps * rows
    nck = rows // chunk
    rowblk = lambda n: pl.BlockSpec((rows, n), lambda s, b: (s * bps + b, 0))
    in_specs = [
        rowblk(D_MODEL),
        pl.BlockSpec((1, 1, A_COLS), lambda s, b: (s, 0, 0)),
        _full((D_MODEL, PROJ_COLS)),
        _full((1, A_COLS)), _full((1, D_A)), _full((1, D_A)), _full((1, D_A)), _full((1, D_A)), _full((1, D_A)),
        _full((LANES, D_A)), _full((LANES, D_A)), _full((GATE_LORA, D_A)),
        _full((D_A, D_A)), _full((rows, rows)),
        _full((1, Q_LORA)), _full((1, KV_LORA)),
        _full((Q_LORA, B_HEADS * QK_PAD)), _full((Q_LORA, B_HEADS * QK_PAD)),
        rowblk(LANES), rowblk(LANES), rowblk(LANES),
    ]
    f32o = lambda n: jax.ShapeDtypeStruct((total, n), F32)
    out_shape = [f32o(D_A)] * 7 + [
        jax.ShapeDtypeStruct((total // chunk * 8, D_A), F32),
        jax.ShapeDtypeStruct((nstreams, 1, A_COLS), F32),
        jax.ShapeDtypeStruct((total, B_HEADS * QK_PAD), BF16),
        f32o(KV_LORA), f32o(ROPE_DIM),
    ]
    out_specs = [rowblk(D_A)] * 7 + [
        pl.BlockSpec((nck * 8, D_A), lambda s, b: (s * bps + b, 0)),
        pl.BlockSpec((1, 1, A_COLS), lambda s, b: (s, 0, 0)),
        rowblk(B_HEADS * QK_PAD), rowblk(KV_LORA), rowblk(ROPE_DIM),
    ]
    ltri = _chunk_tri(rows, chunk)
    return pl.pallas_call(
        functools.partial(_proj_kernel, rows=rows, chunk=chunk),
        out_shape=out_shape,
        grid=(nstreams, bps),
        in_specs=in_specs,
        out_specs=out_specs,
        scratch_shapes=[pltpu.VMEM((1, A_COLS), F32)],
        compiler_params=pltpu.CompilerParams(
            dimension_semantics=("arbitrary", "arbitrary"), vmem_limit_bytes=VMEM_LIMIT),
        name="proj",
    )(x2, shift0, w["wall"], w["mu"], w["w0"], w["a0"], w["k_k"], w["k_a"], w["r_k"],
      w["ww2"], w["wa2"], w["wg2"], w["esum"], ltri, w["qg"], w["kvg"], w["wqa"], w["wqb"],
      tabs["cq"], tabs["sq"], tabs["ck"])


def _chunk_tri(rows, chunk):
    i = np.arange(rows)
    m = (i[:, None] // chunk == i[None, :] // chunk) & (i[None, :] <= i[:, None])
    return jnp.asarray(m, BF16)


def _kvexp_kernel(ckv_ref, kpe_ref, wk_ref, place_ref, wv_ref, k_ref, v_ref):
    c = _bf(ckv_ref[...])
    k_ref[...] = _bf(_dot(c, wk_ref[...]) + _dot(_bf(kpe_ref[...]), place_ref[...]))
    v_ref[...] = _bf(_dot(c, wv_ref[...]))


def _kvexp(ckv, kpe, w, *, rows):
    total = ckv.shape[0]
    rowblk = lambda n: pl.BlockSpec((rows, n), lambda i: (i, 0))
    return pl.pallas_call(
        _kvexp_kernel,
        out_shape=[jax.ShapeDtypeStruct((total, B_HEADS * QK_PAD), BF16),
                   jax.ShapeDtypeStruct((total, D_B), BF16)],
        grid=(total // rows,),
        in_specs=[rowblk(KV_LORA), rowblk(ROPE_DIM), _full((KV_LORA, B_HEADS * QK_PAD)),
                  _full((ROPE_DIM, B_HEADS * QK_PAD)), _full((KV_LORA, D_B))],
        out_specs=[rowblk(B_HEADS * QK_PAD), rowblk(D_B)],
        compiler_params=pltpu.CompilerParams(dimension_semantics=("arbitrary",)),
        name="kvexp",
    )(ckv, kpe, w["wk"], w["place"], w["wv"])


def _attn_kernel(q_ref, k_ref, v_ref, o_ref, m_ref, l_ref, acc_ref, *, causal, bq, bk, nk, kv_len):
    qi = pl.program_id(1)
    ki = pl.program_id(2)

    @pl.when(ki == 0)
    def _():
        m_ref[...] = jnp.full(m_ref.shape, -jnp.inf, F32)
        l_ref[...] = jnp.zeros(l_ref.shape, F32)
        acc_ref[...] = jnp.zeros(acc_ref.shape, F32)

    low = lax.broadcasted_iota(jnp.int32, (1, LANES), 1) < V_DIM

    def step(mask):
        for j in range(B_HEADS // 2):
            vp = v_ref[0, :, j * LANES:(j + 1) * LANES]
            pv = []
            al = []
            for hh in range(2):
                h = 2 * j + hh
                sl = slice(h * QK_PAD, (h + 1) * QK_PAD)
                s = _dot_nt(q_ref[0, :, sl], k_ref[0, :, sl])
                if mask is not None:
                    s = jnp.where(mask, s, -jnp.inf)
                m_prev = m_ref[h]
                m_new = jnp.maximum(m_prev, jnp.max(s, axis=-1, keepdims=True))
                alpha = jnp.exp(m_prev - m_new)
                p = jnp.exp(s - m_new[:, :1])
                l_ref[h] = alpha * l_ref[h] + jnp.sum(p, axis=-1, keepdims=True)
                m_ref[h] = m_new
                pv.append(_dot(_bf(p), vp))
                al.append(alpha)
            acc_ref[j] = jnp.where(low, al[0], al[1]) * acc_ref[j] + jnp.where(low, pv[0], pv[1])

    if causal:
        @pl.when(ki < qi)
        def _():
            step(None)

        @pl.when(ki == qi)
        def _():
            rq = lax.broadcasted_iota(jnp.int32, (bq, bk), 0) // CHUNK
            ck = lax.broadcasted_iota(jnp.int32, (bq, bk), 1) // CHUNK
            step(ck <= rq)
        is_last = ki == qi
    else:
        if kv_len < nk * bk:
            col = lax.broadcasted_iota(jnp.int32, (bq, bk), 1) + ki * bk
            step(col < kv_len)
        else:
            step(None)
        is_last = ki == nk - 1

    @pl.when(is_last)
    def _():
        for j in range(B_HEADS // 2):
            inv = jnp.where(low, 1.0 / l_ref[2 * j], 1.0 / l_ref[2 * j + 1])
            o_ref[0, :, j * LANES:(j + 1) * LANES] = acc_ref[j] * inv


def _attn(q, k, v, *, causal, bq, bk, kv_len):
    nb, tq, _ = q.shape
    tk = k.shape[1]
    nq, nk = tq // bq, tk // bk
    if causal:
        kmap = lambda b, i, j: (b, jnp.minimum(i, j), 0)
    else:
        kmap = lambda b, i, j: (b, j, 0)
    return pl.pallas_call(
        functools.partial(_attn_kernel, causal=causal, bq=bq, bk=bk, nk=nk, kv_len=kv_len),
        out_shape=jax.ShapeDtypeStruct((nb, tq, D_B), F32),
        grid=(nb, nq, nk),
        in_specs=[pl.BlockSpec((1, bq, B_HEADS * QK_PAD), lambda b, i, j: (b, i, 0)),
                  pl.BlockSpec((1, bk, B_HEADS * QK_PAD), kmap),
                  pl.BlockSpec((1, bk, D_B), kmap)],
        out_specs=pl.BlockSpec((1, bq, D_B), lambda b, i, j: (b, i, 0)),
        scratch_shapes=[pltpu.VMEM((B_HEADS, bq, LANES), F32),
                        pltpu.VMEM((B_HEADS, bq, LANES), F32),
                        pltpu.VMEM((B_HEADS // 2, bq, LANES), F32)],
        compiler_params=pltpu.CompilerParams(
            dimension_semantics=("arbitrary", "arbitrary", "arbitrary"), vmem_limit_bytes=VMEM_LIMIT),
        name="attn",
    )(q, k, v)


def _wkv_kernel(rt_ref, kt_ref, bt_ref, at_ref, v_ref, g_ref, bonus_ref, wc_ref, lg_ref, lb_ref,
                eavg_ref, h0_ref, y_ref, hout_ref, h_ref, *, chunk, ncs):
    c = pl.program_id(1)
    C2 = 2 * chunk

    @pl.when(c == 0)
    def _():
        h_ref[...] = h0_ref[0]

    low = lax.broadcasted_iota(jnp.int32, (chunk, LANES), 1) < A_HEAD_DIM
    ii = lax.broadcasted_iota(jnp.int32, (C2, C2), 0)
    jj = lax.broadcasted_iota(jnp.int32, (C2, C2), 1)
    strict = ii > jj
    incl = ii >= jj
    eye_c = (ii == jj).astype(F32)
    ki = lax.broadcasted_iota(jnp.int32, (LANES, LANES), 0)
    kj = lax.broadcasted_iota(jnp.int32, (LANES, LANES), 1)
    eye_k = (ki == kj).astype(F32)
    eavg = eavg_ref[...]

    def stack(t):
        return jnp.concatenate([jnp.where(low, t, 0.0), jnp.where(low, 0.0, t)], axis=0)

    for j in range(A_HEADS // 2):
        sl = slice(j * LANES, (j + 1) * LANES)
        At, Bt, Kt, Rt, Vs = (stack(ref[:, sl]) for ref in (at_ref, bt_ref, kt_ref, rt_ref, v_ref))
        Vb = _bf(Vs)
        g1 = _dot_nt(_bf(jnp.concatenate([At, Rt], axis=0)), _bf(jnp.concatenate([Bt, Kt], axis=0)))
        Aab = jnp.where(strict, g1[:C2, :C2], 0.0)
        Aak = jnp.where(strict, g1[:C2, C2:], 0.0)
        Arb = jnp.where(incl, g1[C2:, :C2], 0.0)
        Ark = jnp.where(incl, g1[C2:, C2:], 0.0)
        Tm = eye_c + Aab
        Pw = Aab
        n = 1
        while 2 * n < chunk:
            Pb = _bf(Pw)
            Pw = _dot(Pb, Pb)
            Tm = Tm + _dot(_bf(Tm), _bf(Pw))
            n *= 2
        akv = _dot(_bf(Aak), Vb)
        PP = _dot(_bf(Tm), _bf(jnp.concatenate([At, akv], axis=1)))
        PPb = _bf(PP)
        QQ = _dot(_bf(Arb), PPb)
        Q1s = Rt + QQ[:, :LANES]
        Q2s = QQ[:, LANES:] + _dot(_bf(Ark), Vb)
        wrow = wc_ref[0:1, sl]
        MM = _dot(_bf((Bt * wrow).T), PPb)
        M1 = eye_k * wrow + MM[:, :LANES]
        M2 = MM[:, LANES:] + _dot(_bf((Kt * wrow).T), Vb)
        Hb = _bf(h_ref[j])
        y = _dot(_bf(Q1s[:chunk] + Q1s[chunk:]), Hb) + (Q2s[:chunk] + Q2s[chunk:])
        h_ref[j] = _dot(_bf(M1), Hb) + M2

        yh, yl = _split2(y)
        mu = _dot(yh, eavg) + _dot(yl, eavg)
        d = y - mu
        dh, dl = _split2(d * d)
        var = _dot(dh, eavg) + _dot(dl, eavg)
        yn = d * lax.rsqrt(var + LNX_EPS) * lg_ref[:, sl] + lb_ref[:, sl]
        y_ref[:, sl] = (yn + bonus_ref[:, sl]) * g_ref[:, sl]

    @pl.when(c == ncs - 1)
    def _():
        hout_ref[0] = h_ref[...]


def _wkv(rt, kt, bt, at, v, g, bonus, wc, h0, w, *, nstreams, ncs, chunk):
    total = rt.shape[0]
    blk = pl.BlockSpec((chunk, D_A), lambda s, c: (s * ncs + c, 0))
    hspec = pl.BlockSpec((1, A_HEADS // 2, LANES, LANES), lambda s, c: (s, 0, 0, 0))
    return pl.pallas_call(
        functools.partial(_wkv_kernel, chunk=chunk, ncs=ncs),
        out_shape=[jax.ShapeDtypeStruct((total, D_A), F32),
                   jax.ShapeDtypeStruct((nstreams, A_HEADS // 2, LANES, LANES), F32)],
        grid=(nstreams, ncs),
        in_specs=[blk] * 7 + [pl.BlockSpec((8, D_A), lambda s, c: (s * ncs + c, 0)),
                              _full((1, D_A)), _full((1, D_A)), _full((LANES, LANES)), hspec],
        out_specs=[blk, hspec],
        scratch_shapes=[pltpu.VMEM((A_HEADS // 2, LANES, LANES), F32)],
        compiler_params=pltpu.CompilerParams(dimension_semantics=("arbitrary", "arbitrary")),
        name="wkv",
    )(rt, kt, bt, at, v, g, bonus, wc, w["lnx_g"], w["lnx_b"], w["eavg"], h0)


def _layer_norm(t, g, b):
    mu = jnp.mean(t, axis=-1, keepdims=True)
    d = t - mu
    var = jnp.mean(d * d, axis=-1, keepdims=True)
    return d * lax.rsqrt(var + LN_EPS) * g + b


def _tail_kernel(x_ref, ya_ref, yb_ref, wg_ref, bg_ref, wpa_ref, wpb_ref, wo_ref, l1g_ref, l1b_ref,
                 wgate_ref, wup_ref, wdown_ref, l2g_ref, l2b_ref, o_ref):
    x = x_ref[...]
    gates = _sigmoid(_dot(_bf(x), wg_ref[...]) + bg_ref[...])
    m = (gates[:, :D_MODEL] * _dot(_bf(ya_ref[...]), wpa_ref[...])
         + gates[:, D_MODEL:] * _dot(_bf(yb_ref[...]), wpb_ref[...]))
    h = _layer_norm(DN_ALPHA * x + _dot(_bf(m), wo_ref[...]), l1g_ref[...], l1b_ref[...])
    hb = _bf(h)

    def ff(c, acc):
        gate = _dot(hb, wgate_ref[c])
        up = _dot(hb, wup_ref[c])
        return acc + _dot(_bf(gate * _sigmoid(gate) * up), wdown_ref[c])

    f = lax.fori_loop(0, N_FF, ff, jnp.zeros(h.shape, F32))
    o_ref[...] = _layer_norm(DN_ALPHA * h + f, l2g_ref[...], l2b_ref[...])


def _tail(x2, ya, yb, w, *, rows):
    total = x2.shape[0]
    rowblk = lambda n: pl.BlockSpec((rows, n), lambda i: (i, 0))

    def const(shape):
        n = len(shape)
        return pl.BlockSpec(shape, lambda i: (0,) * n, pipeline_mode=pl.Buffered(1))

    return pl.pallas_call(
        _tail_kernel,
        out_shape=jax.ShapeDtypeStruct((total, D_MODEL), F32),
        grid=(total // rows,),
        in_specs=[rowblk(D_MODEL), rowblk(D_A), rowblk(D_B),
                  const((D_MODEL, 2 * D_MODEL)), const((1, 2 * D_MODEL)),
                  const((D_A, D_MODEL)), const((D_B, D_MODEL)), const((D_MODEL, D_MODEL)),
                  const((1, D_MODEL)), const((1, D_MODEL)),
                  const((N_FF, D_MODEL, FF_CHUNK)), const((N_FF, D_MODEL, FF_CHUNK)),
                  const((N_FF, FF_CHUNK, D_MODEL)),
                  const((1, D_MODEL)), const((1, D_MODEL))],
        out_specs=rowblk(D_MODEL),
        compiler_params=pltpu.CompilerParams(
            dimension_semantics=("arbitrary",), vmem_limit_bytes=VMEM_LIMIT),
        name="tail",
    )(x2, ya, yb, w["wg"], w["bg"], w["wpa"], w["wpb"], w["wo"], w["l1g"], w["l1b"],
      w["wgate"], w["wup"], w["wdown"], w["l2g"], w["l2b"])


def _prep_weights(w_in, mu_shift, w0, w_w2, a0, w_a2, w_g2, k_k, k_a, r_k, lnx_g, lnx_b, w_pa,
                  q_norm_g, w_uq, kv_norm_g, w_ukv, w_pb, b_gate, w_o, ln1_g, ln1_b, w_gu, w_down,
                  ln2_g, ln2_b):
    row = lambda t: t.reshape(1, -1).astype(F32)
    nb = A_COLS + B_COLS
    pe = w_in[:, nb - ROPE_DIM:nb]
    half = ROPE_DIM // 2
    pe_sw = jnp.concatenate([pe[:, half:], pe[:, :half]], axis=1)
    wall = jnp.concatenate([w_in[:, :nb - ROPE_DIM], pe, pe_sw,
                            jnp.zeros((D_MODEL, LANES - 2 * ROPE_DIM), F32)], axis=1)
    uq = w_uq.reshape(Q_LORA, B_HEADS, NOPE_DIM + ROPE_DIM)
    nope, r1, r2 = uq[..., :NOPE_DIM], uq[..., NOPE_DIM:NOPE_DIM + half], uq[..., NOPE_DIM + half:]
    zpad = jnp.zeros((Q_LORA, B_HEADS, QK_PAD - NOPE_DIM - ROPE_DIM), F32)
    wqa = jnp.concatenate([nope, r1, r2, zpad], axis=-1).reshape(Q_LORA, B_HEADS * QK_PAD)
    wqb = jnp.concatenate([jnp.zeros_like(nope), r2, r1, zpad], axis=-1).reshape(Q_LORA, B_HEADS * QK_PAD)
    ukv = w_ukv.reshape(KV_LORA, B_HEADS, NOPE_DIM + V_DIM)
    wk = jnp.concatenate([ukv[..., :NOPE_DIM], jnp.zeros((KV_LORA, B_HEADS, QK_PAD - NOPE_DIM), F32)],
                         axis=-1).reshape(KV_LORA, B_HEADS * QK_PAD)
    wv = ukv[..., NOPE_DIM:].reshape(KV_LORA, D_B)
    place = np.zeros((ROPE_DIM, B_HEADS * QK_PAD), np.float32)
    for h in range(B_HEADS):
        place[np.arange(ROPE_DIM), h * QK_PAD + NOPE_DIM + np.arange(ROPE_DIM)] = 1.0
    hid = np.arange(D_A) // A_HEAD_DIM
    esum = (hid[:, None] == hid[None, :]).astype(np.float32)
    lid = np.arange(LANES) // A_HEAD_DIM
    eavg = (lid[:, None] == lid[None, :]).astype(np.float32) / A_HEAD_DIM
    zl = jnp.zeros((LANES - DECAY_LORA, D_A), F32)
    return {
        "wall": _bf(wall), "mu": row(mu_shift), "w0": row(w0), "a0": row(a0), "k_k": row(k_k),
        "k_a": row(k_a), "r_k": row(r_k),
        "ww2": _bf(jnp.concatenate([w_w2, zl], axis=0)), "wa2": _bf(jnp.concatenate([zl, w_a2], axis=0)),
        "wg2": _bf(w_g2), "esum": jnp.asarray(esum, BF16), "eavg": jnp.asarray(eavg, BF16),
        "qg": row(q_norm_g), "kvg": row(kv_norm_g), "wqa": _bf(wqa), "wqb": _bf(wqb),
        "wk": _bf(wk), "wv": _bf(wv), "place": jnp.asarray(place, BF16),
        "lnx_g": row(lnx_g), "lnx_b": row(lnx_b),
        "wg": _bf(w_in[:, nb:]), "bg": row(b_gate), "wpa": _bf(w_pa), "wpb": _bf(w_pb), "wo": _bf(w_o),
        "l1g": row(ln1_g), "l1b": row(ln1_b), "l2g": row(ln2_g), "l2b": row(ln2_b),
        "wgate": _bf(w_gu[:, :D_FF].reshape(D_MODEL, N_FF, FF_CHUNK).transpose(1, 0, 2)),
        "wup": _bf(w_gu[:, D_FF:].reshape(D_MODEL, N_FF, FF_CHUNK).transpose(1, 0, 2)),
        "wdown": _bf(w_down.reshape(N_FF, FF_CHUNK, D_MODEL)),
    }


def _rope_tables(pos, reps):
    half = ROPE_DIM // 2
    inv = ROPE_BASE ** (-jnp.arange(half, dtype=F32) / half)
    ang = pos.astype(F32)[:, None] * inv
    cos, sin = jnp.cos(ang), jnp.sin(ang)
    n = pos.shape[0]
    sc = np.float32(ATTN_SCALE)
    cq = jnp.concatenate([jnp.full((n, NOPE_DIM), sc, F32), cos * sc, cos * sc,
                          jnp.zeros((n, QK_PAD - NOPE_DIM - ROPE_DIM), F32)], axis=1)
    sq = jnp.concatenate([jnp.zeros((n, NOPE_DIM), F32), -sin * sc, sin * sc,
                          jnp.zeros((n, QK_PAD - NOPE_DIM - ROPE_DIM), F32)], axis=1)
    ck = jnp.concatenate([cos, cos, -sin, sin, jnp.zeros((n, LANES - 2 * ROPE_DIM), F32)], axis=1)
    tile = lambda t: jnp.tile(t, (reps, 1))
    return {"cq": tile(cq), "sq": tile(sq), "ck": tile(ck)}


def _state_to_pairs(s):
    nb = s.shape[0]
    st = jnp.swapaxes(s, -1, -2).reshape(nb, A_HEADS // 2, 2, A_HEAD_DIM, A_HEAD_DIM)
    eye = jnp.eye(2, dtype=s.dtype)
    return jnp.einsum("sjakv,ab->sjakbv", st, eye).reshape(nb, A_HEADS // 2, LANES, LANES)


def _pairs_to_state(hp):
    nb = hp.shape[0]
    h6 = hp.reshape(nb, A_HEADS // 2, 2, A_HEAD_DIM, 2, A_HEAD_DIM)
    diag = jnp.stack([h6[:, :, 0, :, 0, :], h6[:, :, 1, :, 1, :]], axis=2)
    return jnp.swapaxes(diag.reshape(nb, A_HEADS, A_HEAD_DIM, A_HEAD_DIM), -1, -2)


def _layer(x, pos0, shift0, wkv0, cache, w):
    nstreams, t, _ = x.shape
    total = nstreams * t
    x2 = x.reshape(total, D_MODEL)
    chunk = min(CHUNK, t)
    rows = min(256, t)
    bps = t // rows
    tabs = _rope_tables(pos0 + jnp.arange(t), nstreams)
    (rt, kt, bt, at, v, g, bonus, wc, shift, q, ckv, kpe) = _proj(
        x2, shift0, w, tabs, nstreams=nstreams, bps=bps, rows=rows, chunk=chunk)

    ya, hout = _wkv(rt, kt, bt, at, v, g, bonus, wc, _state_to_pairs(wkv0), w,
                    nstreams=nstreams, ncs=t // chunk, chunk=chunk)

    if cache is None:
        kk, vv = _kvexp(ckv, kpe, w, rows=min(512, total))
        blk = min(512, t)
        yb = _attn(q.reshape(nstreams, t, -1), kk.reshape(nstreams, t, -1), vv.reshape(nstreams, t, -1),
                   causal=True, bq=blk, bk=blk, kv_len=t)
    else:
        cache_ckv, cache_kpe = cache
        past = cache_ckv.shape[1]
        kv_len = past + t
        tk = -(-kv_len // LANES) * LANES
        cat = lambda c, n, d: jnp.concatenate(
            [c.astype(F32), n.reshape(nstreams, t, d), jnp.zeros((nstreams, tk - kv_len, d), F32)],
            axis=1).reshape(nstreams * tk, d)
        kk, vv = _kvexp(cat(cache_ckv, ckv, KV_LORA), cat(cache_kpe, kpe, ROPE_DIM), w, rows=512)
        yb = _attn(q.reshape(nstreams, t, -1), kk.reshape(nstreams, tk, -1), vv.reshape(nstreams, tk, -1),
                   causal=False, bq=t, bk=tk, kv_len=kv_len)

    y = _tail(x2, ya, yb.reshape(total, D_B), w, rows=min(256, total))
    return (y.reshape(nstreams, t, D_MODEL), ckv.reshape(nstreams, t, KV_LORA),
            kpe.reshape(nstreams, t, ROPE_DIM), _pairs_to_state(hout), shift)


def kernel(x_prompt, x_sample, cache_ckv, cache_kpe, state_wkv, state_shift, w_in, mu_shift, w0, w_w2, a0,
           w_a2, w_g2, k_k, k_a, r_k, lnx_g, lnx_b, w_pa, q_norm_g, w_uq, kv_norm_g, w_ukv, w_pb, b_gate,
           w_o, ln1_g, ln1_b, w_gu, w_down, ln2_g, ln2_b):
    w = _prep_weights(w_in, mu_shift, w0, w_w2, a0, w_a2, w_g2, k_k, k_a, r_k, lnx_g, lnx_b, w_pa,
                      q_norm_g, w_uq, kv_norm_g, w_ukv, w_pb, b_gate, w_o, ln1_g, ln1_b, w_gu, w_down,
                      ln2_g, ln2_b)
    bp = x_prompt.shape[0]
    y_p, ckv_p, kpe_p, wkv_p, shift_p = _layer(
        x_prompt, 0, jnp.zeros((bp, 1, A_COLS), F32),
        jnp.zeros((bp, A_HEADS, A_HEAD_DIM, A_HEAD_DIM), F32), None, w)
    y_s, ckv_s, kpe_s, wkv_s, shift_s = _layer(
        x_sample, cache_ckv.shape[1], state_shift, state_wkv, (cache_ckv, cache_kpe), w)
    return (y_p, y_s, ckv_p, kpe_p, wkv_p, shift_p, ckv_s, kpe_s, wkv_s, shift_s)
```

```python
import functools

import numpy as np
import jax
import jax.numpy as jnp
from jax import lax
from jax.experimental import pallas as pl
from jax.experimental.pallas import tpu as pltpu

D_MODEL = 1024
CHUNK = 64
A_HEADS = 8
A_HEAD_DIM = 64
D_A = 512
DECAY_LORA = 64
AAA_LORA = 64
GATE_LORA = 128
A_COLS = 3 * D_A + DECAY_LORA + AAA_LORA + GATE_LORA
LNX_EPS = A_HEAD_DIM * 1e-5
B_HEADS = 8
Q_LORA = 256
KV_LORA = 128
NOPE_DIM = 64
ROPE_DIM = 32
V_DIM = 64
D_B = 512
B_COLS = Q_LORA + KV_LORA + ROPE_DIM
ROPE_BASE = 10000.0
ATTN_SCALE = (NOPE_DIM + ROPE_DIM) ** -0.5
RMS_EPS = 1e-6
D_FF = 2816
LN_EPS = 1e-5
DN_ALPHA = 2.0 ** 0.25

LANES = 128
QK_PAD = 128
PROJ_COLS = A_COLS + Q_LORA + KV_LORA + LANES
FF_CHUNK = 256
N_FF = D_FF // FF_CHUNK
ATTN_BQ = 1024
ATTN_BK = 512
WKV_GROUP = 4
VMEM_LIMIT = 56 * 1024 * 1024

F32 = jnp.float32
BF16 = jnp.bfloat16


def _dot(a, b):
    return jnp.dot(a, b, preferred_element_type=F32)


def _dot_nt(a, b):
    return lax.dot_general(a, b, (((1,), (1,)), ((), ())), preferred_element_type=F32)


def _bf(x):
    return x.astype(BF16)


def _split2(x):
    hi = x.astype(BF16)
    lo = (x - hi.astype(F32)).astype(BF16)
    return hi, lo


def _split3(x):
    h1 = x.astype(BF16)
    r1 = x - h1.astype(F32)
    h2 = r1.astype(BF16)
    h3 = (r1 - h2.astype(F32)).astype(BF16)
    return h1, h2, h3


def _sigmoid(z):
    return 1.0 / (1.0 + jnp.exp(-z))


def _full(shape):
    n = len(shape)
    return pl.BlockSpec(shape, lambda *_: (0,) * n)


def _proj_kernel(x_ref, shift0_ref, wall_ref, mu_ref, w0_ref, a0_ref, kk_ref, ka_ref, rk_ref,
                 ww2_ref, wa2_ref, wg2_ref, esum_ref, ltri_ref, qg_ref, kvg_ref, wqa_ref, wqb_ref,
                 cq_ref, sq_ref, ck_ref,
                 rt_ref, kt_ref, bt_ref, at_ref, v_ref, g_ref, bonus_ref, wc_ref, shift_ref,
                 q_ref, ckv_ref, kpe_ref, carry_ref, *, rows, chunk):
    b = pl.program_id(1)
    xb = _bf(x_ref[...])

    pa = _dot(xb, wall_ref[:, :A_COLS])
    first = jnp.where(b == 0, shift0_ref[0], carry_ref[...])
    row = lax.broadcasted_iota(jnp.int32, (rows, 1), 0)
    prev = jnp.where(row == 0, first, pltpu.roll(pa, 1, axis=0))
    last = pa[rows - 1:rows, :]
    carry_ref[...] = last
    shift_ref[0] = last
    xs = pa + (prev - pa) * mu_ref[...]

    r = xs[:, :D_A]
    k = xs[:, D_A:2 * D_A]
    v = xs[:, 2 * D_A:3 * D_A]
    wa = xs[:, 3 * D_A:3 * D_A + LANES]
    gd = xs[:, 3 * D_A + LANES:]

    z = w0_ref[...] + _dot(_bf(jnp.tanh(wa)), ww2_ref[...])
    ld = -np.float32(np.exp(-0.5)) * _sigmoid(z)
    a = _sigmoid(a0_ref[...] + _dot(_bf(wa), wa2_ref[...]))
    g_ref[...] = _dot(_bf(_sigmoid(gd)), wg2_ref[...])

    esum = esum_ref[...]

    def headsum(t):
        hi, lo = _split2(t)
        return _dot(hi, esum) + _dot(lo, esum)

    kkr = k * kk_ref[...]
    kk = kkr / jnp.maximum(jnp.sqrt(headsum(kkr * kkr)), 1e-12)
    kh = k * (1.0 + (a - 1.0) * ka_ref[...])
    bonus_ref[...] = headsum(r * kh * rk_ref[...]) * v
    v_ref[...] = v

    ltri = ltri_ref[...]
    h1, h2, h3 = _split3(ld)
    cum = _dot(ltri, h1) + _dot(ltri, h2) + _dot(ltri, h3)
    ep = jnp.exp(cum)
    em = jnp.exp(-cum)
    rt_ref[...] = r * ep
    kt_ref[...] = kh * em
    bt_ref[...] = (kk * a) * em
    at_ref[...] = -kk * jnp.exp(cum - ld)
    for c in range(rows // chunk):
        wc_ref[c * 8:(c + 1) * 8, :] = jnp.broadcast_to(ep[(c + 1) * chunk - 1:(c + 1) * chunk, :], (8, D_A))

    pq = _dot(xb, wall_ref[:, A_COLS:A_COLS + Q_LORA])
    cqn = _bf(pq * lax.rsqrt(jnp.mean(pq * pq, axis=-1, keepdims=True) + RMS_EPS) * qg_ref[...])
    qa = _dot(cqn, wqa_ref[...])
    qb = _dot(cqn, wqb_ref[...])
    cq = cq_ref[...]
    sq = sq_ref[...]
    for h in range(B_HEADS):
        sl = slice(h * QK_PAD, (h + 1) * QK_PAD)
        q_ref[:, sl] = _bf(qa[:, sl] * cq + qb[:, sl] * sq)

    pkv = _dot(xb, wall_ref[:, A_COLS + Q_LORA:A_COLS + Q_LORA + KV_LORA])
    ckv_ref[...] = pkv * lax.rsqrt(jnp.mean(pkv * pkv, axis=-1, keepdims=True) + RMS_EPS) * kvg_ref[...]

    ppe = _dot(xb, wall_ref[:, A_COLS + Q_LORA + KV_LORA:]) * ck_ref[...]
    kpe_ref[...] = ppe[:, :ROPE_DIM] + ppe[:, ROPE_DIM:2 * ROPE_DIM]


def _proj(x2, shift0, w, tabs, *, nstreams, bps, rows, chunk):
    total = nstreams * bps * rows
    nck = rows // chunk
    rowblk = lambda n: pl.BlockSpec((rows, n), lambda s, b: (s * bps + b, 0))
    in_specs = [
        rowblk(D_MODEL),
        pl.BlockSpec((1, 1, A_COLS), lambda s, b: (s, 0, 0)),
        _full((D_MODEL, PROJ_COLS)),
        _full((1, A_COLS)), _full((1, D_A)), _full((1, D_A)), _full((1, D_A)), _full((1, D_A)), _full((1, D_A)),
        _full((LANES, D_A)), _full((LANES, D_A)), _full((GATE_LORA, D_A)),
        _full((D_A, D_A)), _full((rows, rows)),
        _full((1, Q_LORA)), _full((1, KV_LORA)),
        _full((Q_LORA, B_HEADS * QK_PAD)), _full((Q_LORA, B_HEADS * QK_PAD)),
        rowblk(LANES), rowblk(LANES), rowblk(LANES),
    ]
    f32o = lambda n: jax.ShapeDtypeStruct((total, n), F32)
    out_shape = [f32o(D_A)] * 7 + [
        jax.ShapeDtypeStruct((total // chunk * 8, D_A), F32),
        jax.ShapeDtypeStruct((nstreams, 1, A_COLS), F32),
        jax.ShapeDtypeStruct((total, B_HEADS * QK_PAD), BF16),
        f32o(KV_LORA), f32o(ROPE_DIM),
    ]
    out_specs = [rowblk(D_A)] * 7 + [
        pl.BlockSpec((nck * 8, D_A), lambda s, b: (s * bps + b, 0)),
        pl.BlockSpec((1, 1, A_COLS), lambda s, b: (s, 0, 0)),
        rowblk(B_HEADS * QK_PAD), rowblk(KV_LORA), rowblk(ROPE_DIM),
    ]
    ltri = _chunk_tri(rows, chunk)
    return pl.pallas_call(
        functools.partial(_proj_kernel, rows=rows, chunk=chunk),
        out_shape=out_shape,
        grid=(nstreams, bps),
        in_specs=in_specs,
        out_specs=out_specs,
        scratch_shapes=[pltpu.VMEM((1, A_COLS), F32)],
        compiler_params=pltpu.CompilerParams(
            dimension_semantics=("arbitrary", "arbitrary"), vmem_limit_bytes=VMEM_LIMIT),
        name="proj",
    )(x2, shift0, w["wall"], w["mu"], w["w0"], w["a0"], w["k_k"], w["k_a"], w["r_k"],
      w["ww2"], w["wa2"], w["wg2"], w["esum"], ltri, w["qg"], w["kvg"], w["wqa"], w["wqb"],
      tabs["cq"], tabs["sq"], tabs["ck"])


def _chunk_tri(rows, chunk):
    i = np.arange(rows)
    m = (i[:, None] // chunk == i[None, :] // chunk) & (i[None, :] <= i[:, None])
    return jnp.asarray(m, BF16)


def _kvexp_kernel(ckv_ref, kpe_ref, wk_ref, place_ref, wv_ref, k_ref, v_ref):
    c = _bf(ckv_ref[...])
    k_ref[...] = _bf(_dot(c, wk_ref[...]) + _dot(_bf(kpe_ref[...]), place_ref[...]))
    v_ref[...] = _bf(_dot(c, wv_ref[...]))


def _kvexp(ckv, kpe, w, *, rows):
    total = ckv.shape[0]
    rowblk = lambda n: pl.BlockSpec((rows, n), lambda i: (i, 0))
    return pl.pallas_call(
        _kvexp_kernel,
        out_shape=[jax.ShapeDtypeStruct((total, B_HEADS * QK_PAD), BF16),
                   jax.ShapeDtypeStruct((total, D_B), BF16)],
        grid=(total // rows,),
        in_specs=[rowblk(KV_LORA), rowblk(ROPE_DIM), _full((KV_LORA, B_HEADS * QK_PAD)),
                  _full((ROPE_DIM, B_HEADS * QK_PAD)), _full((KV_LORA, D_B))],
        out_specs=[rowblk(B_HEADS * QK_PAD), rowblk(D_B)],
        compiler_params=pltpu.CompilerParams(dimension_semantics=("arbitrary",)),
        name="kvexp",
    )(ckv, kpe, w["wk"], w["place"], w["wv"])


def _attn_kernel(qi_ref, ki_ref, q_ref, k_ref, v_ref, o_ref, m_ref, acc_ref, *,
                 causal, bq, bk, nk, kv_len):
    s_id = pl.program_id(1)
    qi = qi_ref[s_id]
    ki = ki_ref[s_id]
    ratio = bq // bk

    @pl.when(ki == 0)
    def _():
        m_ref[...] = jnp.full(m_ref.shape, -jnp.inf, F32)
        acc_ref[...] = jnp.zeros(acc_ref.shape, F32)

    low = lax.broadcasted_iota(jnp.int32, (1, LANES), 1) < V_DIM

    def scores(h):
        sl = slice(h * QK_PAD, (h + 1) * QK_PAD)
        return _dot_nt(q_ref[0, :, sl], k_ref[0, :, sl])

    def step(mask):
        one = jnp.ones((), BF16)
        ahead = 2
        pending = [scores(h) for h in range(ahead)]
        for h in range(B_HEADS):
            s = pending.pop(0)
            if h + ahead < B_HEADS:
                pending.append(scores(h + ahead))
            vp = v_ref[0, :, (h // 2) * LANES:(h // 2 + 1) * LANES]
            vext = jnp.where(low, vp, one) if h % 2 == 0 else jnp.where(low, one, vp)
            if mask is not None:
                s = jnp.where(mask, s, -jnp.inf)
            m_prev = m_ref[h]
            m_new = jnp.maximum(m_prev, jnp.max(s, axis=-1, keepdims=True))
            p = jnp.exp2(s - m_new[:, :1])
            acc_ref[h] = jnp.exp2(m_prev - m_new) * acc_ref[h] + _dot(_bf(p), vext)
            m_ref[h] = m_new

    if causal:
        @pl.when(ki < ratio * qi)
        def _():
            step(None)

        @pl.when(ki >= ratio * qi)
        def _():
            rq = (lax.broadcasted_iota(jnp.int32, (bq, bk), 0) + qi * bq) // CHUNK
            ck = (lax.broadcasted_iota(jnp.int32, (bq, bk), 1) + ki * bk) // CHUNK
            step(ck <= rq)
        is_last = ki == ratio * (qi + 1) - 1
    else:
        if kv_len < nk * bk:
            col = lax.broadcasted_iota(jnp.int32, (bq, bk), 1) + ki * bk
            step(col < kv_len)
        else:
            step(None)
        is_last = ki == nk - 1

    @pl.when(is_last)
    def _():
        for j in range(B_HEADS // 2):
            a0 = acc_ref[2 * j]
            a1 = acc_ref[2 * j + 1]
            num = jnp.where(low, a0, a1)
            den = jnp.where(low, pltpu.roll(a0, V_DIM, axis=1), pltpu.roll(a1, V_DIM, axis=1))
            o_ref[0, :, j * LANES:(j + 1) * LANES] = num / den


def _attn(q, k, v, *, causal, bq, bk, kv_len):
    nb, tq, _ = q.shape
    tk = k.shape[1]
    nq, nk = tq // bq, tk // bk
    ratio = bq // bk
    steps = [(i, j) for i in range(nq) for j in range(ratio * (i + 1) if causal else nk)]
    qi = jnp.asarray([s[0] for s in steps], jnp.int32)
    ki = jnp.asarray([s[1] for s in steps], jnp.int32)
    grid_spec = pltpu.PrefetchScalarGridSpec(
        num_scalar_prefetch=2,
        grid=(nb, len(steps)),
        in_specs=[pl.BlockSpec((1, bq, B_HEADS * QK_PAD), lambda b, s, qi, ki: (b, qi[s], 0)),
                  pl.BlockSpec((1, bk, B_HEADS * QK_PAD), lambda b, s, qi, ki: (b, ki[s], 0)),
                  pl.BlockSpec((1, bk, D_B), lambda b, s, qi, ki: (b, ki[s], 0))],
        out_specs=pl.BlockSpec((1, bq, D_B), lambda b, s, qi, ki: (b, qi[s], 0)),
        scratch_shapes=[pltpu.VMEM((B_HEADS, bq, LANES), F32),
                        pltpu.VMEM((B_HEADS, bq, LANES), F32)])
    return pl.pallas_call(
        functools.partial(_attn_kernel, causal=causal, bq=bq, bk=bk, nk=nk, kv_len=kv_len),
        out_shape=jax.ShapeDtypeStruct((nb, tq, D_B), F32),
        grid_spec=grid_spec,
        compiler_params=pltpu.CompilerParams(
            dimension_semantics=("arbitrary", "arbitrary"), vmem_limit_bytes=VMEM_LIMIT),
        name="attn",
    )(qi, ki, q, k, v)


def _wkv_kernel(rt_ref, kt_ref, bt_ref, at_ref, v_ref, g_ref, bonus_ref, wc_ref, lg_ref, lb_ref,
                eavg_ref, h0_ref, y_ref, hout_ref, h_ref, *, chunk, group, nsteps):
    c = pl.program_id(1)
    C2 = 2 * chunk
    npair = A_HEADS // 2

    @pl.when(c == 0)
    def _():
        h_ref[...] = h0_ref[0]

    low = lax.broadcasted_iota(jnp.int32, (chunk, LANES), 1) < A_HEAD_DIM
    ii = lax.broadcasted_iota(jnp.int32, (C2, C2), 0)
    jj = lax.broadcasted_iota(jnp.int32, (C2, C2), 1)
    strict = ii > jj
    incl = ii >= jj
    eye_c = (ii == jj).astype(F32)
    ki = lax.broadcasted_iota(jnp.int32, (LANES, LANES), 0)
    kj = lax.broadcasted_iota(jnp.int32, (LANES, LANES), 1)
    eye_k = (ki == kj).astype(F32)
    eavg = eavg_ref[...]

    def stack(t):
        return jnp.concatenate([jnp.where(low, t, 0.0), jnp.where(low, 0.0, t)], axis=0)

    units = [(ci, j) for ci in range(group) for j in range(npair)]
    rows = lambda ci: slice(ci * chunk, (ci + 1) * chunk)
    lanes = lambda j: slice(j * LANES, (j + 1) * LANES)
    ld = lambda ref: [stack(ref[rows(ci), lanes(j)]) for ci, j in units]
    At, Bt, Kt, Rt, Vs = ld(at_ref), ld(bt_ref), ld(kt_ref), ld(rt_ref), ld(v_ref)
    nu = range(len(units))
    Vb = [_bf(Vs[u]) for u in nu]
    g1 = [_dot_nt(_bf(jnp.concatenate([At[u], Rt[u]], axis=0)),
                  _bf(jnp.concatenate([Bt[u], Kt[u]], axis=0))) for u in nu]
    Aab = [jnp.where(strict, g1[u][:C2, :C2], 0.0) for u in nu]
    Aak = [_bf(jnp.where(strict, g1[u][:C2, C2:], 0.0)) for u in nu]
    Arb = [_bf(jnp.where(incl, g1[u][C2:, :C2], 0.0)) for u in nu]
    Ark = [_bf(jnp.where(incl, g1[u][C2:, C2:], 0.0)) for u in nu]
    Tm = [eye_c + Aab[u] for u in nu]
    Pw = Aab
    n = 1
    while 2 * n < chunk:
        Pb = [_bf(Pw[u]) for u in nu]
        Pw = [_dot(Pb[u], Pb[u]) for u in nu]
        Tm = [Tm[u] + _dot(_bf(Tm[u]), _bf(Pw[u])) for u in nu]
        n *= 2
    akv = [_dot(Aak[u], Vb[u]) for u in nu]
    PPb = [_bf(_dot(_bf(Tm[u]), _bf(jnp.concatenate([At[u], akv[u]], axis=1)))) for u in nu]
    QQ = [_dot(Arb[u], PPb[u]) for u in nu]
    arkv = [_dot(Ark[u], Vb[u]) for u in nu]
    wrow = [wc_ref[ci * 8:ci * 8 + 1, lanes(j)] for ci, j in units]
    MM = [_dot(_bf((Bt[u] * wrow[u]).T), PPb[u]) for u in nu]
    ktv = [_dot(_bf((Kt[u] * wrow[u]).T), Vb[u]) for u in nu]
    Q1, Q2, M1, M2 = [], [], [], []
    for u in nu:
        q1s = Rt[u] + QQ[u][:, :LANES]
        q2s = QQ[u][:, LANES:] + arkv[u]
        Q1.append(_bf(q1s[:chunk] + q1s[chunk:]))
        Q2.append(q2s[:chunk] + q2s[chunk:])
        M1.append(_bf(eye_k * wrow[u] + MM[u][:, :LANES]))
        M2.append(MM[u][:, LANES:] + ktv[u])

    H = [h_ref[j] for j in range(npair)]
    Y = []
    for u, (ci, j) in enumerate(units):
        Hb = _bf(H[j])
        Y.append(_dot(Q1[u], Hb) + Q2[u])
        H[j] = _dot(M1[u], Hb) + M2[u]
    for j in range(npair):
        h_ref[j] = H[j]

    ysp = [_split2(Y[u]) for u in nu]
    mu = [_dot(ysp[u][0], eavg) + _dot(ysp[u][1], eavg) for u in nu]
    dv = [Y[u] - mu[u] for u in nu]
    dsp = [_split2(dv[u] * dv[u]) for u in nu]
    var = [_dot(dsp[u][0], eavg) + _dot(dsp[u][1], eavg) for u in nu]
    for u, (ci, j) in enumerate(units):
        yn = dv[u] * lax.rsqrt(var[u] + LNX_EPS) * lg_ref[:, lanes(j)] + lb_ref[:, lanes(j)]
        y_ref[rows(ci), lanes(j)] = (yn + bonus_ref[rows(ci), lanes(j)]) * g_ref[rows(ci), lanes(j)]

    @pl.when(c == nsteps - 1)
    def _():
        hout_ref[0] = h_ref[...]


def _wkv(rt, kt, bt, at, v, g, bonus, wc, h0, w, *, nstreams, ncs, chunk, group):
    total = rt.shape[0]
    nsteps = ncs // group
    blk = pl.BlockSpec((group * chunk, D_A), lambda s, c: (s * nsteps + c, 0))
    hspec = pl.BlockSpec((1, A_HEADS // 2, LANES, LANES), lambda s, c: (s, 0, 0, 0))
    return pl.pallas_call(
        functools.partial(_wkv_kernel, chunk=chunk, group=group, nsteps=nsteps),
        out_shape=[jax.ShapeDtypeStruct((total, D_A), F32),
                   jax.ShapeDtypeStruct((nstreams, A_HEADS // 2, LANES, LANES), F32)],
        grid=(nstreams, nsteps),
        in_specs=[blk] * 7 + [pl.BlockSpec((group * 8, D_A), lambda s, c: (s * nsteps + c, 0)),
                              _full((1, D_A)), _full((1, D_A)), _full((LANES, LANES)), hspec],
        out_specs=[blk, hspec],
        scratch_shapes=[pltpu.VMEM((A_HEADS // 2, LANES, LANES), F32)],
        compiler_params=pltpu.CompilerParams(
            dimension_semantics=("arbitrary", "arbitrary"), vmem_limit_bytes=VMEM_LIMIT),
        name="wkv",
    )(rt, kt, bt, at, v, g, bonus, wc, w["lnx_g"], w["lnx_b"], w["eavg"], h0)


def _layer_norm(t, g, b):
    mu = jnp.mean(t, axis=-1, keepdims=True)
    d = t - mu
    var = jnp.mean(d * d, axis=-1, keepdims=True)
    return d * lax.rsqrt(var + LN_EPS) * g + b


def _tail_kernel(x_ref, ya_ref, yb_ref, wg_ref, bg_ref, wpa_ref, wpb_ref, wo_ref, l1g_ref, l1b_ref,
                 wgate_ref, wup_ref, wdown_ref, l2g_ref, l2b_ref, o_ref):
    x = x_ref[...]
    gates = _sigmoid(_dot(_bf(x), wg_ref[...]) + bg_ref[...])
    m = (gates[:, :D_MODEL] * _dot(_bf(ya_ref[...]), wpa_ref[...])
         + gates[:, D_MODEL:] * _dot(_bf(yb_ref[...]), wpb_ref[...]))
    h = _layer_norm(DN_ALPHA * x + _dot(_bf(m), wo_ref[...]), l1g_ref[...], l1b_ref[...])
    hb = _bf(h)

    def ff(c, acc):
        gate = _dot(hb, wgate_ref[c])
        up = _dot(hb, wup_ref[c])
        return acc + _dot(_bf(gate * _sigmoid(gate) * up), wdown_ref[c])

    f = lax.fori_loop(0, N_FF, ff, jnp.zeros(h.shape, F32))
    o_ref[...] = _layer_norm(DN_ALPHA * h + f, l2g_ref[...], l2b_ref[...])


def _tail(x2, ya, yb, w, *, rows):
    total = x2.shape[0]
    rowblk = lambda n: pl.BlockSpec((rows, n), lambda i: (i, 0))

    def const(shape):
        n = len(shape)
        return pl.BlockSpec(shape, lambda i: (0,) * n, pipeline_mode=pl.Buffered(1))

    return pl.pallas_call(
        _tail_kernel,
        out_shape=jax.ShapeDtypeStruct((total, D_MODEL), F32),
        grid=(total // rows,),
        in_specs=[rowblk(D_MODEL), rowblk(D_A), rowblk(D_B),
                  const((D_MODEL, 2 * D_MODEL)), const((1, 2 * D_MODEL)),
                  const((D_A, D_MODEL)), const((D_B, D_MODEL)), const((D_MODEL, D_MODEL)),
                  const((1, D_MODEL)), const((1, D_MODEL)),
                  const((N_FF, D_MODEL, FF_CHUNK)), const((N_FF, D_MODEL, FF_CHUNK)),
                  const((N_FF, FF_CHUNK, D_MODEL)),
                  const((1, D_MODEL)), const((1, D_MODEL))],
        out_specs=rowblk(D_MODEL),
        compiler_params=pltpu.CompilerParams(
            dimension_semantics=("arbitrary",), vmem_limit_bytes=VMEM_LIMIT),
        name="tail",
    )(x2, ya, yb, w["wg"], w["bg"], w["wpa"], w["wpb"], w["wo"], w["l1g"], w["l1b"],
      w["wgate"], w["wup"], w["wdown"], w["l2g"], w["l2b"])


def _prep_weights(w_in, mu_shift, w0, w_w2, a0, w_a2, w_g2, k_k, k_a, r_k, lnx_g, lnx_b, w_pa,
                  q_norm_g, w_uq, kv_norm_g, w_ukv, w_pb, b_gate, w_o, ln1_g, ln1_b, w_gu, w_down,
                  ln2_g, ln2_b):
    row = lambda t: t.reshape(1, -1).astype(F32)
    nb = A_COLS + B_COLS
    pe = w_in[:, nb - ROPE_DIM:nb]
    half = ROPE_DIM // 2
    pe_sw = jnp.concatenate([pe[:, half:], pe[:, :half]], axis=1)
    wall = jnp.concatenate([w_in[:, :nb - ROPE_DIM], pe, pe_sw,
                            jnp.zeros((D_MODEL, LANES - 2 * ROPE_DIM), F32)], axis=1)
    uq = w_uq.reshape(Q_LORA, B_HEADS, NOPE_DIM + ROPE_DIM)
    nope, r1, r2 = uq[..., :NOPE_DIM], uq[..., NOPE_DIM:NOPE_DIM + half], uq[..., NOPE_DIM + half:]
    zpad = jnp.zeros((Q_LORA, B_HEADS, QK_PAD - NOPE_DIM - ROPE_DIM), F32)
    wqa = jnp.concatenate([nope, r1, r2, zpad], axis=-1).reshape(Q_LORA, B_HEADS * QK_PAD)
    wqb = jnp.concatenate([jnp.zeros_like(nope), r2, r1, zpad], axis=-1).reshape(Q_LORA, B_HEADS * QK_PAD)
    ukv = w_ukv.reshape(KV_LORA, B_HEADS, NOPE_DIM + V_DIM)
    wk = jnp.concatenate([ukv[..., :NOPE_DIM], jnp.zeros((KV_LORA, B_HEADS, QK_PAD - NOPE_DIM), F32)],
                         axis=-1).reshape(KV_LORA, B_HEADS * QK_PAD)
    wv = ukv[..., NOPE_DIM:].reshape(KV_LORA, D_B)
    place = np.zeros((ROPE_DIM, B_HEADS * QK_PAD), np.float32)
    for h in range(B_HEADS):
        place[np.arange(ROPE_DIM), h * QK_PAD + NOPE_DIM + np.arange(ROPE_DIM)] = 1.0
    hid = np.arange(D_A) // A_HEAD_DIM
    esum = (hid[:, None] == hid[None, :]).astype(np.float32)
    lid = np.arange(LANES) // A_HEAD_DIM
    eavg = (lid[:, None] == lid[None, :]).astype(np.float32) / A_HEAD_DIM
    zl = jnp.zeros((LANES - DECAY_LORA, D_A), F32)
    return {
        "wall": _bf(wall), "mu": row(mu_shift), "w0": row(w0), "a0": row(a0), "k_k": row(k_k),
        "k_a": row(k_a), "r_k": row(r_k),
        "ww2": _bf(jnp.concatenate([w_w2, zl], axis=0)), "wa2": _bf(jnp.concatenate([zl, w_a2], axis=0)),
        "wg2": _bf(w_g2), "esum": jnp.asarray(esum, BF16), "eavg": jnp.asarray(eavg, BF16),
        "qg": row(q_norm_g), "kvg": row(kv_norm_g), "wqa": _bf(wqa), "wqb": _bf(wqb),
        "wk": _bf(wk), "wv": _bf(wv), "place": jnp.asarray(place, BF16),
        "lnx_g": row(lnx_g), "lnx_b": row(lnx_b),
        "wg": _bf(w_in[:, nb:]), "bg": row(b_gate), "wpa": _bf(w_pa), "wpb": _bf(w_pb), "wo": _bf(w_o),
        "l1g": row(ln1_g), "l1b": row(ln1_b), "l2g": row(ln2_g), "l2b": row(ln2_b),
        "wgate": _bf(w_gu[:, :D_FF].reshape(D_MODEL, N_FF, FF_CHUNK).transpose(1, 0, 2)),
        "wup": _bf(w_gu[:, D_FF:].reshape(D_MODEL, N_FF, FF_CHUNK).transpose(1, 0, 2)),
        "wdown": _bf(w_down.reshape(N_FF, FF_CHUNK, D_MODEL)),
    }


def _rope_tables(pos, reps):
    half = ROPE_DIM // 2
    inv = ROPE_BASE ** (-jnp.arange(half, dtype=F32) / half)
    ang = pos.astype(F32)[:, None] * inv
    cos, sin = jnp.cos(ang), jnp.sin(ang)
    n = pos.shape[0]
    sc = np.float32(ATTN_SCALE * np.log2(np.e))
    cq = jnp.concatenate([jnp.full((n, NOPE_DIM), sc, F32), cos * sc, cos * sc,
                          jnp.zeros((n, QK_PAD - NOPE_DIM - ROPE_DIM), F32)], axis=1)
    sq = jnp.concatenate([jnp.zeros((n, NOPE_DIM), F32), -sin * sc, sin * sc,
                          jnp.zeros((n, QK_PAD - NOPE_DIM - ROPE_DIM), F32)], axis=1)
    ck = jnp.concatenate([cos, cos, -sin, sin, jnp.zeros((n, LANES - 2 * ROPE_DIM), F32)], axis=1)
    tile = lambda t: jnp.tile(t, (reps, 1))
    return {"cq": tile(cq), "sq": tile(sq), "ck": tile(ck)}


def _state_to_pairs(s):
    nb = s.shape[0]
    st = jnp.swapaxes(s, -1, -2).reshape(nb, A_HEADS // 2, 2, A_HEAD_DIM, A_HEAD_DIM)
    eye = jnp.eye(2, dtype=s.dtype)
    return jnp.einsum("sjakv,ab->sjakbv", st, eye).reshape(nb, A_HEADS // 2, LANES, LANES)


def _pairs_to_state(hp):
    nb = hp.shape[0]
    h6 = hp.reshape(nb, A_HEADS // 2, 2, A_HEAD_DIM, 2, A_HEAD_DIM)
    diag = jnp.stack([h6[:, :, 0, :, 0, :], h6[:, :, 1, :, 1, :]], axis=2)
    return jnp.swapaxes(diag.reshape(nb, A_HEADS, A_HEAD_DIM, A_HEAD_DIM), -1, -2)


def _layer(x, pos0, shift0, wkv0, cache, w):
    nstreams, t, _ = x.shape
    total = nstreams * t
    x2 = x.reshape(total, D_MODEL)
    chunk = min(CHUNK, t)
    rows = min(256, t)
    bps = t // rows
    tabs = _rope_tables(pos0 + jnp.arange(t), nstreams)
    (rt, kt, bt, at, v, g, bonus, wc, shift, q, ckv, kpe) = _proj(
        x2, shift0, w, tabs, nstreams=nstreams, bps=bps, rows=rows, chunk=chunk)

    ya, hout = _wkv(rt, kt, bt, at, v, g, bonus, wc, _state_to_pairs(wkv0), w,
                    nstreams=nstreams, ncs=t // chunk, chunk=chunk, group=min(WKV_GROUP, t // chunk))

    if cache is None:
        kk, vv = _kvexp(ckv, kpe, w, rows=min(512, total))
        yb = _attn(q.reshape(nstreams, t, -1), kk.reshape(nstreams, t, -1), vv.reshape(nstreams, t, -1),
                   causal=True, bq=min(ATTN_BQ, t), bk=min(ATTN_BK, t), kv_len=t)
    else:
        cache_ckv, cache_kpe = cache
        past = cache_ckv.shape[1]
        kv_len = past + t
        tk = -(-kv_len // LANES) * LANES
        cat = lambda c, n, d: jnp.concatenate(
            [c.astype(F32), n.reshape(nstreams, t, d), jnp.zeros((nstreams, tk - kv_len, d), F32)],
            axis=1).reshape(nstreams * tk, d)
        kk, vv = _kvexp(cat(cache_ckv, ckv, KV_LORA), cat(cache_kpe, kpe, ROPE_DIM), w, rows=512)
        yb = _attn(q.reshape(nstreams, t, -1), kk.reshape(nstreams, tk, -1), vv.reshape(nstreams, tk, -1),
                   causal=False, bq=t, bk=tk, kv_len=kv_len)

    y = _tail(x2, ya, yb.reshape(total, D_B), w, rows=min(256, total))
    return (y.reshape(nstreams, t, D_MODEL), ckv.reshape(nstreams, t, KV_LORA),
            kpe.reshape(nstreams, t, ROPE_DIM), _pairs_to_state(hout), shift)


def kernel(x_prompt, x_sample, cache_ckv, cache_kpe, state_wkv, state_shift, w_in, mu_shift, w0, w_w2, a0,
           w_a2, w_g2, k_k, k_a, r_k, lnx_g, lnx_b, w_pa, q_norm_g, w_uq, kv_norm_g, w_ukv, w_pb, b_gate,
           w_o, ln1_g, ln1_b, w_gu, w_down, ln2_g, ln2_b):
    w = _prep_weights(w_in, mu_shift, w0, w_w2, a0, w_a2, w_g2, k_k, k_a, r_k, lnx_g, lnx_b, w_pa,
                      q_norm_g, w_uq, kv_norm_g, w_ukv, w_pb, b_gate, w_o, ln1_g, ln1_b, w_gu, w_down,
                      ln2_g, ln2_b)
    bp = x_prompt.shape[0]
    y_p, ckv_p, kpe_p, wkv_p, shift_p = _layer(
        x_prompt, 0, jnp.zeros((bp, 1, A_COLS), F32),
        jnp.zeros((bp, A_HEADS, A_HEAD_DIM, A_HEAD_DIM), F32), None, w)
    y_s, ckv_s, kpe_s, wkv_s, shift_s = _layer(
        x_sample, cache_ckv.shape[1], state_shift, state_wkv, (cache_ckv, cache_kpe), w)
    return (y_p, y_s, ckv_p, kpe_p, wkv_p, shift_p, ckv_s, kpe_s, wkv_s, shift_s)
```

```python
import functools

import numpy as np
import jax
import jax.numpy as jnp
from jax import lax
from jax.experimental import pallas as pl
from jax.experimental.pallas import tpu as pltpu

D_MODEL = 1024
CHUNK = 64
A_HEADS = 8
A_HEAD_DIM = 64
D_A = 512
DECAY_LORA = 64
AAA_LORA = 64
GATE_LORA = 128
A_COLS = 3 * D_A + DECAY_LORA + AAA_LORA + GATE_LORA
LNX_EPS = A_HEAD_DIM * 1e-5
B_HEADS = 8
Q_LORA = 256
KV_LORA = 128
NOPE_DIM = 64
ROPE_DIM = 32
V_DIM = 64
D_B = 512
B_COLS = Q_LORA + KV_LORA + ROPE_DIM
ROPE_BASE = 10000.0
ATTN_SCALE = (NOPE_DIM + ROPE_DIM) ** -0.5
RMS_EPS = 1e-6
D_FF = 2816
LN_EPS = 1e-5
DN_ALPHA = 2.0 ** 0.25

LANES = 128
QK_PAD = 128
PROJ_COLS = A_COLS + Q_LORA + KV_LORA + LANES
FF_CHUNK = 256
N_FF = D_FF // FF_CHUNK
TRI_ROWS = 256
PROJ_ROWS = 512
TAIL_ROWS = 512
ATTN_BQ = 1024
ATTN_BK = 512
WKV_GROUP = 4
VMEM_LIMIT = 56 * 1024 * 1024

F32 = jnp.float32
BF16 = jnp.bfloat16


def _dot(a, b):
    return jnp.dot(a, b, preferred_element_type=F32)


def _dot_nt(a, b):
    return lax.dot_general(a, b, (((1,), (1,)), ((), ())), preferred_element_type=F32)


def _bf(x):
    return x.astype(BF16)


def _split2(x):
    hi = x.astype(BF16)
    lo = (x - hi.astype(F32)).astype(BF16)
    return hi, lo


def _split3(x):
    h1 = x.astype(BF16)
    r1 = x - h1.astype(F32)
    h2 = r1.astype(BF16)
    h3 = (r1 - h2.astype(F32)).astype(BF16)
    return h1, h2, h3


def _sigmoid(z):
    return 1.0 / (1.0 + jnp.exp(-z))


def _full(shape):
    n = len(shape)
    return pl.BlockSpec(shape, lambda *_: (0,) * n)


def _proj_kernel(x_ref, shift0_ref, wall_ref, mu_ref, w0_ref, a0_ref, kk_ref, ka_ref, rk_ref,
                 ww2_ref, wa2_ref, wg2_ref, esum_ref, ltri_ref, qg_ref, kvg_ref, wqa_ref, wqb_ref,
                 cq_ref, sq_ref, ck_ref,
                 rt_ref, kt_ref, bt_ref, at_ref, v_ref, g_ref, bonus_ref, wc_ref, shift_ref,
                 q_ref, ckv_ref, kpe_ref, carry_ref, *, rows, chunk):
    b = pl.program_id(1)
    xb = _bf(x_ref[...])

    pa = _dot(xb, wall_ref[:, :A_COLS])
    first = jnp.where(b == 0, shift0_ref[0], carry_ref[...])
    row = lax.broadcasted_iota(jnp.int32, (rows, 1), 0)
    prev = jnp.where(row == 0, first, pltpu.roll(pa, 1, axis=0))
    last = pa[rows - 1:rows, :]
    carry_ref[...] = last
    shift_ref[0] = last
    xs = pa + (prev - pa) * mu_ref[...]

    r = xs[:, :D_A]
    k = xs[:, D_A:2 * D_A]
    v = xs[:, 2 * D_A:3 * D_A]
    wa = xs[:, 3 * D_A:3 * D_A + LANES]
    gd = xs[:, 3 * D_A + LANES:]

    z = w0_ref[...] + _dot(_bf(jnp.tanh(wa)), ww2_ref[...])
    ld = -np.float32(np.exp(-0.5)) * _sigmoid(z)
    a = _sigmoid(a0_ref[...] + _dot(_bf(wa), wa2_ref[...]))
    g_ref[...] = _dot(_bf(_sigmoid(gd)), wg2_ref[...])

    esum = esum_ref[...]

    def headsum(t):
        hi, lo = _split2(t)
        return _dot(hi, esum) + _dot(lo, esum)

    kkr = k * kk_ref[...]
    kk = kkr / jnp.maximum(jnp.sqrt(headsum(kkr * kkr)), 1e-12)
    kh = k * (1.0 + (a - 1.0) * ka_ref[...])
    bonus_ref[...] = _dot(_bf(r * kh * rk_ref[...]), esum) * v
    v_ref[...] = v

    ltri = ltri_ref[...]
    tri = ltri.shape[0]
    h1, h2, h3 = _split3(ld)
    cum = jnp.concatenate(
        [_dot(ltri, h1[i:i + tri]) + _dot(ltri, h2[i:i + tri]) + _dot(ltri, h3[i:i + tri])
         for i in range(0, rows, tri)], axis=0)
    ep = jnp.exp(cum)
    em = jnp.exp(-cum)
    rt_ref[...] = r * ep
    kt_ref[...] = kh * em
    bt_ref[...] = (kk * a) * em
    at_ref[...] = -kk * jnp.exp(cum - ld)
    for c in range(rows // chunk):
        wc_ref[c * 8:(c + 1) * 8, :] = jnp.broadcast_to(ep[(c + 1) * chunk - 1:(c + 1) * chunk, :], (8, D_A))

    pq = _dot(xb, wall_ref[:, A_COLS:A_COLS + Q_LORA])
    cqn = _bf(pq * lax.rsqrt(jnp.mean(pq * pq, axis=-1, keepdims=True) + RMS_EPS) * qg_ref[...])
    qa = _dot(cqn, wqa_ref[...])
    qb = _dot(cqn, wqb_ref[...])
    cq = cq_ref[...]
    sq = sq_ref[...]
    for h in range(B_HEADS):
        sl = slice(h * QK_PAD, (h + 1) * QK_PAD)
        q_ref[:, sl] = _bf(qa[:, sl] * cq + qb[:, sl] * sq)

    pkv = _dot(xb, wall_ref[:, A_COLS + Q_LORA:A_COLS + Q_LORA + KV_LORA])
    ckv_ref[...] = pkv * lax.rsqrt(jnp.mean(pkv * pkv, axis=-1, keepdims=True) + RMS_EPS) * kvg_ref[...]

    ppe = _dot(xb, wall_ref[:, A_COLS + Q_LORA + KV_LORA:]) * ck_ref[...]
    kpe_ref[...] = ppe[:, :ROPE_DIM] + ppe[:, ROPE_DIM:2 * ROPE_DIM]


def _proj(x2, shift0, w, tabs, *, nstreams, bps, rows, chunk):
    total = nstreams * bps * rows
    nck = rows // chunk
    tri = min(rows, TRI_ROWS)
    ltri = _chunk_tri(tri, chunk)
    rowblk = lambda n: pl.BlockSpec((rows, n), lambda s, b: (s * bps + b, 0))
    in_specs = [
        rowblk(D_MODEL),
        pl.BlockSpec((1, 1, A_COLS), lambda s, b: (s, 0, 0)),
        _full((D_MODEL, PROJ_COLS)),
        _full((1, A_COLS)), _full((1, D_A)), _full((1, D_A)), _full((1, D_A)), _full((1, D_A)), _full((1, D_A)),
        _full((LANES, D_A)), _full((LANES, D_A)), _full((GATE_LORA, D_A)),
        _full((D_A, D_A)), _full((tri, tri)),
        _full((1, Q_LORA)), _full((1, KV_LORA)),
        _full((Q_LORA, B_HEADS * QK_PAD)), _full((Q_LORA, B_HEADS * QK_PAD)),
        rowblk(LANES), rowblk(LANES), rowblk(LANES),
    ]
    f32o = lambda n: jax.ShapeDtypeStruct((total, n), F32)
    out_shape = [f32o(D_A)] * 7 + [
        jax.ShapeDtypeStruct((total // chunk * 8, D_A), F32),
        jax.ShapeDtypeStruct((nstreams, 1, A_COLS), F32),
        jax.ShapeDtypeStruct((total, B_HEADS * QK_PAD), BF16),
        f32o(KV_LORA), f32o(ROPE_DIM),
    ]
    out_specs = [rowblk(D_A)] * 7 + [
        pl.BlockSpec((nck * 8, D_A), lambda s, b: (s * bps + b, 0)),
        pl.BlockSpec((1, 1, A_COLS), lambda s, b: (s, 0, 0)),
        rowblk(B_HEADS * QK_PAD), rowblk(KV_LORA), rowblk(ROPE_DIM),
    ]
    return pl.pallas_call(
        functools.partial(_proj_kernel, rows=rows, chunk=chunk),
        out_shape=out_shape,
        grid=(nstreams, bps),
        in_specs=in_specs,
        out_specs=out_specs,
        scratch_shapes=[pltpu.VMEM((1, A_COLS), F32)],
        compiler_params=pltpu.CompilerParams(
            dimension_semantics=("arbitrary", "arbitrary"), vmem_limit_bytes=VMEM_LIMIT),
        name="proj",
    )(x2, shift0, w["wall"], w["mu"], w["w0"], w["a0"], w["k_k"], w["k_a"], w["r_k"],
      w["ww2"], w["wa2"], w["wg2"], w["esum"], ltri, w["qg"], w["kvg"], w["wqa"], w["wqb"],
      tabs["cq"], tabs["sq"], tabs["ck"])


def _chunk_tri(rows, chunk):
    i = np.arange(rows)
    m = (i[:, None] // chunk == i[None, :] // chunk) & (i[None, :] <= i[:, None])
    return jnp.asarray(m, BF16)


def _kvexp_kernel(ckv_ref, kpe_ref, wk_ref, place_ref, wv_ref, k_ref, v_ref):
    c = _bf(ckv_ref[...])
    k_ref[...] = _bf(_dot(c, wk_ref[...]) + _dot(_bf(kpe_ref[...]), place_ref[...]))
    v_ref[...] = _bf(_dot(c, wv_ref[...]))


def _kvexp(ckv, kpe, w, *, rows):
    total = ckv.shape[0]
    rowblk = lambda n: pl.BlockSpec((rows, n), lambda i: (i, 0))
    return pl.pallas_call(
        _kvexp_kernel,
        out_shape=[jax.ShapeDtypeStruct((total, B_HEADS * QK_PAD), BF16),
                   jax.ShapeDtypeStruct((total, D_B), BF16)],
        grid=(total // rows,),
        in_specs=[rowblk(KV_LORA), rowblk(ROPE_DIM), _full((KV_LORA, B_HEADS * QK_PAD)),
                  _full((ROPE_DIM, B_HEADS * QK_PAD)), _full((KV_LORA, D_B))],
        out_specs=[rowblk(B_HEADS * QK_PAD), rowblk(D_B)],
        compiler_params=pltpu.CompilerParams(dimension_semantics=("arbitrary",)),
        name="kvexp",
    )(ckv, kpe, w["wk"], w["place"], w["wv"])


def _attn_kernel(qi_ref, ki_ref, q_ref, k_ref, v_ref, o_ref, m_ref, acc_ref, *,
                 causal, bq, bk, nk, kv_len):
    s_id = pl.program_id(1)
    qi = qi_ref[s_id]
    ki = ki_ref[s_id]
    ratio = bq // bk

    @pl.when(ki == 0)
    def _():
        m_ref[...] = jnp.full(m_ref.shape, -jnp.inf, F32)
        acc_ref[...] = jnp.zeros(acc_ref.shape, F32)

    low = lax.broadcasted_iota(jnp.int32, (1, LANES), 1) < V_DIM

    def scores(h):
        sl = slice(h * QK_PAD, (h + 1) * QK_PAD)
        return _dot_nt(q_ref[0, :, sl], k_ref[0, :, sl])

    def step(mask):
        one = jnp.ones((), BF16)
        ahead = 2
        pending = [scores(h) for h in range(ahead)]
        for h in range(B_HEADS):
            s = pending.pop(0)
            if h + ahead < B_HEADS:
                pending.append(scores(h + ahead))
            vp = v_ref[0, :, (h // 2) * LANES:(h // 2 + 1) * LANES]
            vext = jnp.where(low, vp, one) if h % 2 == 0 else jnp.where(low, one, vp)
            if mask is not None:
                s = jnp.where(mask, s, -jnp.inf)
            m_prev = m_ref[h]
            m_new = jnp.maximum(m_prev, jnp.max(s, axis=-1, keepdims=True))
            p = jnp.exp2(s - m_new[:, :1])
            acc_ref[h] = jnp.exp2(m_prev - m_new) * acc_ref[h] + _dot(_bf(p), vext)
            m_ref[h] = m_new

    if causal:
        @pl.when(ki < ratio * qi)
        def _():
            step(None)

        @pl.when(ki >= ratio * qi)
        def _():
            rq = (lax.broadcasted_iota(jnp.int32, (bq, bk), 0) + qi * bq) // CHUNK
            ck = (lax.broadcasted_iota(jnp.int32, (bq, bk), 1) + ki * bk) // CHUNK
            step(ck <= rq)
        is_last = ki == ratio * (qi + 1) - 1
    else:
        if kv_len < nk * bk:
            col = lax.broadcasted_iota(jnp.int32, (bq, bk), 1) + ki * bk
            step(col < kv_len)
        else:
            step(None)
        is_last = ki == nk - 1

    @pl.when(is_last)
    def _():
        for j in range(B_HEADS // 2):
            a0 = acc_ref[2 * j]
            a1 = acc_ref[2 * j + 1]
            num = jnp.where(low, a0, a1)
            den = jnp.where(low, pltpu.roll(a0, V_DIM, axis=1), pltpu.roll(a1, V_DIM, axis=1))
            o_ref[0, :, j * LANES:(j + 1) * LANES] = num / den


def _attn(q, k, v, *, causal, bq, bk, kv_len):
    nb, tq, _ = q.shape
    tk = k.shape[1]
    nq, nk = tq // bq, tk // bk
    ratio = bq // bk
    steps = [(i, j) for i in range(nq) for j in range(ratio * (i + 1) if causal else nk)]
    qi = jnp.asarray([s[0] for s in steps], jnp.int32)
    ki = jnp.asarray([s[1] for s in steps], jnp.int32)
    grid_spec = pltpu.PrefetchScalarGridSpec(
        num_scalar_prefetch=2,
        grid=(nb, len(steps)),
        in_specs=[pl.BlockSpec((1, bq, B_HEADS * QK_PAD), lambda b, s, qi, ki: (b, qi[s], 0)),
                  pl.BlockSpec((1, bk, B_HEADS * QK_PAD), lambda b, s, qi, ki: (b, ki[s], 0)),
                  pl.BlockSpec((1, bk, D_B), lambda b, s, qi, ki: (b, ki[s], 0))],
        out_specs=pl.BlockSpec((1, bq, D_B), lambda b, s, qi, ki: (b, qi[s], 0)),
        scratch_shapes=[pltpu.VMEM((B_HEADS, bq, LANES), F32),
                        pltpu.VMEM((B_HEADS, bq, LANES), F32)])
    return pl.pallas_call(
        functools.partial(_attn_kernel, causal=causal, bq=bq, bk=bk, nk=nk, kv_len=kv_len),
        out_shape=jax.ShapeDtypeStruct((nb, tq, D_B), F32),
        grid_spec=grid_spec,
        compiler_params=pltpu.CompilerParams(
            dimension_semantics=("arbitrary", "arbitrary"), vmem_limit_bytes=VMEM_LIMIT),
        name="attn",
    )(qi, ki, q, k, v)


def _wkv_kernel(rt_ref, kt_ref, bt_ref, at_ref, v_ref, g_ref, bonus_ref, wc_ref, lg_ref, lb_ref,
                eavg_ref, h0_ref, y_ref, hout_ref, h_ref, *, chunk, group, nsteps):
    c = pl.program_id(1)
    C2 = 2 * chunk
    npair = A_HEADS // 2

    @pl.when(c == 0)
    def _():
        h_ref[...] = h0_ref[0]

    low = lax.broadcasted_iota(jnp.int32, (chunk, LANES), 1) < A_HEAD_DIM
    ii = lax.broadcasted_iota(jnp.int32, (C2, C2), 0)
    jj = lax.broadcasted_iota(jnp.int32, (C2, C2), 1)
    strict = ii > jj
    incl = ii >= jj
    eye_c = (ii == jj).astype(F32)
    ki = lax.broadcasted_iota(jnp.int32, (LANES, LANES), 0)
    kj = lax.broadcasted_iota(jnp.int32, (LANES, LANES), 1)
    eye_k = (ki == kj).astype(F32)
    eavg = eavg_ref[...]

    def stack(t):
        return jnp.concatenate([jnp.where(low, t, 0.0), jnp.where(low, 0.0, t)], axis=0)

    units = [(ci, j) for ci in range(group) for j in range(npair)]
    rows = lambda ci: slice(ci * chunk, (ci + 1) * chunk)
    lanes = lambda j: slice(j * LANES, (j + 1) * LANES)
    ld = lambda ref: [stack(ref[rows(ci), lanes(j)]) for ci, j in units]
    At, Bt, Kt, Rt, Vs = ld(at_ref), ld(bt_ref), ld(kt_ref), ld(rt_ref), ld(v_ref)
    nu = range(len(units))
    Vb = [_bf(Vs[u]) for u in nu]
    g1 = [_dot_nt(_bf(jnp.concatenate([At[u], Rt[u]], axis=0)),
                  _bf(jnp.concatenate([Bt[u], Kt[u]], axis=0))) for u in nu]
    Aab = [jnp.where(strict, g1[u][:C2, :C2], 0.0) for u in nu]
    Aak = [_bf(jnp.where(strict, g1[u][:C2, C2:], 0.0)) for u in nu]
    Arb = [_bf(jnp.where(incl, g1[u][C2:, :C2], 0.0)) for u in nu]
    Ark = [_bf(jnp.where(incl, g1[u][C2:, C2:], 0.0)) for u in nu]
    Tm = [eye_c + Aab[u] for u in nu]
    Pw = Aab
    n = 1
    while 2 * n < chunk:
        Pb = [_bf(Pw[u]) for u in nu]
        Pw = [_dot(Pb[u], Pb[u]) for u in nu]
        Tm = [Tm[u] + _dot(_bf(Tm[u]), _bf(Pw[u])) for u in nu]
        n *= 2
    akv = [_dot(Aak[u], Vb[u]) for u in nu]
    PPb = [_bf(_dot(_bf(Tm[u]), _bf(jnp.concatenate([At[u], akv[u]], axis=1)))) for u in nu]
    QQ = [_dot(Arb[u], PPb[u]) for u in nu]
    arkv = [_dot(Ark[u], Vb[u]) for u in nu]
    wrow = [wc_ref[ci * 8:ci * 8 + 1, lanes(j)] for ci, j in units]
    MM = [_dot(_bf((Bt[u] * wrow[u]).T), PPb[u]) for u in nu]
    ktv = [_dot(_bf((Kt[u] * wrow[u]).T), Vb[u]) for u in nu]
    Q1, Q2, M1, M2 = [], [], [], []
    for u in nu:
        q1s = Rt[u] + QQ[u][:, :LANES]
        q2s = QQ[u][:, LANES:] + arkv[u]
        Q1.append(_bf(q1s[:chunk] + q1s[chunk:]))
        Q2.append(q2s[:chunk] + q2s[chunk:])
        M1.append(_bf(eye_k * wrow[u] + MM[u][:, :LANES]))
        M2.append(MM[u][:, LANES:] + ktv[u])

    H = [h_ref[j] for j in range(npair)]
    Y = []
    for u, (ci, j) in enumerate(units):
        Hb = _bf(H[j])
        Y.append(_dot(Q1[u], Hb) + Q2[u])
        H[j] = _dot(M1[u], Hb) + M2[u]
    for j in range(npair):
        h_ref[j] = H[j]

    ysp = [_split2(Y[u]) for u in nu]
    mu = [_dot(ysp[u][0], eavg) + _dot(ysp[u][1], eavg) for u in nu]
    dv = [Y[u] - mu[u] for u in nu]
    dsp = [_split2(dv[u] * dv[u]) for u in nu]
    var = [_dot(dsp[u][0], eavg) + _dot(dsp[u][1], eavg) for u in nu]
    for u, (ci, j) in enumerate(units):
        yn = dv[u] * lax.rsqrt(var[u] + LNX_EPS) * lg_ref[:, lanes(j)] + lb_ref[:, lanes(j)]
        y_ref[rows(ci), lanes(j)] = (yn + bonus_ref[rows(ci), lanes(j)]) * g_ref[rows(ci), lanes(j)]

    @pl.when(c == nsteps - 1)
    def _():
        hout_ref[0] = h_ref[...]


def _wkv(rt, kt, bt, at, v, g, bonus, wc, h0, w, *, nstreams, ncs, chunk, group):
    total = rt.shape[0]
    nsteps = ncs // group
    blk = pl.BlockSpec((group * chunk, D_A), lambda s, c: (s * nsteps + c, 0))
    hspec = pl.BlockSpec((1, A_HEADS // 2, LANES, LANES), lambda s, c: (s, 0, 0, 0))
    return pl.pallas_call(
        functools.partial(_wkv_kernel, chunk=chunk, group=group, nsteps=nsteps),
        out_shape=[jax.ShapeDtypeStruct((total, D_A), F32),
                   jax.ShapeDtypeStruct((nstreams, A_HEADS // 2, LANES, LANES), F32)],
        grid=(nstreams, nsteps),
        in_specs=[blk] * 7 + [pl.BlockSpec((group * 8, D_A), lambda s, c: (s * nsteps + c, 0)),
                              _full((1, D_A)), _full((1, D_A)), _full((LANES, LANES)), hspec],
        out_specs=[blk, hspec],
        scratch_shapes=[pltpu.VMEM((A_HEADS // 2, LANES, LANES), F32)],
        compiler_params=pltpu.CompilerParams(
            dimension_semantics=("arbitrary", "arbitrary"), vmem_limit_bytes=VMEM_LIMIT),
        name="wkv",
    )(rt, kt, bt, at, v, g, bonus, wc, w["lnx_g"], w["lnx_b"], w["eavg"], h0)


def _layer_norm(t, g, b):
    mu = jnp.mean(t, axis=-1, keepdims=True)
    d = t - mu
    var = jnp.mean(d * d, axis=-1, keepdims=True)
    return d * lax.rsqrt(var + LN_EPS) * g + b


def _tail_kernel(x_ref, ya_ref, yb_ref, wg_ref, bg_ref, wpa_ref, wpb_ref, wo_ref, l1g_ref, l1b_ref,
                 wgu_ref, wdown_ref, l2g_ref, l2b_ref, o_ref):
    x = x_ref[...]
    gates = _sigmoid(_dot(_bf(x), wg_ref[...]) + bg_ref[...])
    m = (gates[:, :D_MODEL] * _dot(_bf(ya_ref[...]), wpa_ref[...])
         + gates[:, D_MODEL:] * _dot(_bf(yb_ref[...]), wpb_ref[...]))
    h = _layer_norm(DN_ALPHA * x + _dot(_bf(m), wo_ref[...]), l1g_ref[...], l1b_ref[...])
    hb = _bf(h)

    def gate_up(c):
        cols = slice(c * FF_CHUNK, (c + 1) * FF_CHUNK)
        ucols = slice(D_FF + c * FF_CHUNK, D_FF + (c + 1) * FF_CHUNK)
        return _dot(hb, wgu_ref[:, cols]), _dot(hb, wgu_ref[:, ucols])

    f = None
    nxt = gate_up(0)
    for c in range(N_FF):
        gate, up = nxt
        if c + 1 < N_FF:
            nxt = gate_up(c + 1)
        d = _dot(_bf(gate * _sigmoid(gate) * up), wdown_ref[c * FF_CHUNK:(c + 1) * FF_CHUNK, :])
        f = d if f is None else f + d
    o_ref[...] = _layer_norm(DN_ALPHA * h + f, l2g_ref[...], l2b_ref[...])


def _tail(x2, ya, yb, w, *, rows):
    total = x2.shape[0]
    rowblk = lambda n: pl.BlockSpec((rows, n), lambda i: (i, 0))

    def const(shape):
        n = len(shape)
        return pl.BlockSpec(shape, lambda i: (0,) * n, pipeline_mode=pl.Buffered(1))

    return pl.pallas_call(
        _tail_kernel,
        out_shape=jax.ShapeDtypeStruct((total, D_MODEL), F32),
        grid=(total // rows,),
        in_specs=[rowblk(D_MODEL), rowblk(D_A), rowblk(D_B),
                  const((D_MODEL, 2 * D_MODEL)), const((1, 2 * D_MODEL)),
                  const((D_A, D_MODEL)), const((D_B, D_MODEL)), const((D_MODEL, D_MODEL)),
                  const((1, D_MODEL)), const((1, D_MODEL)),
                  const((D_MODEL, 2 * D_FF)), const((D_FF, D_MODEL)),
                  const((1, D_MODEL)), const((1, D_MODEL))],
        out_specs=rowblk(D_MODEL),
        compiler_params=pltpu.CompilerParams(
            dimension_semantics=("arbitrary",), vmem_limit_bytes=VMEM_LIMIT),
        name="tail",
    )(x2, ya, yb, w["wg"], w["bg"], w["wpa"], w["wpb"], w["wo"], w["l1g"], w["l1b"],
      w["wgu"], w["wdown"], w["l2g"], w["l2b"])


def _prep_weights(w_in, mu_shift, w0, w_w2, a0, w_a2, w_g2, k_k, k_a, r_k, lnx_g, lnx_b, w_pa,
                  q_norm_g, w_uq, kv_norm_g, w_ukv, w_pb, b_gate, w_o, ln1_g, ln1_b, w_gu, w_down,
                  ln2_g, ln2_b):
    row = lambda t: t.reshape(1, -1).astype(F32)
    nb = A_COLS + B_COLS
    pe = w_in[:, nb - ROPE_DIM:nb]
    half = ROPE_DIM // 2
    pe_sw = jnp.concatenate([pe[:, half:], pe[:, :half]], axis=1)
    wall = jnp.concatenate([w_in[:, :nb - ROPE_DIM], pe, pe_sw,
                            jnp.zeros((D_MODEL, LANES - 2 * ROPE_DIM), F32)], axis=1)
    uq = w_uq.reshape(Q_LORA, B_HEADS, NOPE_DIM + ROPE_DIM)
    nope, r1, r2 = uq[..., :NOPE_DIM], uq[..., NOPE_DIM:NOPE_DIM + half], uq[..., NOPE_DIM + half:]
    zpad = jnp.zeros((Q_LORA, B_HEADS, QK_PAD - NOPE_DIM - ROPE_DIM), F32)
    wqa = jnp.concatenate([nope, r1, r2, zpad], axis=-1).reshape(Q_LORA, B_HEADS * QK_PAD)
    wqb = jnp.concatenate([jnp.zeros_like(nope), r2, r1, zpad], axis=-1).reshape(Q_LORA, B_HEADS * QK_PAD)
    ukv = w_ukv.reshape(KV_LORA, B_HEADS, NOPE_DIM + V_DIM)
    wk = jnp.concatenate([ukv[..., :NOPE_DIM], jnp.zeros((KV_LORA, B_HEADS, QK_PAD - NOPE_DIM), F32)],
                         axis=-1).reshape(KV_LORA, B_HEADS * QK_PAD)
    wv = ukv[..., NOPE_DIM:].reshape(KV_LORA, D_B)
    place = np.zeros((ROPE_DIM, B_HEADS * QK_PAD), np.float32)
    for h in range(B_HEADS):
        place[np.arange(ROPE_DIM), h * QK_PAD + NOPE_DIM + np.arange(ROPE_DIM)] = 1.0
    hid = np.arange(D_A) // A_HEAD_DIM
    esum = (hid[:, None] == hid[None, :]).astype(np.float32)
    lid = np.arange(LANES) // A_HEAD_DIM
    eavg = (lid[:, None] == lid[None, :]).astype(np.float32) / A_HEAD_DIM
    zl = jnp.zeros((LANES - DECAY_LORA, D_A), F32)
    return {
        "wall": _bf(wall), "mu": row(mu_shift), "w0": row(w0), "a0": row(a0), "k_k": row(k_k),
        "k_a": row(k_a), "r_k": row(r_k),
        "ww2": _bf(jnp.concatenate([w_w2, zl], axis=0)), "wa2": _bf(jnp.concatenate([zl, w_a2], axis=0)),
        "wg2": _bf(w_g2), "esum": jnp.asarray(esum, BF16), "eavg": jnp.asarray(eavg, BF16),
        "qg": row(q_norm_g), "kvg": row(kv_norm_g), "wqa": _bf(wqa), "wqb": _bf(wqb),
        "wk": _bf(wk), "wv": _bf(wv), "place": jnp.asarray(place, BF16),
        "lnx_g": row(lnx_g), "lnx_b": row(lnx_b),
        "wg": _bf(w_in[:, nb:]), "bg": row(b_gate), "wpa": _bf(w_pa), "wpb": _bf(w_pb), "wo": _bf(w_o),
        "l1g": row(ln1_g), "l1b": row(ln1_b), "l2g": row(ln2_g), "l2b": row(ln2_b),
        "wgu": _bf(w_gu), "wdown": _bf(w_down),
    }


def _rope_tables(pos, reps):
    half = ROPE_DIM // 2
    inv = ROPE_BASE ** (-jnp.arange(half, dtype=F32) / half)
    ang = pos.astype(F32)[:, None] * inv
    cos, sin = lax.optimization_barrier((jnp.cos(ang), jnp.sin(ang)))
    n = pos.shape[0]
    sc = np.float32(ATTN_SCALE * np.log2(np.e))
    cq = jnp.concatenate([jnp.full((n, NOPE_DIM), sc, F32), cos * sc, cos * sc,
                          jnp.zeros((n, QK_PAD - NOPE_DIM - ROPE_DIM), F32)], axis=1)
    sq = jnp.concatenate([jnp.zeros((n, NOPE_DIM), F32), -sin * sc, sin * sc,
                          jnp.zeros((n, QK_PAD - NOPE_DIM - ROPE_DIM), F32)], axis=1)
    ck = jnp.concatenate([cos, cos, -sin, sin, jnp.zeros((n, LANES - 2 * ROPE_DIM), F32)], axis=1)
    tile = lambda t: jnp.tile(t, (reps, 1))
    return {"cq": tile(cq), "sq": tile(sq), "ck": tile(ck)}


def _state_to_pairs(s):
    nb = s.shape[0]
    st = jnp.swapaxes(s, -1, -2).reshape(nb, A_HEADS // 2, 2, A_HEAD_DIM, A_HEAD_DIM)
    eye = jnp.eye(2, dtype=s.dtype)
    return jnp.einsum("sjakv,ab->sjakbv", st, eye).reshape(nb, A_HEADS // 2, LANES, LANES)


def _pairs_to_state(hp):
    nb = hp.shape[0]
    h6 = hp.reshape(nb, A_HEADS // 2, 2, A_HEAD_DIM, 2, A_HEAD_DIM)
    diag = jnp.stack([h6[:, :, 0, :, 0, :], h6[:, :, 1, :, 1, :]], axis=2)
    return jnp.swapaxes(diag.reshape(nb, A_HEADS, A_HEAD_DIM, A_HEAD_DIM), -1, -2)


def _layer(x, pos0, shift0, wkv0, cache, w):
    nstreams, t, _ = x.shape
    total = nstreams * t
    x2 = x.reshape(total, D_MODEL)
    chunk = min(CHUNK, t)
    rows = min(PROJ_ROWS, t)
    bps = t // rows
    tabs = _rope_tables(pos0 + jnp.arange(t), nstreams)
    (rt, kt, bt, at, v, g, bonus, wc, shift, q, ckv, kpe) = _proj(
        x2, shift0, w, tabs, nstreams=nstreams, bps=bps, rows=rows, chunk=chunk)

    ya, hout = _wkv(rt, kt, bt, at, v, g, bonus, wc, _state_to_pairs(wkv0), w,
                    nstreams=nstreams, ncs=t // chunk, chunk=chunk, group=min(WKV_GROUP, t // chunk))

    if cache is None:
        kk, vv = _kvexp(ckv, kpe, w, rows=min(512, total))
        yb = _attn(q.reshape(nstreams, t, -1), kk.reshape(nstreams, t, -1), vv.reshape(nstreams, t, -1),
                   causal=True, bq=min(ATTN_BQ, t), bk=min(ATTN_BK, t), kv_len=t)
    else:
        cache_ckv, cache_kpe = cache
        past = cache_ckv.shape[1]
        kv_len = past + t
        tk = -(-kv_len // LANES) * LANES
        cat = lambda c, n, d: jnp.concatenate(
            [c.astype(F32), n.reshape(nstreams, t, d), jnp.zeros((nstreams, tk - kv_len, d), F32)],
            axis=1).reshape(nstreams * tk, d)
        kk, vv = _kvexp(cat(cache_ckv, ckv, KV_LORA), cat(cache_kpe, kpe, ROPE_DIM), w, rows=512)
        yb = _attn(q.reshape(nstreams, t, -1), kk.reshape(nstreams, tk, -1), vv.reshape(nstreams, tk, -1),
                   causal=False, bq=t, bk=tk, kv_len=kv_len)

    y = _tail(x2, ya, yb.reshape(total, D_B), w, rows=min(TAIL_ROWS, total))
    return (y.reshape(nstreams, t, D_MODEL), ckv.reshape(nstreams, t, KV_LORA),
            kpe.reshape(nstreams, t, ROPE_DIM), _pairs_to_state(hout), shift)


def kernel(x_prompt, x_sample, cache_ckv, cache_kpe, state_wkv, state_shift, w_in, mu_shift, w0, w_w2, a0,
           w_a2, w_g2, k_k, k_a, r_k, lnx_g, lnx_b, w_pa, q_norm_g, w_uq, kv_norm_g, w_ukv, w_pb, b_gate,
           w_o, ln1_g, ln1_b, w_gu, w_down, ln2_g, ln2_b):
    w = _prep_weights(w_in, mu_shift, w0, w_w2, a0, w_a2, w_g2, k_k, k_a, r_k, lnx_g, lnx_b, w_pa,
                      q_norm_g, w_uq, kv_norm_g, w_ukv, w_pb, b_gate, w_o, ln1_g, ln1_b, w_gu, w_down,
                      ln2_g, ln2_b)
    bp = x_prompt.shape[0]
    y_p, ckv_p, kpe_p, wkv_p, shift_p = _layer(
        x_prompt, 0, jnp.zeros((bp, 1, A_COLS), F32),
        jnp.zeros((bp, A_HEADS, A_HEAD_DIM, A_HEAD_DIM), F32), None, w)
    y_s, ckv_s, kpe_s, wkv_s, shift_s = _layer(
        x_sample, cache_ckv.shape[1], state_shift, state_wkv, (cache_ckv, cache_kpe), w)
    return (y_p, y_s, ckv_p, kpe_p, wkv_p, shift_p, ckv_s, kpe_s, wkv_s, shift_s)
```

```python
import functools

import numpy as np
import jax
import jax.numpy as jnp
from jax import lax
from jax.experimental import pallas as pl
from jax.experimental.pallas import tpu as pltpu

D_MODEL = 1024
CHUNK = 64
A_HEADS = 8
A_HEAD_DIM = 64
D_A = 512
DECAY_LORA = 64
AAA_LORA = 64
GATE_LORA = 128
A_COLS = 3 * D_A + DECAY_LORA + AAA_LORA + GATE_LORA
LNX_EPS = A_HEAD_DIM * 1e-5
B_HEADS = 8
Q_LORA = 256
KV_LORA = 128
NOPE_DIM = 64
ROPE_DIM = 32
V_DIM = 64
D_B = 512
B_COLS = Q_LORA + KV_LORA + ROPE_DIM
ROPE_BASE = 10000.0
ATTN_SCALE = (NOPE_DIM + ROPE_DIM) ** -0.5
SCORE_SCALE = ATTN_SCALE * float(np.log2(np.e))
RMS_EPS = 1e-6
D_FF = 2816
LN_EPS = 1e-5
DN_ALPHA = 2.0 ** 0.25

LANES = 128
QK_PAD = 128
PROJ_COLS = A_COLS + Q_LORA + KV_LORA + LANES
FF_CHUNK = 256
N_FF = D_FF // FF_CHUNK
TRI_ROWS = 256
PROJ_ROWS = 512
TAIL_ROWS = 512
ATTN_BQ = 1024
ATTN_BK = 1024
WKV_GROUP = 4
VMEM_LIMIT = 56 * 1024 * 1024

F32 = jnp.float32
BF16 = jnp.bfloat16


def _dot(a, b):
    return jnp.dot(a, b, preferred_element_type=F32)


def _dot_nt(a, b):
    return lax.dot_general(a, b, (((1,), (1,)), ((), ())), preferred_element_type=F32)


def _bf(x):
    return x.astype(BF16)


def _split2(x):
    hi = x.astype(BF16)
    lo = (x - hi.astype(F32)).astype(BF16)
    return hi, lo


def _split3(x):
    h1 = x.astype(BF16)
    r1 = x - h1.astype(F32)
    h2 = r1.astype(BF16)
    h3 = (r1 - h2.astype(F32)).astype(BF16)
    return h1, h2, h3


def _sigmoid(z):
    return 1.0 / (1.0 + jnp.exp(-z))


def _full(shape):
    n = len(shape)
    return pl.BlockSpec(shape, lambda *_: (0,) * n)


def _proj_kernel(x_ref, shift0_ref, wall_ref, mu_ref, w0_ref, a0_ref, kk_ref, ka_ref, rk_ref,
                 ww2_ref, wa2_ref, wg2_ref, esum_ref, ltri_ref, qg_ref, kvg_ref, wqa_ref, wqb_ref,
                 rope_ref,
                 rt_ref, kt_ref, bt_ref, at_ref, v_ref, g_ref, bonus_ref, wc_ref, shift_ref,
                 q_ref, ckv_ref, kpe_ref, carry_ref, *, rows, chunk):
    b = pl.program_id(1)
    xb = _bf(x_ref[...])

    pa = _dot(xb, wall_ref[:, :A_COLS])
    first = jnp.where(b == 0, shift0_ref[0], carry_ref[...])
    row = lax.broadcasted_iota(jnp.int32, (rows, 1), 0)
    prev = jnp.where(row == 0, first, pltpu.roll(pa, 1, axis=0))
    last = pa[rows - 1:rows, :]
    carry_ref[...] = last
    shift_ref[0] = last
    xs = pa + (prev - pa) * mu_ref[...]

    r = xs[:, :D_A]
    k = xs[:, D_A:2 * D_A]
    v = xs[:, 2 * D_A:3 * D_A]
    wa = xs[:, 3 * D_A:3 * D_A + LANES]
    gd = xs[:, 3 * D_A + LANES:]

    z = w0_ref[...] + _dot(_bf(jnp.tanh(wa)), ww2_ref[...])
    ld = -np.float32(np.exp(-0.5)) * _sigmoid(z)
    a = _sigmoid(a0_ref[...] + _dot(_bf(wa), wa2_ref[...]))
    g_ref[...] = _dot(_bf(_sigmoid(gd)), wg2_ref[...])

    kkr = k * kk_ref[...]
    kh = k * (1.0 + (a - 1.0) * ka_ref[...])
    hi, lo = _split2(kkr * kkr)
    hs = _dot(jnp.concatenate([hi, lo, _bf(r * kh * rk_ref[...])], axis=0), esum_ref[...])
    kk = kkr / jnp.maximum(jnp.sqrt(hs[:rows] + hs[rows:2 * rows]), 1e-12)
    bonus_ref[...] = hs[2 * rows:] * v
    v_ref[...] = v

    ltri = ltri_ref[...]
    tri = ltri.shape[0]
    h1, h2, h3 = _split3(ld)
    cum = jnp.concatenate(
        [_dot(ltri, h1[i:i + tri]) + _dot(ltri, h2[i:i + tri]) + _dot(ltri, h3[i:i + tri])
         for i in range(0, rows, tri)], axis=0)
    ep = jnp.exp(cum)
    em = jnp.exp(-cum)
    rt_ref[...] = r * ep
    kt_ref[...] = kh * em
    bt_ref[...] = (kk * a) * em
    at_ref[...] = -kk * jnp.exp(cum - ld)
    for c in range(rows // chunk):
        wc_ref[c * 8:(c + 1) * 8, :] = jnp.broadcast_to(ep[(c + 1) * chunk - 1:(c + 1) * chunk, :], (8, D_A))

    pq = _dot(xb, wall_ref[:, A_COLS:A_COLS + Q_LORA])
    cqn = _bf(pq * lax.rsqrt(jnp.mean(pq * pq, axis=-1, keepdims=True) + RMS_EPS) * qg_ref[...])
    qa = _dot(cqn, wqa_ref[...])
    qb = _dot(cqn, wqb_ref[...])
    rope = rope_ref[...]
    lane = lax.broadcasted_iota(jnp.int32, (1, QK_PAD), 1)
    cq = jnp.where(lane < NOPE_DIM, np.float32(SCORE_SCALE), rope)
    sq = pltpu.roll(rope, QK_PAD - ROPE_DIM, axis=1)
    for h in range(B_HEADS):
        sl = slice(h * QK_PAD, (h + 1) * QK_PAD)
        q_ref[:, sl] = _bf(qa[:, sl] * cq + qb[:, sl] * sq)

    pkv = _dot(xb, wall_ref[:, A_COLS + Q_LORA:A_COLS + Q_LORA + KV_LORA])
    ckv_ref[...] = pkv * lax.rsqrt(jnp.mean(pkv * pkv, axis=-1, keepdims=True) + RMS_EPS) * kvg_ref[...]

    ppe = _dot(xb, wall_ref[:, A_COLS + Q_LORA + KV_LORA:]) * rope
    kpe_ref[...] = ppe[:, :ROPE_DIM] + ppe[:, ROPE_DIM:2 * ROPE_DIM]


def _proj(x2, shift0, w, rope, *, nstreams, bps, rows, chunk):
    total = nstreams * bps * rows
    nck = rows // chunk
    tri = min(rows, TRI_ROWS)
    ltri = _chunk_tri(tri, chunk)
    rowblk = lambda n: pl.BlockSpec((rows, n), lambda s, b: (s * bps + b, 0))
    in_specs = [
        rowblk(D_MODEL),
        pl.BlockSpec((1, 1, A_COLS), lambda s, b: (s, 0, 0)),
        _full((D_MODEL, PROJ_COLS)),
        _full((1, A_COLS)), _full((1, D_A)), _full((1, D_A)), _full((1, D_A)), _full((1, D_A)), _full((1, D_A)),
        _full((LANES, D_A)), _full((LANES, D_A)), _full((GATE_LORA, D_A)),
        _full((D_A, D_A)), _full((tri, tri)),
        _full((1, Q_LORA)), _full((1, KV_LORA)),
        _full((Q_LORA, B_HEADS * QK_PAD)), _full((Q_LORA, B_HEADS * QK_PAD)),
        rowblk(LANES),
    ]
    f32o = lambda n: jax.ShapeDtypeStruct((total, n), F32)
    out_shape = [f32o(D_A)] * 7 + [
        jax.ShapeDtypeStruct((total // chunk * 8, D_A), F32),
        jax.ShapeDtypeStruct((nstreams, 1, A_COLS), F32),
        jax.ShapeDtypeStruct((total, B_HEADS * QK_PAD), BF16),
        f32o(KV_LORA), f32o(ROPE_DIM),
    ]
    out_specs = [rowblk(D_A)] * 7 + [
        pl.BlockSpec((nck * 8, D_A), lambda s, b: (s * bps + b, 0)),
        pl.BlockSpec((1, 1, A_COLS), lambda s, b: (s, 0, 0)),
        rowblk(B_HEADS * QK_PAD), rowblk(KV_LORA), rowblk(ROPE_DIM),
    ]
    return pl.pallas_call(
        functools.partial(_proj_kernel, rows=rows, chunk=chunk),
        out_shape=out_shape,
        grid=(nstreams, bps),
        in_specs=in_specs,
        out_specs=out_specs,
        scratch_shapes=[pltpu.VMEM((1, A_COLS), F32)],
        compiler_params=pltpu.CompilerParams(
            dimension_semantics=("arbitrary", "arbitrary"), vmem_limit_bytes=VMEM_LIMIT),
        name="proj",
    )(x2, shift0, w["wall"], w["mu"], w["w0"], w["a0"], w["k_k"], w["k_a"], w["r_k"],
      w["ww2"], w["wa2"], w["wg2"], w["esum"], ltri, w["qg"], w["kvg"], w["wqa"], w["wqb"],
      rope)


def _chunk_tri(rows, chunk):
    i = np.arange(rows)
    m = (i[:, None] // chunk == i[None, :] // chunk) & (i[None, :] <= i[:, None])
    return jnp.asarray(m, BF16)


def _kvexp_kernel(ckv_ref, kpe_ref, wk_ref, place_ref, wv_ref, k_ref, v_ref):
    c = _bf(ckv_ref[...])
    k_ref[...] = _bf(_dot(c, wk_ref[...]) + _dot(_bf(kpe_ref[...]), place_ref[...]))
    v_ref[...] = _bf(_dot(c, wv_ref[...]))


def _kvexp(ckv, kpe, w, *, rows):
    total = ckv.shape[0]
    rowblk = lambda n: pl.BlockSpec((rows, n), lambda i: (i, 0))
    return pl.pallas_call(
        _kvexp_kernel,
        out_shape=[jax.ShapeDtypeStruct((total, B_HEADS * QK_PAD), BF16),
                   jax.ShapeDtypeStruct((total, D_B), BF16)],
        grid=(total // rows,),
        in_specs=[rowblk(KV_LORA), rowblk(ROPE_DIM), _full((KV_LORA, B_HEADS * QK_PAD)),
                  _full((ROPE_DIM, B_HEADS * QK_PAD)), _full((KV_LORA, D_B))],
        out_specs=[rowblk(B_HEADS * QK_PAD), rowblk(D_B)],
        compiler_params=pltpu.CompilerParams(dimension_semantics=("arbitrary",)),
        name="kvexp",
    )(ckv, kpe, w["wk"], w["place"], w["wv"])


def _attn_kernel(qi_ref, ki_ref, q_ref, k_ref, v_ref, o_ref, m_ref, acc_ref, *,
                 causal, bq, bk, nk, kv_len):
    s_id = pl.program_id(1)
    qi = qi_ref[s_id]
    ki = ki_ref[s_id]
    ratio = bq // bk

    @pl.when(ki == 0)
    def _():
        m_ref[...] = jnp.full(m_ref.shape, -jnp.inf, F32)
        acc_ref[...] = jnp.zeros(acc_ref.shape, F32)

    low = lax.broadcasted_iota(jnp.int32, (1, LANES), 1) < V_DIM

    def scores(h):
        sl = slice(h * QK_PAD, (h + 1) * QK_PAD)
        return _dot_nt(q_ref[0, :, sl], k_ref[0, :, sl])

    def step(mask):
        one = jnp.ones((), BF16)
        ahead = 2
        pending = [scores(h) for h in range(ahead)]
        for h in range(B_HEADS):
            s = pending.pop(0)
            if h + ahead < B_HEADS:
                pending.append(scores(h + ahead))
            vp = v_ref[0, :, (h // 2) * LANES:(h // 2 + 1) * LANES]
            vext = jnp.where(low, vp, one) if h % 2 == 0 else jnp.where(low, one, vp)
            if mask is not None:
                s = jnp.where(mask, s, -jnp.inf)
            m_prev = m_ref[h]
            m_new = jnp.maximum(m_prev, jnp.max(s, axis=-1, keepdims=True))
            p = jnp.exp2(s - m_new[:, :1])
            acc_ref[h] = jnp.exp2(m_prev - m_new) * acc_ref[h] + _dot(_bf(p), vext)
            m_ref[h] = m_new

    if causal:
        @pl.when(ki < ratio * qi)
        def _():
            step(None)

        @pl.when(ki >= ratio * qi)
        def _():
            rq = (lax.broadcasted_iota(jnp.int32, (bq, bk), 0) + qi * bq) // CHUNK
            ck = (lax.broadcasted_iota(jnp.int32, (bq, bk), 1) + ki * bk) // CHUNK
            step(ck <= rq)
        is_last = ki == ratio * (qi + 1) - 1
    else:
        if kv_len < nk * bk:
            col = lax.broadcasted_iota(jnp.int32, (bq, bk), 1) + ki * bk
            step(col < kv_len)
        else:
            step(None)
        is_last = ki == nk - 1

    @pl.when(is_last)
    def _():
        for j in range(B_HEADS // 2):
            a0 = acc_ref[2 * j]
            a1 = acc_ref[2 * j + 1]
            num = jnp.where(low, a0, a1)
            den = jnp.where(low, pltpu.roll(a0, V_DIM, axis=1), pltpu.roll(a1, V_DIM, axis=1))
            o_ref[0, :, j * LANES:(j + 1) * LANES] = num / den


def _attn(q, k, v, *, causal, bq, bk, kv_len):
    nb, tq, _ = q.shape
    tk = k.shape[1]
    nq, nk = tq // bq, tk // bk
    ratio = bq // bk
    steps = [(i, j) for i in range(nq) for j in range(ratio * (i + 1) if causal else nk)]
    qi = jnp.asarray([s[0] for s in steps], jnp.int32)
    ki = jnp.asarray([s[1] for s in steps], jnp.int32)
    grid_spec = pltpu.PrefetchScalarGridSpec(
        num_scalar_prefetch=2,
        grid=(nb, len(steps)),
        in_specs=[pl.BlockSpec((1, bq, B_HEADS * QK_PAD), lambda b, s, qi, ki: (b, qi[s], 0)),
                  pl.BlockSpec((1, bk, B_HEADS * QK_PAD), lambda b, s, qi, ki: (b, ki[s], 0)),
                  pl.BlockSpec((1, bk, D_B), lambda b, s, qi, ki: (b, ki[s], 0))],
        out_specs=pl.BlockSpec((1, bq, D_B), lambda b, s, qi, ki: (b, qi[s], 0)),
        scratch_shapes=[pltpu.VMEM((B_HEADS, bq, LANES), F32),
                        pltpu.VMEM((B_HEADS, bq, LANES), F32)])
    return pl.pallas_call(
        functools.partial(_attn_kernel, causal=causal, bq=bq, bk=bk, nk=nk, kv_len=kv_len),
        out_shape=jax.ShapeDtypeStruct((nb, tq, D_B), F32),
        grid_spec=grid_spec,
        compiler_params=pltpu.CompilerParams(
            dimension_semantics=("arbitrary", "arbitrary"), vmem_limit_bytes=VMEM_LIMIT),
        name="attn",
    )(qi, ki, q, k, v)


def _wkv_kernel(rt_ref, kt_ref, bt_ref, at_ref, v_ref, g_ref, bonus_ref, wc_ref, lg_ref, lb_ref,
                eavg_ref, h0_ref, y_ref, hout_ref, h_ref, *, chunk, group, nsteps):
    c = pl.program_id(1)
    C2 = 2 * chunk
    npair = A_HEADS // 2

    @pl.when(c == 0)
    def _():
        h_ref[...] = h0_ref[0]

    low = lax.broadcasted_iota(jnp.int32, (chunk, LANES), 1) < A_HEAD_DIM
    ii = lax.broadcasted_iota(jnp.int32, (C2, C2), 0)
    jj = lax.broadcasted_iota(jnp.int32, (C2, C2), 1)
    strict = ii > jj
    incl = ii >= jj
    eye_c = (ii == jj).astype(F32)
    ki = lax.broadcasted_iota(jnp.int32, (LANES, LANES), 0)
    kj = lax.broadcasted_iota(jnp.int32, (LANES, LANES), 1)
    eye_k = (ki == kj).astype(F32)
    eavg = eavg_ref[...]

    def stack(t):
        return jnp.concatenate([jnp.where(low, t, 0.0), jnp.where(low, 0.0, t)], axis=0)

    units = [(ci, j) for ci in range(group) for j in range(npair)]
    rows = lambda ci: slice(ci * chunk, (ci + 1) * chunk)
    lanes = lambda j: slice(j * LANES, (j + 1) * LANES)
    ld = lambda ref: [stack(ref[rows(ci), lanes(j)]) for ci, j in units]
    At, Bt, Kt, Rt, Vs = ld(at_ref), ld(bt_ref), ld(kt_ref), ld(rt_ref), ld(v_ref)
    nu = range(len(units))
    Vb = [_bf(Vs[u]) for u in nu]
    g1 = [_dot_nt(_bf(jnp.concatenate([At[u], Rt[u]], axis=0)),
                  _bf(jnp.concatenate([Bt[u], Kt[u]], axis=0))) for u in nu]
    Aab = [jnp.where(strict, g1[u][:C2, :C2], 0.0) for u in nu]
    Aak = [_bf(jnp.where(strict, g1[u][:C2, C2:], 0.0)) for u in nu]
    Arb = [_bf(jnp.where(incl, g1[u][C2:, :C2], 0.0)) for u in nu]
    Ark = [_bf(jnp.where(incl, g1[u][C2:, C2:], 0.0)) for u in nu]
    rcat = lambda *t: jnp.concatenate(t, axis=0)
    Tm = [eye_c + Aab[u] for u in nu]
    Nb = [_bf(Aab[u]) for u in nu]
    Pw = [_dot(Nb[u], Nb[u]) for u in nu]
    for _ in range(chunk.bit_length() - 3):
        Pb = [_bf(Pw[u]) for u in nu]
        st = [_dot(rcat(_bf(Tm[u]), Pb[u]), Pb[u]) for u in nu]
        Tm = [Tm[u] + st[u][:C2] for u in nu]
        Pw = [st[u][C2:] for u in nu]
    Tm = [Tm[u] + _dot(_bf(Tm[u]), _bf(Pw[u])) for u in nu]
    wrow = [wc_ref[ci * 8:ci * 8 + 1, lanes(j)] for ci, j in units]
    BwT = [_bf((Bt[u] * wrow[u]).T) for u in nu]
    KwT = [_bf((Kt[u] * wrow[u]).T) for u in nu]
    sv = [_dot(rcat(Aak[u], Ark[u], KwT[u]), Vb[u]) for u in nu]
    PPb = [_bf(_dot(_bf(Tm[u]), _bf(jnp.concatenate([At[u], sv[u][:C2]], axis=1)))) for u in nu]
    sp = [_dot(rcat(Arb[u], BwT[u]), PPb[u]) for u in nu]
    Q1M1, Q2, M2 = [], [], []
    for u in nu:
        q1s = Rt[u] + sp[u][:C2, :LANES]
        q2s = sp[u][:C2, LANES:] + sv[u][C2:2 * C2]
        m1 = eye_k * wrow[u] + sp[u][C2:, :LANES]
        Q1M1.append(_bf(rcat(q1s[:chunk] + q1s[chunk:], m1)))
        Q2.append(q2s[:chunk] + q2s[chunk:])
        M2.append(sp[u][C2:, LANES:] + sv[u][2 * C2:])

    H = [h_ref[j] for j in range(npair)]
    Y = []
    for u, (ci, j) in enumerate(units):
        sh = _dot(Q1M1[u], _bf(H[j]))
        Y.append(sh[:chunk] + Q2[u])
        H[j] = sh[chunk:] + M2[u]
    for j in range(npair):
        h_ref[j] = H[j]

    def headmean(t):
        hi, lo = _split2(t)
        m = _dot(rcat(hi, lo), eavg)
        return m[:chunk] + m[chunk:]

    mu = [headmean(Y[u]) for u in nu]
    dv = [Y[u] - mu[u] for u in nu]
    var = [headmean(dv[u] * dv[u]) for u in nu]
    for u, (ci, j) in enumerate(units):
        yn = dv[u] * lax.rsqrt(var[u] + LNX_EPS) * lg_ref[:, lanes(j)] + lb_ref[:, lanes(j)]
        y_ref[rows(ci), lanes(j)] = (yn + bonus_ref[rows(ci), lanes(j)]) * g_ref[rows(ci), lanes(j)]

    @pl.when(c == nsteps - 1)
    def _():
        hout_ref[0] = h_ref[...]


def _wkv(rt, kt, bt, at, v, g, bonus, wc, h0, w, *, nstreams, ncs, chunk, group):
    total = rt.shape[0]
    nsteps = ncs // group
    blk = pl.BlockSpec((group * chunk, D_A), lambda s, c: (s * nsteps + c, 0))
    hspec = pl.BlockSpec((1, A_HEADS // 2, LANES, LANES), lambda s, c: (s, 0, 0, 0))
    return pl.pallas_call(
        functools.partial(_wkv_kernel, chunk=chunk, group=group, nsteps=nsteps),
        out_shape=[jax.ShapeDtypeStruct((total, D_A), F32),
                   jax.ShapeDtypeStruct((nstreams, A_HEADS // 2, LANES, LANES), F32)],
        grid=(nstreams, nsteps),
        in_specs=[blk] * 7 + [pl.BlockSpec((group * 8, D_A), lambda s, c: (s * nsteps + c, 0)),
                              _full((1, D_A)), _full((1, D_A)), _full((LANES, LANES)), hspec],
        out_specs=[blk, hspec],
        scratch_shapes=[pltpu.VMEM((A_HEADS // 2, LANES, LANES), F32)],
        compiler_params=pltpu.CompilerParams(
            dimension_semantics=("arbitrary", "arbitrary"), vmem_limit_bytes=VMEM_LIMIT),
        name="wkv",
    )(rt, kt, bt, at, v, g, bonus, wc, w["lnx_g"], w["lnx_b"], w["eavg"], h0)


def _layer_norm(t, g, b):
    mu = jnp.mean(t, axis=-1, keepdims=True)
    d = t - mu
    var = jnp.mean(d * d, axis=-1, keepdims=True)
    return d * lax.rsqrt(var + LN_EPS) * g + b


def _tail_kernel(x_ref, ya_ref, yb_ref, wg_ref, bg_ref, wpa_ref, wpb_ref, wo_ref, l1g_ref, l1b_ref,
                 wgu_ref, wdown_ref, l2g_ref, l2b_ref, o_ref):
    x = x_ref[...]
    gates = _sigmoid(_dot(_bf(x), wg_ref[...]) + bg_ref[...])
    m = (gates[:, :D_MODEL] * _dot(_bf(ya_ref[...]), wpa_ref[...])
         + gates[:, D_MODEL:] * _dot(_bf(yb_ref[...]), wpb_ref[...]))
    h = _layer_norm(DN_ALPHA * x + _dot(_bf(m), wo_ref[...]), l1g_ref[...], l1b_ref[...])
    hb = _bf(h)

    def gate_up(c):
        cols = slice(c * FF_CHUNK, (c + 1) * FF_CHUNK)
        ucols = slice(D_FF + c * FF_CHUNK, D_FF + (c + 1) * FF_CHUNK)
        return _dot(hb, wgu_ref[:, cols]), _dot(hb, wgu_ref[:, ucols])

    f = None
    nxt = gate_up(0)
    for c in range(N_FF):
        gate, up = nxt
        if c + 1 < N_FF:
            nxt = gate_up(c + 1)
        d = _dot(_bf(gate * _sigmoid(gate) * up), wdown_ref[c * FF_CHUNK:(c + 1) * FF_CHUNK, :])
        f = d if f is None else f + d
    o_ref[...] = _layer_norm(DN_ALPHA * h + f, l2g_ref[...], l2b_ref[...])


def _tail(x2, ya, yb, w, *, rows):
    total = x2.shape[0]
    rowblk = lambda n: pl.BlockSpec((rows, n), lambda i: (i, 0))

    def const(shape):
        n = len(shape)
        return pl.BlockSpec(shape, lambda i: (0,) * n, pipeline_mode=pl.Buffered(1))

    return pl.pallas_call(
        _tail_kernel,
        out_shape=jax.ShapeDtypeStruct((total, D_MODEL), F32),
        grid=(total // rows,),
        in_specs=[rowblk(D_MODEL), rowblk(D_A), rowblk(D_B),
                  const((D_MODEL, 2 * D_MODEL)), const((1, 2 * D_MODEL)),
                  const((D_A, D_MODEL)), const((D_B, D_MODEL)), const((D_MODEL, D_MODEL)),
                  const((1, D_MODEL)), const((1, D_MODEL)),
                  const((D_MODEL, 2 * D_FF)), const((D_FF, D_MODEL)),
                  const((1, D_MODEL)), const((1, D_MODEL))],
        out_specs=rowblk(D_MODEL),
        compiler_params=pltpu.CompilerParams(
            dimension_semantics=("arbitrary",), vmem_limit_bytes=VMEM_LIMIT),
        name="tail",
    )(x2, ya, yb, w["wg"], w["bg"], w["wpa"], w["wpb"], w["wo"], w["l1g"], w["l1b"],
      w["wgu"], w["wdown"], w["l2g"], w["l2b"])


def _prep_weights(w_in, mu_shift, w0, w_w2, a0, w_a2, w_g2, k_k, k_a, r_k, lnx_g, lnx_b, w_pa,
                  q_norm_g, w_uq, kv_norm_g, w_ukv, w_pb, b_gate, w_o, ln1_g, ln1_b, w_gu, w_down,
                  ln2_g, ln2_b):
    row = lambda t: t.reshape(1, -1).astype(F32)
    nb = A_COLS + B_COLS
    pe = w_in[:, nb - ROPE_DIM:nb]
    half = ROPE_DIM // 2
    pe_sw = jnp.concatenate([pe[:, half:], pe[:, :half]], axis=1)
    wall = jnp.concatenate([w_in[:, :nb - ROPE_DIM], pe, pe_sw,
                            jnp.zeros((D_MODEL, LANES - 2 * ROPE_DIM), F32)], axis=1)
    uq = w_uq.reshape(Q_LORA, B_HEADS, NOPE_DIM + ROPE_DIM)
    nope, r1, r2 = uq[..., :NOPE_DIM], uq[..., NOPE_DIM:NOPE_DIM + half], uq[..., NOPE_DIM + half:]
    zpad = jnp.zeros((Q_LORA, B_HEADS, QK_PAD - NOPE_DIM - ROPE_DIM), F32)
    wqa = jnp.concatenate([nope, r1, r2, zpad], axis=-1).reshape(Q_LORA, B_HEADS * QK_PAD)
    wqb = jnp.concatenate([jnp.zeros_like(nope), r2, r1, zpad], axis=-1).reshape(Q_LORA, B_HEADS * QK_PAD)
    ukv = w_ukv.reshape(KV_LORA, B_HEADS, NOPE_DIM + V_DIM)
    wk = jnp.concatenate([ukv[..., :NOPE_DIM], jnp.zeros((KV_LORA, B_HEADS, QK_PAD - NOPE_DIM), F32)],
                         axis=-1).reshape(KV_LORA, B_HEADS * QK_PAD)
    wv = ukv[..., NOPE_DIM:].reshape(KV_LORA, D_B)
    place = np.zeros((ROPE_DIM, B_HEADS * QK_PAD), np.float32)
    for h in range(B_HEADS):
        place[np.arange(ROPE_DIM), h * QK_PAD + NOPE_DIM + np.arange(ROPE_DIM)] = 1.0
    hid = np.arange(D_A) // A_HEAD_DIM
    esum = (hid[:, None] == hid[None, :]).astype(np.float32)
    lid = np.arange(LANES) // A_HEAD_DIM
    eavg = (lid[:, None] == lid[None, :]).astype(np.float32) / A_HEAD_DIM
    zl = jnp.zeros((LANES - DECAY_LORA, D_A), F32)
    return {
        "wall": _bf(wall), "mu": row(mu_shift), "w0": row(w0), "a0": row(a0), "k_k": row(k_k),
        "k_a": row(k_a), "r_k": row(r_k),
        "ww2": _bf(jnp.concatenate([w_w2, zl], axis=0)), "wa2": _bf(jnp.concatenate([zl, w_a2], axis=0)),
        "wg2": _bf(w_g2), "esum": jnp.asarray(esum, BF16), "eavg": jnp.asarray(eavg, BF16),
        "qg": row(q_norm_g), "kvg": row(kv_norm_g), "wqa": _bf(wqa), "wqb": _bf(wqb),
        "wk": _bf(wk), "wv": _bf(wv), "place": jnp.asarray(place, BF16),
        "lnx_g": row(lnx_g), "lnx_b": row(lnx_b),
        "wg": _bf(w_in[:, nb:]), "bg": row(b_gate), "wpa": _bf(w_pa), "wpb": _bf(w_pb), "wo": _bf(w_o),
        "l1g": row(ln1_g), "l1b": row(ln1_b), "l2g": row(ln2_g), "l2b": row(ln2_b),
        "wgu": _bf(w_gu), "wdown": _bf(w_down),
    }


def _rope_table(pos, reps):
    half = ROPE_DIM // 2
    inv = ROPE_BASE ** (-jnp.arange(half, dtype=F32) / half)
    ang = pos.astype(F32)[:, None] * jnp.tile(inv, LANES // half)[None, :]
    grp = np.arange(LANES) // half
    sc = np.where(grp >= 4, SCORE_SCALE, 1.0)
    mc = jnp.asarray(np.where(grp % 4 < 2, sc, 0.0), F32)
    ms = jnp.asarray(np.where(grp % 4 == 2, -sc, np.where(grp % 4 == 3, sc, 0.0)), F32)
    return jnp.tile(jnp.cos(ang) * mc + jnp.sin(ang) * ms, (reps, 1))


def _state_to_pairs(s):
    nb = s.shape[0]
    st = jnp.swapaxes(s, -1, -2).reshape(nb, A_HEADS // 2, 2, A_HEAD_DIM, A_HEAD_DIM)
    z = jnp.zeros_like(st[:, :, 0])
    top = jnp.concatenate([st[:, :, 0], z], axis=-1)
    bot = jnp.concatenate([z, st[:, :, 1]], axis=-1)
    return jnp.concatenate([top, bot], axis=-2)


def _pairs_to_state(hp):
    nb = hp.shape[0]
    diag = jnp.stack([hp[:, :, :A_HEAD_DIM, :A_HEAD_DIM], hp[:, :, A_HEAD_DIM:, A_HEAD_DIM:]], axis=2)
    return jnp.swapaxes(diag.reshape(nb, A_HEADS, A_HEAD_DIM, A_HEAD_DIM), -1, -2)


def _layer(x, pos0, shift0, wkv0, cache, w):
    nstreams, t, _ = x.shape
    total = nstreams * t
    x2 = x.reshape(total, D_MODEL)
    chunk = min(CHUNK, t)
    rows = min(PROJ_ROWS, t)
    bps = t // rows
    rope = _rope_table(pos0 + jnp.arange(t), nstreams)
    (rt, kt, bt, at, v, g, bonus, wc, shift, q, ckv, kpe) = _proj(
        x2, shift0, w, rope, nstreams=nstreams, bps=bps, rows=rows, chunk=chunk)

    ya, hout = _wkv(rt, kt, bt, at, v, g, bonus, wc, _state_to_pairs(wkv0), w,
                    nstreams=nstreams, ncs=t // chunk, chunk=chunk, group=min(WKV_GROUP, t // chunk))

    if cache is None:
        kk, vv = _kvexp(ckv, kpe, w, rows=min(512, total))
        yb = _attn(q.reshape(nstreams, t, -1), kk.reshape(nstreams, t, -1), vv.reshape(nstreams, t, -1),
                   causal=True, bq=min(ATTN_BQ, t), bk=min(ATTN_BK, t), kv_len=t)
    else:
        cache_ckv, cache_kpe = cache
        past = cache_ckv.shape[1]
        kv_len = past + t
        tk = -(-kv_len // LANES) * LANES
        cat = lambda c, n, d: jnp.concatenate(
            [c.astype(F32), n.reshape(nstreams, t, d), jnp.zeros((nstreams, tk - kv_len, d), F32)],
            axis=1).reshape(nstreams * tk, d)
        kk, vv = _kvexp(cat(cache_ckv, ckv, KV_LORA), cat(cache_kpe, kpe, ROPE_DIM), w, rows=512)
        yb = _attn(q.reshape(nstreams, t, -1), kk.reshape(nstreams, tk, -1), vv.reshape(nstreams, tk, -1),
                   causal=False, bq=t, bk=tk, kv_len=kv_len)

    y = _tail(x2, ya, yb.reshape(total, D_B), w, rows=min(TAIL_ROWS, total))
    return (y.reshape(nstreams, t, D_MODEL), ckv.reshape(nstreams, t, KV_LORA),
            kpe.reshape(nstreams, t, ROPE_DIM), _pairs_to_state(hout), shift)


def kernel(x_prompt, x_sample, cache_ckv, cache_kpe, state_wkv, state_shift, w_in, mu_shift, w0, w_w2, a0,
           w_a2, w_g2, k_k, k_a, r_k, lnx_g, lnx_b, w_pa, q_norm_g, w_uq, kv_norm_g, w_ukv, w_pb, b_gate,
           w_o, ln1_g, ln1_b, w_gu, w_down, ln2_g, ln2_b):
    w = _prep_weights(w_in, mu_shift, w0, w_w2, a0, w_a2, w_g2, k_k, k_a, r_k, lnx_g, lnx_b, w_pa,
                      q_norm_g, w_uq, kv_norm_g, w_ukv, w_pb, b_gate, w_o, ln1_g, ln1_b, w_gu, w_down,
                      ln2_g, ln2_b)
    bp = x_prompt.shape[0]
    y_p, ckv_p, kpe_p, wkv_p, shift_p = _layer(
        x_prompt, 0, jnp.zeros((bp, 1, A_COLS), F32),
        jnp.zeros((bp, A_HEADS, A_HEAD_DIM, A_HEAD_DIM), F32), None, w)
    y_s, ckv_s, kpe_s, wkv_s, shift_s = _layer(
        x_sample, cache_ckv.shape[1], state_shift, state_wkv, (cache_ckv, cache_kpe), w)
    return (y_p, y_s, ckv_p, kpe_p, wkv_p, shift_p, ckv_s, kpe_s, wkv_s, shift_s)
```

```python
import functools

import numpy as np
import jax
import jax.numpy as jnp
from jax import lax
from jax.experimental import pallas as pl
from jax.experimental.pallas import tpu as pltpu

D_MODEL = 1024
CHUNK = 64
A_HEADS = 8
A_HEAD_DIM = 64
D_A = 512
DECAY_LORA = 64
AAA_LORA = 64
GATE_LORA = 128
A_COLS = 3 * D_A + DECAY_LORA + AAA_LORA + GATE_LORA
LNX_EPS = A_HEAD_DIM * 1e-5
B_HEADS = 8
Q_LORA = 256
KV_LORA = 128
NOPE_DIM = 64
ROPE_DIM = 32
V_DIM = 64
D_B = 512
B_COLS = Q_LORA + KV_LORA + ROPE_DIM
ROPE_BASE = 10000.0
ATTN_SCALE = (NOPE_DIM + ROPE_DIM) ** -0.5
SCORE_SCALE = ATTN_SCALE * float(np.log2(np.e))
RMS_EPS = 1e-6
D_FF = 2816
LN_EPS = 1e-5
DN_ALPHA = 2.0 ** 0.25

LANES = 128
QK_PAD = 128
PROJ_COLS = A_COLS + Q_LORA + KV_LORA + LANES
FF_CHUNK = 256
N_FF = D_FF // FF_CHUNK
TRI_ROWS = 256
PROJ_ROWS = 512
TAIL_ROWS = 512
ATTN_BLOCK = 1024
WKV_UNITS = 4
VMEM_LIMIT = 56 * 1024 * 1024

F32 = jnp.float32
BF16 = jnp.bfloat16


def _dot(a, b):
    return jnp.dot(a, b, preferred_element_type=F32)


def _dot_nt(a, b):
    return lax.dot_general(a, b, (((1,), (1,)), ((), ())), preferred_element_type=F32)


def _bf(x):
    return x.astype(BF16)


def _split2(x):
    hi = x.astype(BF16)
    lo = (x - hi.astype(F32)).astype(BF16)
    return hi, lo


def _split3(x):
    h1 = x.astype(BF16)
    r1 = x - h1.astype(F32)
    h2 = r1.astype(BF16)
    h3 = (r1 - h2.astype(F32)).astype(BF16)
    return h1, h2, h3


def _sigmoid(z):
    return 1.0 / (1.0 + jnp.exp(-z))


def _full(shape):
    n = len(shape)
    return pl.BlockSpec(shape, lambda *_: (0,) * n)


def _proj_kernel(x_ref, shift0_ref, wall_ref, mu_ref, w0_ref, a0_ref, kk_ref, ka_ref, rk_ref,
                 ww2_ref, wa2_ref, wg2_ref, esum_ref, ltri_ref, qg_ref, kvg_ref, wqa_ref, wqb_ref,
                 rope_ref,
                 rt_ref, kt_ref, bt_ref, at_ref, v_ref, g_ref, bonus_ref, wc_ref, shift_ref,
                 q_ref, ckv_ref, kpe_ref, carry_ref, *, rows, chunk, seg):
    b = pl.program_id(1)
    xb = _bf(x_ref[...])

    pa = _dot(xb, wall_ref[:, :A_COLS])
    row = lax.broadcasted_iota(jnp.int32, (rows, 1), 0)
    if seg >= rows:
        first = jnp.where(b == 0, shift0_ref[0], carry_ref[...])
        starts = row == 0
        last = pa[rows - 1:rows, :]
        carry_ref[...] = last
        shift_ref[0] = last
    else:
        first = jnp.broadcast_to(shift0_ref[...], (rows // seg, seg, A_COLS)).reshape(rows, A_COLS)
        starts = row % seg == 0
        for s in range(rows // seg):
            shift_ref[s] = pa[(s + 1) * seg - 1:(s + 1) * seg, :]
    prev = jnp.where(starts, first, pltpu.roll(pa, 1, axis=0))
    xs = pa + (prev - pa) * mu_ref[...]

    r = xs[:, :D_A]
    k = xs[:, D_A:2 * D_A]
    v = xs[:, 2 * D_A:3 * D_A]
    wa = xs[:, 3 * D_A:3 * D_A + LANES]
    gd = xs[:, 3 * D_A + LANES:]

    z = w0_ref[...] + _dot(_bf(jnp.tanh(wa)), ww2_ref[...])
    ld = -np.float32(np.exp(-0.5)) * _sigmoid(z)
    a = _sigmoid(a0_ref[...] + _dot(_bf(wa), wa2_ref[...]))
    g_ref[...] = _dot(_bf(_sigmoid(gd)), wg2_ref[...])

    kkr = k * kk_ref[...]
    kh = k * (1.0 + (a - 1.0) * ka_ref[...])
    hi, lo = _split2(kkr * kkr)
    hs = _dot(jnp.concatenate([hi, lo, _bf(r * kh * rk_ref[...])], axis=0), esum_ref[...])
    kk = kkr / jnp.maximum(jnp.sqrt(hs[:rows] + hs[rows:2 * rows]), 1e-12)
    bonus_ref[...] = hs[2 * rows:] * v
    v_ref[...] = v

    ltri = ltri_ref[...]
    tri = ltri.shape[0]
    h1, h2, h3 = _split3(ld)
    cum = jnp.concatenate(
        [_dot(ltri, h1[i:i + tri]) + _dot(ltri, h2[i:i + tri]) + _dot(ltri, h3[i:i + tri])
         for i in range(0, rows, tri)], axis=0)
    ep = jnp.exp(cum)
    em = jnp.exp(-cum)
    rt_ref[...] = r * ep
    kt_ref[...] = kh * em
    bt_ref[...] = (kk * a) * em
    at_ref[...] = -kk * jnp.exp(cum - ld)
    for c in range(rows // chunk):
        wc_ref[c * 8:(c + 1) * 8, :] = jnp.broadcast_to(ep[(c + 1) * chunk - 1:(c + 1) * chunk, :], (8, D_A))

    pq = _dot(xb, wall_ref[:, A_COLS:A_COLS + Q_LORA])
    cqn = _bf(pq * lax.rsqrt(jnp.mean(pq * pq, axis=-1, keepdims=True) + RMS_EPS) * qg_ref[...])
    qa = _dot(cqn, wqa_ref[...])
    qb = _dot(cqn, wqb_ref[...])
    rope = rope_ref[...]
    lane = lax.broadcasted_iota(jnp.int32, (1, QK_PAD), 1)
    cq = jnp.where(lane < NOPE_DIM, np.float32(SCORE_SCALE), rope)
    sq = pltpu.roll(rope, QK_PAD - ROPE_DIM, axis=1)
    for h in range(B_HEADS):
        sl = slice(h * QK_PAD, (h + 1) * QK_PAD)
        q_ref[:, sl] = _bf(qa[:, sl] * cq + qb[:, sl] * sq)

    pkv = _dot(xb, wall_ref[:, A_COLS + Q_LORA:A_COLS + Q_LORA + KV_LORA])
    ckv_ref[...] = pkv * lax.rsqrt(jnp.mean(pkv * pkv, axis=-1, keepdims=True) + RMS_EPS) * kvg_ref[...]

    ppe = _dot(xb, wall_ref[:, A_COLS + Q_LORA + KV_LORA:]) * rope
    kpe_ref[...] = ppe[:, :ROPE_DIM] + ppe[:, ROPE_DIM:2 * ROPE_DIM]


def _proj(x2, shift0, w, rope, *, nstreams, seg, rows, chunk):
    total = nstreams * seg
    spb = max(1, rows // seg)
    bps = max(1, seg // rows)
    nck = rows // chunk
    tri = min(rows, TRI_ROWS)
    ltri = _chunk_tri(tri, chunk)
    rowblk = lambda n: pl.BlockSpec((rows, n), lambda s, b: (s * bps + b, 0))
    in_specs = [
        rowblk(D_MODEL),
        pl.BlockSpec((spb, 1, A_COLS), lambda s, b: (s, 0, 0)),
        _full((D_MODEL, PROJ_COLS)),
        _full((1, A_COLS)), _full((1, D_A)), _full((1, D_A)), _full((1, D_A)), _full((1, D_A)), _full((1, D_A)),
        _full((LANES, D_A)), _full((LANES, D_A)), _full((GATE_LORA, D_A)),
        _full((D_A, D_A)), _full((tri, tri)),
        _full((1, Q_LORA)), _full((1, KV_LORA)),
        _full((Q_LORA, B_HEADS * QK_PAD)), _full((Q_LORA, B_HEADS * QK_PAD)),
        rowblk(LANES),
    ]
    f32o = lambda n: jax.ShapeDtypeStruct((total, n), F32)
    out_shape = [f32o(D_A)] * 7 + [
        jax.ShapeDtypeStruct((total // chunk * 8, D_A), F32),
        jax.ShapeDtypeStruct((nstreams, 1, A_COLS), F32),
        jax.ShapeDtypeStruct((total, B_HEADS * QK_PAD), BF16),
        f32o(KV_LORA), f32o(ROPE_DIM),
    ]
    out_specs = [rowblk(D_A)] * 7 + [
        pl.BlockSpec((nck * 8, D_A), lambda s, b: (s * bps + b, 0)),
        pl.BlockSpec((spb, 1, A_COLS), lambda s, b: (s, 0, 0)),
        rowblk(B_HEADS * QK_PAD), rowblk(KV_LORA), rowblk(ROPE_DIM),
    ]
    return pl.pallas_call(
        functools.partial(_proj_kernel, rows=rows, chunk=chunk, seg=seg),
        out_shape=out_shape,
        grid=(nstreams // spb, bps),
        in_specs=in_specs,
        out_specs=out_specs,
        scratch_shapes=[pltpu.VMEM((1, A_COLS), F32)],
        compiler_params=pltpu.CompilerParams(
            dimension_semantics=("arbitrary", "arbitrary"), vmem_limit_bytes=VMEM_LIMIT),
        name="proj",
    )(x2, shift0, w["wall"], w["mu"], w["w0"], w["a0"], w["k_k"], w["k_a"], w["r_k"],
      w["ww2"], w["wa2"], w["wg2"], w["esum"], ltri, w["qg"], w["kvg"], w["wqa"], w["wqb"],
      rope)


def _chunk_tri(rows, chunk):
    i = np.arange(rows)
    m = (i[:, None] // chunk == i[None, :] // chunk) & (i[None, :] <= i[:, None])
    return jnp.asarray(m, BF16)


def _kvexp_kernel(ckv_ref, kpe_ref, wk_ref, place_ref, wv_ref, k_ref, v_ref):
    c = _bf(ckv_ref[...])
    k_ref[...] = _bf(_dot(c, wk_ref[...]) + _dot(_bf(kpe_ref[...]), place_ref[...]))
    v_ref[...] = _bf(_dot(c, wv_ref[...]))


def _kvexp(ckv, kpe, w, *, rows):
    total = ckv.shape[0]
    rowblk = lambda n: pl.BlockSpec((rows, n), lambda i: (i, 0))
    return pl.pallas_call(
        _kvexp_kernel,
        out_shape=[jax.ShapeDtypeStruct((total, B_HEADS * QK_PAD), BF16),
                   jax.ShapeDtypeStruct((total, D_B), BF16)],
        grid=(total // rows,),
        in_specs=[rowblk(KV_LORA), rowblk(ROPE_DIM), _full((KV_LORA, B_HEADS * QK_PAD)),
                  _full((ROPE_DIM, B_HEADS * QK_PAD)), _full((KV_LORA, D_B))],
        out_specs=[rowblk(B_HEADS * QK_PAD), rowblk(D_B)],
        compiler_params=pltpu.CompilerParams(dimension_semantics=("arbitrary",)),
        name="kvexp",
    )(ckv, kpe, w["wk"], w["place"], w["wv"])


def _attn_kernel(qi_ref, ki_ref, q_ref, k_ref, v_ref, o_ref, m_ref, acc_ref, *, blk):
    s_id = pl.program_id(1)
    qi = qi_ref[s_id]
    ki = ki_ref[s_id]

    @pl.when(ki == 0)
    def _():
        m_ref[...] = jnp.full(m_ref.shape, -jnp.inf, F32)
        acc_ref[...] = jnp.zeros(acc_ref.shape, F32)

    low = lax.broadcasted_iota(jnp.int32, (1, LANES), 1) < V_DIM

    def step(r0, nr, nc, masked):
        rs = slice(r0, r0 + nr)

        def scores(h):
            sl = slice(h * QK_PAD, (h + 1) * QK_PAD)
            return _dot_nt(q_ref[0, rs, sl], k_ref[0, :nc, sl])

        if masked:
            rq = (lax.broadcasted_iota(jnp.int32, (nr, nc), 0) + r0) // CHUNK
            mask = lax.broadcasted_iota(jnp.int32, (nr, nc), 1) // CHUNK <= rq
        one = jnp.ones((), BF16)
        ahead = 2
        pending = [scores(h) for h in range(ahead)]
        for h in range(B_HEADS):
            s = pending.pop(0)
            if h + ahead < B_HEADS:
                pending.append(scores(h + ahead))
            vp = v_ref[0, :nc, (h // 2) * LANES:(h // 2 + 1) * LANES]
            vext = jnp.where(low, vp, one) if h % 2 == 0 else jnp.where(low, one, vp)
            if masked:
                s = jnp.where(mask, s, -jnp.inf)
            m_prev = m_ref[h, rs]
            m_new = jnp.maximum(m_prev, jnp.max(s, axis=-1, keepdims=True))
            p = jnp.exp2(s - m_new[:, :1])
            acc_ref[h, rs] = jnp.exp2(m_prev - m_new) * acc_ref[h, rs] + _dot(_bf(p), vext)
            m_ref[h, rs] = m_new

    @pl.when(ki < qi)
    def _():
        step(0, blk, blk, False)

    @pl.when(ki == qi)
    def _():
        half = blk // 2
        step(0, half, half, True)
        step(half, half, blk, True)
        for j in range(B_HEADS // 2):
            a0 = acc_ref[2 * j]
            a1 = acc_ref[2 * j + 1]
            num = jnp.where(low, a0, a1)
            den = jnp.where(low, pltpu.roll(a0, V_DIM, axis=1), pltpu.roll(a1, V_DIM, axis=1))
            o_ref[0, :, j * LANES:(j + 1) * LANES] = num / den


def _attn(q, k, v, *, blk):
    nb, t, _ = q.shape
    steps = [(i, j) for i in range(t // blk) for j in range(i + 1)]
    qi = jnp.asarray([s[0] for s in steps], jnp.int32)
    ki = jnp.asarray([s[1] for s in steps], jnp.int32)
    grid_spec = pltpu.PrefetchScalarGridSpec(
        num_scalar_prefetch=2,
        grid=(nb, len(steps)),
        in_specs=[pl.BlockSpec((1, blk, B_HEADS * QK_PAD), lambda b, s, qi, ki: (b, qi[s], 0)),
                  pl.BlockSpec((1, blk, B_HEADS * QK_PAD), lambda b, s, qi, ki: (b, ki[s], 0)),
                  pl.BlockSpec((1, blk, D_B), lambda b, s, qi, ki: (b, ki[s], 0))],
        out_specs=pl.BlockSpec((1, blk, D_B), lambda b, s, qi, ki: (b, qi[s], 0)),
        scratch_shapes=[pltpu.VMEM((B_HEADS, blk, LANES), F32),
                        pltpu.VMEM((B_HEADS, blk, LANES), F32)])
    return pl.pallas_call(
        functools.partial(_attn_kernel, blk=blk),
        out_shape=jax.ShapeDtypeStruct((nb, t, D_B), F32),
        grid_spec=grid_spec,
        compiler_params=pltpu.CompilerParams(
            dimension_semantics=("arbitrary", "arbitrary"), vmem_limit_bytes=VMEM_LIMIT),
        name="attn",
    )(qi, ki, q, k, v)


def _attnc_kernel(q_ref, cckv_ref, ckpe_ref, nckv_ref, nkpe_ref, wk_ref, wv_ref, place_ref, o_ref, *, t):
    q = q_ref[0]
    heads = [q[:, h * QK_PAD:(h + 1) * QK_PAD] for h in range(B_HEADS)]
    qf = jnp.concatenate(heads, axis=0)
    qa = jnp.concatenate([_bf(_dot_nt(heads[h], wk_ref[:, h * QK_PAD:(h + 1) * QK_PAD]))
                          for h in range(B_HEADS)], axis=0)
    place = place_ref[:, :QK_PAD]

    def scores(ckv, kpe):
        cb = _bf(ckv)
        return _dot_nt(qa, cb) + _dot_nt(qf, _bf(_dot(_bf(kpe), place))), cb

    s_c, cb_c = scores(cckv_ref[0], ckpe_ref[0])
    s_n, cb_n = scores(nckv_ref[...], nkpe_ref[...])
    m = jnp.maximum(jnp.max(s_c, axis=-1, keepdims=True), jnp.max(s_n, axis=-1, keepdims=True))
    p_c = jnp.exp2(s_c - m)
    p_n = jnp.exp2(s_n - m)
    den = jnp.sum(p_c, axis=-1, keepdims=True) + jnp.sum(p_n, axis=-1, keepdims=True)
    lat = _bf((_dot(_bf(p_c), cb_c) + _dot(_bf(p_n), cb_n)) / den)
    low = lax.broadcasted_iota(jnp.int32, (1, LANES), 1) < V_DIM
    zero = jnp.zeros((), BF16)
    for j in range(B_HEADS // 2):
        wvp = wv_ref[:, j * LANES:(j + 1) * LANES]
        o_ref[0, :, j * LANES:(j + 1) * LANES] = (
            _dot(lat[2 * j * t:(2 * j + 1) * t], jnp.where(low, wvp, zero))
            + _dot(lat[(2 * j + 1) * t:(2 * j + 2) * t], jnp.where(low, zero, wvp)))


def _attnc(q, cache_ckv, cache_kpe, ckv, kpe, w):
    nb, t, _ = q.shape
    past = cache_ckv.shape[1]
    return pl.pallas_call(
        functools.partial(_attnc_kernel, t=t),
        out_shape=jax.ShapeDtypeStruct((nb, t, D_B), F32),
        grid=(nb,),
        in_specs=[pl.BlockSpec((1, t, B_HEADS * QK_PAD), lambda b: (b, 0, 0)),
                  pl.BlockSpec((1, past, KV_LORA), lambda b: (b, 0, 0)),
                  pl.BlockSpec((1, past, ROPE_DIM), lambda b: (b, 0, 0)),
                  pl.BlockSpec((t, KV_LORA), lambda b: (b, 0)),
                  pl.BlockSpec((t, ROPE_DIM), lambda b: (b, 0)),
                  _full((KV_LORA, B_HEADS * QK_PAD)), _full((KV_LORA, D_B)),
                  _full((ROPE_DIM, B_HEADS * QK_PAD))],
        out_specs=pl.BlockSpec((1, t, D_B), lambda b: (b, 0, 0)),
        compiler_params=pltpu.CompilerParams(dimension_semantics=("arbitrary",)),
        name="attnc",
    )(q, cache_ckv, cache_kpe, ckv, kpe, w["wk"], w["wv"], w["place"])


def _wkv_kernel(rt_ref, kt_ref, bt_ref, at_ref, v_ref, g_ref, bonus_ref, wc_ref, lg_ref, lb_ref,
                eavg_ref, h0_ref, y_ref, hout_ref, h_ref, *, chunk, group, nsub, nsteps):
    c = pl.program_id(1)
    C2 = 2 * chunk
    npair = A_HEADS // 2

    @pl.when(c == 0)
    def _():
        h_ref[...] = h0_ref[...]

    low = lax.broadcasted_iota(jnp.int32, (chunk, LANES), 1) < A_HEAD_DIM
    ii = lax.broadcasted_iota(jnp.int32, (C2, C2), 0)
    jj = lax.broadcasted_iota(jnp.int32, (C2, C2), 1)
    strict = ii > jj
    incl = ii >= jj
    eye_c = (ii == jj).astype(F32)
    ki = lax.broadcasted_iota(jnp.int32, (LANES, LANES), 0)
    kj = lax.broadcasted_iota(jnp.int32, (LANES, LANES), 1)
    eye_k = (ki == kj).astype(F32)
    eavg = eavg_ref[...]

    def stack(t):
        return jnp.concatenate([jnp.where(low, t, 0.0), jnp.where(low, 0.0, t)], axis=0)

    units = [(ci, j) for ci in range(nsub * group) for j in range(npair)]
    rows = lambda ci: slice(ci * chunk, (ci + 1) * chunk)
    lanes = lambda j: slice(j * LANES, (j + 1) * LANES)
    ld = lambda ref: [stack(ref[rows(ci), lanes(j)]) for ci, j in units]
    At, Bt, Kt, Rt, Vs = ld(at_ref), ld(bt_ref), ld(kt_ref), ld(rt_ref), ld(v_ref)
    nu = range(len(units))
    Vb = [_bf(Vs[u]) for u in nu]
    g1 = [_dot_nt(_bf(jnp.concatenate([At[u], Rt[u]], axis=0)),
                  _bf(jnp.concatenate([Bt[u], Kt[u]], axis=0))) for u in nu]
    Aab = [jnp.where(strict, g1[u][:C2, :C2], 0.0) for u in nu]
    Aak = [_bf(jnp.where(strict, g1[u][:C2, C2:], 0.0)) for u in nu]
    Arb = [_bf(jnp.where(incl, g1[u][C2:, :C2], 0.0)) for u in nu]
    Ark = [_bf(jnp.where(incl, g1[u][C2:, C2:], 0.0)) for u in nu]
    rcat = lambda *t: jnp.concatenate(t, axis=0)
    Tm = [eye_c + Aab[u] for u in nu]
    Nb = [_bf(Aab[u]) for u in nu]
    Pw = [_dot(Nb[u], Nb[u]) for u in nu]
    for _ in range(chunk.bit_length() - 3):
        Pb = [_bf(Pw[u]) for u in nu]
        st = [_dot(rcat(_bf(Tm[u]), Pb[u]), Pb[u]) for u in nu]
        Tm = [Tm[u] + st[u][:C2] for u in nu]
        Pw = [st[u][C2:] for u in nu]
    Tm = [Tm[u] + _dot(_bf(Tm[u]), _bf(Pw[u])) for u in nu]
    wrow = [wc_ref[ci * 8:ci * 8 + 1, lanes(j)] for ci, j in units]
    BwT = [_bf((Bt[u] * wrow[u]).T) for u in nu]
    KwT = [_bf((Kt[u] * wrow[u]).T) for u in nu]
    sv = [_dot(rcat(Aak[u], Ark[u], KwT[u]), Vb[u]) for u in nu]
    PPb = [_bf(_dot(_bf(Tm[u]), _bf(jnp.concatenate([At[u], sv[u][:C2]], axis=1)))) for u in nu]
    sp = [_dot(rcat(Arb[u], BwT[u]), PPb[u]) for u in nu]
    Q1M1, Q2, M2 = [], [], []
    for u in nu:
        q1s = Rt[u] + sp[u][:C2, :LANES]
        q2s = sp[u][:C2, LANES:] + sv[u][C2:2 * C2]
        m1 = eye_k * wrow[u] + sp[u][C2:, :LANES]
        Q1M1.append(_bf(rcat(q1s[:chunk] + q1s[chunk:], m1)))
        Q2.append(q2s[:chunk] + q2s[chunk:])
        M2.append(sp[u][C2:, LANES:] + sv[u][2 * C2:])

    H = {(si, j): h_ref[si, j] for si in range(nsub) for j in range(npair)}
    Y = []
    for u, (ci, j) in enumerate(units):
        key = (ci // group, j)
        sh = _dot(Q1M1[u], _bf(H[key]))
        Y.append(sh[:chunk] + Q2[u])
        H[key] = sh[chunk:] + M2[u]
    for (si, j), val in H.items():
        h_ref[si, j] = val

    def headmean(t):
        hi, lo = _split2(t)
        m = _dot(rcat(hi, lo), eavg)
        return m[:chunk] + m[chunk:]

    mu = [headmean(Y[u]) for u in nu]
    dv = [Y[u] - mu[u] for u in nu]
    var = [headmean(dv[u] * dv[u]) for u in nu]
    for u, (ci, j) in enumerate(units):
        yn = dv[u] * lax.rsqrt(var[u] + LNX_EPS) * lg_ref[:, lanes(j)] + lb_ref[:, lanes(j)]
        y_ref[rows(ci), lanes(j)] = (yn + bonus_ref[rows(ci), lanes(j)]) * g_ref[rows(ci), lanes(j)]

    @pl.when(c == nsteps - 1)
    def _():
        hout_ref[...] = h_ref[...]


def _wkv(rt, kt, bt, at, v, g, bonus, wc, h0, w, *, nstreams, ncs, chunk):
    total = rt.shape[0]
    group = min(WKV_UNITS, ncs)
    nsub = min(WKV_UNITS // group, nstreams)
    nsteps = ncs // group
    blk = pl.BlockSpec((nsub * group * chunk, D_A), lambda s, c: (s * nsteps + c, 0))
    hspec = pl.BlockSpec((nsub, A_HEADS // 2, LANES, LANES), lambda s, c: (s, 0, 0, 0))
    return pl.pallas_call(
        functools.partial(_wkv_kernel, chunk=chunk, group=group, nsub=nsub, nsteps=nsteps),
        out_shape=[jax.ShapeDtypeStruct((total, D_A), F32),
                   jax.ShapeDtypeStruct((nstreams, A_HEADS // 2, LANES, LANES), F32)],
        grid=(nstreams // nsub, nsteps),
        in_specs=[blk] * 7 + [pl.BlockSpec((nsub * group * 8, D_A), lambda s, c: (s * nsteps + c, 0)),
                              _full((1, D_A)), _full((1, D_A)), _full((LANES, LANES)), hspec],
        out_specs=[blk, hspec],
        scratch_shapes=[pltpu.VMEM((nsub, A_HEADS // 2, LANES, LANES), F32)],
        compiler_params=pltpu.CompilerParams(
            dimension_semantics=("arbitrary", "arbitrary"), vmem_limit_bytes=VMEM_LIMIT),
        name="wkv",
    )(rt, kt, bt, at, v, g, bonus, wc, w["lnx_g"], w["lnx_b"], w["eavg"], h0)


def _layer_norm(t, g, b):
    mu = jnp.mean(t, axis=-1, keepdims=True)
    d = t - mu
    var = jnp.mean(d * d, axis=-1, keepdims=True)
    return d * lax.rsqrt(var + LN_EPS) * g + b


def _tail_kernel(x_ref, ya_ref, yb_ref, wg_ref, bg_ref, wpa_ref, wpb_ref, wo_ref, l1g_ref, l1b_ref,
                 wgu_ref, wdown_ref, l2g_ref, l2b_ref, o_ref):
    x = x_ref[...]
    gates = _sigmoid(_dot(_bf(x), wg_ref[...]) + bg_ref[...])
    m = (gates[:, :D_MODEL] * _dot(_bf(ya_ref[...]), wpa_ref[...])
         + gates[:, D_MODEL:] * _dot(_bf(yb_ref[...]), wpb_ref[...]))
    h = _layer_norm(DN_ALPHA * x + _dot(_bf(m), wo_ref[...]), l1g_ref[...], l1b_ref[...])
    hb = _bf(h)

    def gate_up(c):
        cols = slice(c * FF_CHUNK, (c + 1) * FF_CHUNK)
        ucols = slice(D_FF + c * FF_CHUNK, D_FF + (c + 1) * FF_CHUNK)
        return _dot(hb, wgu_ref[:, cols]), _dot(hb, wgu_ref[:, ucols])

    f = None
    nxt = gate_up(0)
    for c in range(N_FF):
        gate, up = nxt
        if c + 1 < N_FF:
            nxt = gate_up(c + 1)
        d = _dot(_bf(gate * _sigmoid(gate) * up), wdown_ref[c * FF_CHUNK:(c + 1) * FF_CHUNK, :])
        f = d if f is None else f + d
    o_ref[...] = _layer_norm(DN_ALPHA * h + f, l2g_ref[...], l2b_ref[...])


def _tail(x2, ya, yb, w, *, rows):
    total = x2.shape[0]
    rowblk = lambda n: pl.BlockSpec((rows, n), lambda i: (i, 0))

    def const(shape):
        n = len(shape)
        return pl.BlockSpec(shape, lambda i: (0,) * n, pipeline_mode=pl.Buffered(1))

    return pl.pallas_call(
        _tail_kernel,
        out_shape=jax.ShapeDtypeStruct((total, D_MODEL), F32),
        grid=(total // rows,),
        in_specs=[rowblk(D_MODEL), rowblk(D_A), rowblk(D_B),
                  const((D_MODEL, 2 * D_MODEL)), const((1, 2 * D_MODEL)),
                  const((D_A, D_MODEL)), const((D_B, D_MODEL)), const((D_MODEL, D_MODEL)),
                  const((1, D_MODEL)), const((1, D_MODEL)),
                  const((D_MODEL, 2 * D_FF)), const((D_FF, D_MODEL)),
                  const((1, D_MODEL)), const((1, D_MODEL))],
        out_specs=rowblk(D_MODEL),
        compiler_params=pltpu.CompilerParams(
            dimension_semantics=("arbitrary",), vmem_limit_bytes=VMEM_LIMIT),
        name="tail",
    )(x2, ya, yb, w["wg"], w["bg"], w["wpa"], w["wpb"], w["wo"], w["l1g"], w["l1b"],
      w["wgu"], w["wdown"], w["l2g"], w["l2b"])


def _prep_weights(w_in, mu_shift, w0, w_w2, a0, w_a2, w_g2, k_k, k_a, r_k, lnx_g, lnx_b, w_pa,
                  q_norm_g, w_uq, kv_norm_g, w_ukv, w_pb, b_gate, w_o, ln1_g, ln1_b, w_gu, w_down,
                  ln2_g, ln2_b):
    row = lambda t: t.reshape(1, -1).astype(F32)
    nb = A_COLS + B_COLS
    pe = w_in[:, nb - ROPE_DIM:nb]
    half = ROPE_DIM // 2
    pe_sw = jnp.concatenate([pe[:, half:], pe[:, :half]], axis=1)
    wall = jnp.concatenate([w_in[:, :nb - ROPE_DIM], pe, pe_sw,
                            jnp.zeros((D_MODEL, LANES - 2 * ROPE_DIM), F32)], axis=1)
    uq = w_uq.reshape(Q_LORA, B_HEADS, NOPE_DIM + ROPE_DIM)
    nope, r1, r2 = uq[..., :NOPE_DIM], uq[..., NOPE_DIM:NOPE_DIM + half], uq[..., NOPE_DIM + half:]
    zpad = jnp.zeros((Q_LORA, B_HEADS, QK_PAD - NOPE_DIM - ROPE_DIM), F32)
    wqa = jnp.concatenate([nope, r1, r2, zpad], axis=-1).reshape(Q_LORA, B_HEADS * QK_PAD)
    wqb = jnp.concatenate([jnp.zeros_like(nope), r2, r1, zpad], axis=-1).reshape(Q_LORA, B_HEADS * QK_PAD)
    ukv = w_ukv.reshape(KV_LORA, B_HEADS, NOPE_DIM + V_DIM)
    wk = jnp.concatenate([ukv[..., :NOPE_DIM], jnp.zeros((KV_LORA, B_HEADS, QK_PAD - NOPE_DIM), F32)],
                         axis=-1).reshape(KV_LORA, B_HEADS * QK_PAD)
    wv = ukv[..., NOPE_DIM:].reshape(KV_LORA, D_B)
    place = np.zeros((ROPE_DIM, B_HEADS * QK_PAD), np.float32)
    for h in range(B_HEADS):
        place[np.arange(ROPE_DIM), h * QK_PAD + NOPE_DIM + np.arange(ROPE_DIM)] = 1.0
    hid = np.arange(D_A) // A_HEAD_DIM
    esum = (hid[:, None] == hid[None, :]).astype(np.float32)
    lid = np.arange(LANES) // A_HEAD_DIM
    eavg = (lid[:, None] == lid[None, :]).astype(np.float32) / A_HEAD_DIM
    zl = jnp.zeros((LANES - DECAY_LORA, D_A), F32)
    return {
        "wall": _bf(wall), "mu": row(mu_shift), "w0": row(w0), "a0": row(a0), "k_k": row(k_k),
        "k_a": row(k_a), "r_k": row(r_k),
        "ww2": _bf(jnp.concatenate([w_w2, zl], axis=0)), "wa2": _bf(jnp.concatenate([zl, w_a2], axis=0)),
        "wg2": _bf(w_g2), "esum": jnp.asarray(esum, BF16), "eavg": jnp.asarray(eavg, BF16),
        "qg": row(q_norm_g), "kvg": row(kv_norm_g), "wqa": _bf(wqa), "wqb": _bf(wqb),
        "wk": _bf(wk), "wv": _bf(wv), "place": jnp.asarray(place, BF16),
        "lnx_g": row(lnx_g), "lnx_b": row(lnx_b),
        "wg": _bf(w_in[:, nb:]), "bg": row(b_gate), "wpa": _bf(w_pa), "wpb": _bf(w_pb), "wo": _bf(w_o),
        "l1g": row(ln1_g), "l1b": row(ln1_b), "l2g": row(ln2_g), "l2b": row(ln2_b),
        "wgu": _bf(w_gu), "wdown": _bf(w_down),
    }


def _rope_table(pos, reps):
    half = ROPE_DIM // 2
    inv = ROPE_BASE ** (-jnp.arange(half, dtype=F32) / half)
    ang = pos.astype(F32)[:, None] * jnp.tile(inv, LANES // half)[None, :]
    grp = np.arange(LANES) // half
    sc = np.where(grp >= 4, SCORE_SCALE, 1.0)
    mc = jnp.asarray(np.where(grp % 4 < 2, sc, 0.0), F32)
    ms = jnp.asarray(np.where(grp % 4 == 2, -sc, np.where(grp % 4 == 3, sc, 0.0)), F32)
    return jnp.tile(jnp.cos(ang) * mc + jnp.sin(ang) * ms, (reps, 1))


def _state_to_pairs(s):
    nb = s.shape[0]
    st = jnp.swapaxes(s, -1, -2).reshape(nb, A_HEADS // 2, 2, A_HEAD_DIM, A_HEAD_DIM)
    z = jnp.zeros_like(st[:, :, 0])
    top = jnp.concatenate([st[:, :, 0], z], axis=-1)
    bot = jnp.concatenate([z, st[:, :, 1]], axis=-1)
    return jnp.concatenate([top, bot], axis=-2)


def _pairs_to_state(hp):
    nb = hp.shape[0]
    diag = jnp.stack([hp[:, :, :A_HEAD_DIM, :A_HEAD_DIM], hp[:, :, A_HEAD_DIM:, A_HEAD_DIM:]], axis=2)
    return jnp.swapaxes(diag.reshape(nb, A_HEADS, A_HEAD_DIM, A_HEAD_DIM), -1, -2)


def _layer(x, pos0, shift0, wkv0, cache, w):
    nstreams, t, _ = x.shape
    total = nstreams * t
    x2 = x.reshape(total, D_MODEL)
    chunk = min(CHUNK, t)
    rope = _rope_table(pos0 + jnp.arange(t), nstreams)
    (rt, kt, bt, at, v, g, bonus, wc, shift, q, ckv, kpe) = _proj(
        x2, shift0, w, rope, nstreams=nstreams, seg=t, rows=min(PROJ_ROWS, total), chunk=chunk)

    ya, hout = _wkv(rt, kt, bt, at, v, g, bonus, wc, _state_to_pairs(wkv0), w,
                    nstreams=nstreams, ncs=t // chunk, chunk=chunk)

    if cache is None:
        kk, vv = _kvexp(ckv, kpe, w, rows=min(512, total))
        yb = _attn(q.reshape(nstreams, t, -1), kk.reshape(nstreams, t, -1), vv.reshape(nstreams, t, -1),
                   blk=min(ATTN_BLOCK, t))
    else:
        yb = _attnc(q.reshape(nstreams, t, -1), cache[0], cache[1], ckv, kpe, w)

    y = _tail(x2, ya, yb.reshape(total, D_B), w, rows=min(TAIL_ROWS, total))
    return (y.reshape(nstreams, t, D_MODEL), ckv.reshape(nstreams, t, KV_LORA),
            kpe.reshape(nstreams, t, ROPE_DIM), _pairs_to_state(hout), shift)


def kernel(x_prompt, x_sample, cache_ckv, cache_kpe, state_wkv, state_shift, w_in, mu_shift, w0, w_w2, a0,
           w_a2, w_g2, k_k, k_a, r_k, lnx_g, lnx_b, w_pa, q_norm_g, w_uq, kv_norm_g, w_ukv, w_pb, b_gate,
           w_o, ln1_g, ln1_b, w_gu, w_down, ln2_g, ln2_b):
    w = _prep_weights(w_in, mu_shift, w0, w_w2, a0, w_a2, w_g2, k_k, k_a, r_k, lnx_g, lnx_b, w_pa,
                      q_norm_g, w_uq, kv_norm_g, w_ukv, w_pb, b_gate, w_o, ln1_g, ln1_b, w_gu, w_down,
                      ln2_g, ln2_b)
    bp = x_prompt.shape[0]
    y_p, ckv_p, kpe_p, wkv_p, shift_p = _layer(
        x_prompt, 0, jnp.zeros((bp, 1, A_COLS), F32),
        jnp.zeros((bp, A_HEADS, A_HEAD_DIM, A_HEAD_DIM), F32), None, w)
    y_s, ckv_s, kpe_s, wkv_s, shift_s = _layer(
        x_sample, cache_ckv.shape[1], state_shift, state_wkv, (cache_ckv, cache_kpe), w)
    return (y_p, y_s, ckv_p, kpe_p, wkv_p, shift_p, ckv_s, kpe_s, wkv_s, shift_s)
```

```python
import functools

import numpy as np
import jax
import jax.numpy as jnp
from jax import lax
from jax.experimental import pallas as pl
from jax.experimental.pallas import tpu as pltpu

D_MODEL = 1024
CHUNK = 64
A_HEADS = 8
A_HEAD_DIM = 64
D_A = 512
DECAY_LORA = 64
AAA_LORA = 64
GATE_LORA = 128
A_COLS = 3 * D_A + DECAY_LORA + AAA_LORA + GATE_LORA
LNX_EPS = A_HEAD_DIM * 1e-5
B_HEADS = 8
Q_LORA = 256
KV_LORA = 128
NOPE_DIM = 64
ROPE_DIM = 32
V_DIM = 64
D_B = 512
B_COLS = Q_LORA + KV_LORA + ROPE_DIM
ROPE_BASE = 10000.0
ATTN_SCALE = (NOPE_DIM + ROPE_DIM) ** -0.5
SCORE_SCALE = ATTN_SCALE * float(np.log2(np.e))
RMS_EPS = 1e-6
D_FF = 2816
LN_EPS = 1e-5
DN_ALPHA = 2.0 ** 0.25

LANES = 128
QK_PAD = 128
PROJ_COLS = A_COLS + Q_LORA + KV_LORA + LANES
FF_CHUNK = 256
N_FF = D_FF // FF_CHUNK
TRI_ROWS = 256
PROJ_ROWS = 512
TAIL_ROWS = 512
ATTN_BLOCK = 1024
WKV_UNITS = 4
VMEM_LIMIT = 56 * 1024 * 1024

F32 = jnp.float32
BF16 = jnp.bfloat16


def _dot(a, b):
    return jnp.dot(a, b, preferred_element_type=F32)


def _dot_nt(a, b):
    return lax.dot_general(a, b, (((1,), (1,)), ((), ())), preferred_element_type=F32)


def _bf(x):
    return x.astype(BF16)


def _split2(x):
    hi = x.astype(BF16)
    lo = (x - hi.astype(F32)).astype(BF16)
    return hi, lo


def _split3(x):
    h1 = x.astype(BF16)
    r1 = x - h1.astype(F32)
    h2 = r1.astype(BF16)
    h3 = (r1 - h2.astype(F32)).astype(BF16)
    return h1, h2, h3


def _sigmoid(z):
    return 1.0 / (1.0 + jnp.exp(-z))


def _full(shape):
    n = len(shape)
    return pl.BlockSpec(shape, lambda *_: (0,) * n)


def _proj_kernel(x_ref, shift0_ref, wall_ref, mu_ref, w0_ref, a0_ref, kk_ref, ka_ref, rk_ref,
                 ww2_ref, wa2_ref, wg2_ref, esum_ref, ltri_ref, qg_ref, kvg_ref, wqa_ref, wqb_ref,
                 rope_ref, wk_ref, place_ref, wv_ref,
                 rt_ref, kt_ref, bt_ref, at_ref, v_ref, g_ref, bonus_ref, wc_ref, shift_ref,
                 q_ref, ckv_ref, kpe_ref, *rest, rows, chunk, seg):
    *kv_out, carry_ref = rest
    b = pl.program_id(1)
    xb = _bf(x_ref[...])

    pa = _dot(xb, wall_ref[:, :A_COLS])
    pq = _dot(xb, wall_ref[:, A_COLS:A_COLS + Q_LORA])
    pkv = _dot(xb, wall_ref[:, A_COLS + Q_LORA:A_COLS + Q_LORA + KV_LORA])
    ppe = _dot(xb, wall_ref[:, A_COLS + Q_LORA + KV_LORA:])

    cqn = _bf(pq * lax.rsqrt(jnp.mean(pq * pq, axis=-1, keepdims=True) + RMS_EPS) * qg_ref[...])
    qa = _dot(cqn, wqa_ref[...])
    qb = _dot(cqn, wqb_ref[...])
    rope = rope_ref[...]
    lane = lax.broadcasted_iota(jnp.int32, (1, QK_PAD), 1)
    cq = jnp.where(lane < NOPE_DIM, np.float32(SCORE_SCALE), rope)
    sq = pltpu.roll(rope, QK_PAD - ROPE_DIM, axis=1)
    for h in range(B_HEADS):
        sl = slice(h * QK_PAD, (h + 1) * QK_PAD)
        q_ref[:, sl] = _bf(qa[:, sl] * cq + qb[:, sl] * sq)
    ckv = pkv * lax.rsqrt(jnp.mean(pkv * pkv, axis=-1, keepdims=True) + RMS_EPS) * kvg_ref[...]
    ckv_ref[...] = ckv
    ppe = ppe * rope
    kpe = ppe[:, :ROPE_DIM] + ppe[:, ROPE_DIM:2 * ROPE_DIM]
    kpe_ref[...] = kpe
    if kv_out:
        kx_ref, vx_ref = kv_out
        cb = _bf(ckv)
        kx_ref[...] = _bf(_dot(cb, wk_ref[...]) + _dot(_bf(kpe), place_ref[...]))
        vx_ref[...] = _bf(_dot(cb, wv_ref[...]))

    row = lax.broadcasted_iota(jnp.int32, (rows, 1), 0)
    if seg >= rows:
        first = jnp.where(b == 0, shift0_ref[0], carry_ref[...])
        starts = row == 0
        last = pa[rows - 1:rows, :]
        carry_ref[...] = last
        shift_ref[0] = last
    else:
        first = jnp.broadcast_to(shift0_ref[...], (rows // seg, seg, A_COLS)).reshape(rows, A_COLS)
        starts = row % seg == 0
        for s in range(rows // seg):
            shift_ref[s] = pa[(s + 1) * seg - 1:(s + 1) * seg, :]
    prev = jnp.where(starts, first, pltpu.roll(pa, 1, axis=0))
    xs = pa + (prev - pa) * mu_ref[...]

    r = xs[:, :D_A]
    k = xs[:, D_A:2 * D_A]
    v = xs[:, 2 * D_A:3 * D_A]
    wa = xs[:, 3 * D_A:3 * D_A + LANES]
    gd = xs[:, 3 * D_A + LANES:]

    z = w0_ref[...] + _dot(_bf(jnp.tanh(wa)), ww2_ref[...])
    ld = -np.float32(np.exp(-0.5)) * _sigmoid(z)
    a = _sigmoid(a0_ref[...] + _dot(_bf(wa), wa2_ref[...]))
    g_ref[...] = _dot(_bf(_sigmoid(gd)), wg2_ref[...])

    kkr = k * kk_ref[...]
    kh = k * (1.0 + (a - 1.0) * ka_ref[...])
    hi, lo = _split2(kkr * kkr)
    hs = _dot(jnp.concatenate([hi, lo, _bf(r * kh * rk_ref[...])], axis=0), esum_ref[...])
    kk = kkr / jnp.maximum(jnp.sqrt(hs[:rows] + hs[rows:2 * rows]), 1e-12)
    bonus_ref[...] = hs[2 * rows:] * v
    v_ref[...] = v

    ltri = ltri_ref[...]
    tri = ltri.shape[0]
    h1, h2, h3 = _split3(ld)
    cum = jnp.concatenate(
        [_dot(ltri, h1[i:i + tri]) + _dot(ltri, h2[i:i + tri]) + _dot(ltri, h3[i:i + tri])
         for i in range(0, rows, tri)], axis=0)
    ep = jnp.exp(cum)
    em = jnp.exp(-cum)
    rt_ref[...] = r * ep
    kt_ref[...] = kh * em
    bt_ref[...] = (kk * a) * em
    at_ref[...] = -kk * jnp.exp(cum - ld)
    for c in range(rows // chunk):
        wc_ref[c * 8:(c + 1) * 8, :] = jnp.broadcast_to(ep[(c + 1) * chunk - 1:(c + 1) * chunk, :], (8, D_A))


def _proj(x2, shift0, w, rope, *, nstreams, seg, rows, chunk, expand_kv):
    total = nstreams * seg
    spb = max(1, rows // seg)
    bps = max(1, seg // rows)
    nck = rows // chunk
    tri = min(rows, TRI_ROWS)
    ltri = _chunk_tri(tri, chunk)
    rowblk = lambda n: pl.BlockSpec((rows, n), lambda s, b: (s * bps + b, 0))
    in_specs = [
        rowblk(D_MODEL),
        pl.BlockSpec((spb, 1, A_COLS), lambda s, b: (s, 0, 0)),
        _full((D_MODEL, PROJ_COLS)),
        _full((1, A_COLS)), _full((1, D_A)), _full((1, D_A)), _full((1, D_A)), _full((1, D_A)), _full((1, D_A)),
        _full((LANES, D_A)), _full((LANES, D_A)), _full((GATE_LORA, D_A)),
        _full((D_A, D_A)), _full((tri, tri)),
        _full((1, Q_LORA)), _full((1, KV_LORA)),
        _full((Q_LORA, B_HEADS * QK_PAD)), _full((Q_LORA, B_HEADS * QK_PAD)),
        rowblk(LANES),
        _full((KV_LORA, B_HEADS * QK_PAD)), _full((ROPE_DIM, B_HEADS * QK_PAD)), _full((KV_LORA, D_B)),
    ]
    f32o = lambda n: jax.ShapeDtypeStruct((total, n), F32)
    out_shape = [f32o(D_A)] * 7 + [
        jax.ShapeDtypeStruct((total // chunk * 8, D_A), F32),
        jax.ShapeDtypeStruct((nstreams, 1, A_COLS), F32),
        jax.ShapeDtypeStruct((total, B_HEADS * QK_PAD), BF16),
        f32o(KV_LORA), f32o(ROPE_DIM),
    ]
    out_specs = [rowblk(D_A)] * 7 + [
        pl.BlockSpec((nck * 8, D_A), lambda s, b: (s * bps + b, 0)),
        pl.BlockSpec((spb, 1, A_COLS), lambda s, b: (s, 0, 0)),
        rowblk(B_HEADS * QK_PAD), rowblk(KV_LORA), rowblk(ROPE_DIM),
    ]
    if expand_kv:
        out_shape += [jax.ShapeDtypeStruct((total, B_HEADS * QK_PAD), BF16),
                      jax.ShapeDtypeStruct((total, D_B), BF16)]
        out_specs += [rowblk(B_HEADS * QK_PAD), rowblk(D_B)]
    return pl.pallas_call(
        functools.partial(_proj_kernel, rows=rows, chunk=chunk, seg=seg),
        out_shape=out_shape,
        grid=(nstreams // spb, bps),
        in_specs=in_specs,
        out_specs=out_specs,
        scratch_shapes=[pltpu.VMEM((1, A_COLS), F32)],
        compiler_params=pltpu.CompilerParams(
            dimension_semantics=("arbitrary", "arbitrary"), vmem_limit_bytes=VMEM_LIMIT),
        name="proj",
    )(x2, shift0, w["wall"], w["mu"], w["w0"], w["a0"], w["k_k"], w["k_a"], w["r_k"],
      w["ww2"], w["wa2"], w["wg2"], w["esum"], ltri, w["qg"], w["kvg"], w["wqa"], w["wqb"],
      rope, w["wk"], w["place"], w["wv"])


def _chunk_tri(rows, chunk):
    i = np.arange(rows)
    m = (i[:, None] // chunk == i[None, :] // chunk) & (i[None, :] <= i[:, None])
    return jnp.asarray(m, BF16)


def _attn_kernel(qi_ref, ki_ref, q_ref, k_ref, v_ref, o_ref, m_ref, acc_ref, *, blk):
    s_id = pl.program_id(1)
    qi = qi_ref[s_id]
    ki = ki_ref[s_id]

    @pl.when(ki == 0)
    def _():
        m_ref[...] = jnp.full(m_ref.shape, -jnp.inf, F32)
        acc_ref[...] = jnp.zeros(acc_ref.shape, F32)

    low = lax.broadcasted_iota(jnp.int32, (1, LANES), 1) < V_DIM

    def step(r0, nr, nc, masked):
        rs = slice(r0, r0 + nr)

        def scores(h):
            sl = slice(h * QK_PAD, (h + 1) * QK_PAD)
            return _dot_nt(q_ref[0, rs, sl], k_ref[0, :nc, sl])

        if masked:
            rq = (lax.broadcasted_iota(jnp.int32, (nr, nc), 0) + r0) // CHUNK
            mask = lax.broadcasted_iota(jnp.int32, (nr, nc), 1) // CHUNK <= rq
        one = jnp.ones((), BF16)
        ahead = 2
        pending = [scores(h) for h in range(ahead)]
        for h in range(B_HEADS):
            s = pending.pop(0)
            if h + ahead < B_HEADS:
                pending.append(scores(h + ahead))
            vp = v_ref[0, :nc, (h // 2) * LANES:(h // 2 + 1) * LANES]
            vext = jnp.where(low, vp, one) if h % 2 == 0 else jnp.where(low, one, vp)
            if masked:
                s = jnp.where(mask, s, -jnp.inf)
            m_prev = m_ref[h, rs]
            m_new = jnp.maximum(m_prev, jnp.max(s, axis=-1, keepdims=True))
            p = jnp.exp2(s - m_new[:, :1])
            acc_ref[h, rs] = jnp.exp2(m_prev - m_new) * acc_ref[h, rs] + _dot(_bf(p), vext)
            m_ref[h, rs] = m_new

    @pl.when(ki < qi)
    def _():
        step(0, blk, blk, False)

    @pl.when(ki == qi)
    def _():
        half = blk // 2
        step(0, half, half, True)
        step(half, half, blk, True)
        for j in range(B_HEADS // 2):
            a0 = acc_ref[2 * j]
            a1 = acc_ref[2 * j + 1]
            num = jnp.where(low, a0, a1)
            den = jnp.where(low, pltpu.roll(a0, V_DIM, axis=1), pltpu.roll(a1, V_DIM, axis=1))
            o_ref[0, :, j * LANES:(j + 1) * LANES] = num / den


def _attn(q, k, v, *, blk):
    nb, t, _ = q.shape
    steps = [(i, j) for i in range(t // blk) for j in range(i + 1)]
    qi = jnp.asarray([s[0] for s in steps], jnp.int32)
    ki = jnp.asarray([s[1] for s in steps], jnp.int32)
    grid_spec = pltpu.PrefetchScalarGridSpec(
        num_scalar_prefetch=2,
        grid=(nb, len(steps)),
        in_specs=[pl.BlockSpec((1, blk, B_HEADS * QK_PAD), lambda b, s, qi, ki: (b, qi[s], 0)),
                  pl.BlockSpec((1, blk, B_HEADS * QK_PAD), lambda b, s, qi, ki: (b, ki[s], 0)),
                  pl.BlockSpec((1, blk, D_B), lambda b, s, qi, ki: (b, ki[s], 0))],
        out_specs=pl.BlockSpec((1, blk, D_B), lambda b, s, qi, ki: (b, qi[s], 0)),
        scratch_shapes=[pltpu.VMEM((B_HEADS, blk, LANES), F32),
                        pltpu.VMEM((B_HEADS, blk, LANES), F32)])
    return pl.pallas_call(
        functools.partial(_attn_kernel, blk=blk),
        out_shape=jax.ShapeDtypeStruct((nb, t, D_B), F32),
        grid_spec=grid_spec,
        compiler_params=pltpu.CompilerParams(
            dimension_semantics=("arbitrary", "arbitrary"), vmem_limit_bytes=VMEM_LIMIT),
        name="attn",
    )(qi, ki, q, k, v)


def _attnc_kernel(q_ref, cckv_ref, ckpe_ref, nckv_ref, nkpe_ref, wk_ref, wv_ref, place_ref, o_ref, *, t):
    q = q_ref[0]
    heads = [q[:, h * QK_PAD:(h + 1) * QK_PAD] for h in range(B_HEADS)]
    qf = jnp.concatenate(heads, axis=0)
    qa = jnp.concatenate([_bf(_dot_nt(heads[h], wk_ref[:, h * QK_PAD:(h + 1) * QK_PAD]))
                          for h in range(B_HEADS)], axis=0)
    place = place_ref[:, :QK_PAD]

    def scores(ckv, kpe):
        cb = _bf(ckv)
        return _dot_nt(qa, cb) + _dot_nt(qf, _bf(_dot(_bf(kpe), place))), cb

    s_c, cb_c = scores(cckv_ref[0], ckpe_ref[0])
    s_n, cb_n = scores(nckv_ref[...], nkpe_ref[...])
    m = jnp.maximum(jnp.max(s_c, axis=-1, keepdims=True), jnp.max(s_n, axis=-1, keepdims=True))
    p_c = jnp.exp2(s_c - m)
    p_n = jnp.exp2(s_n - m)
    den = jnp.sum(p_c, axis=-1, keepdims=True) + jnp.sum(p_n, axis=-1, keepdims=True)
    lat = _bf((_dot(_bf(p_c), cb_c) + _dot(_bf(p_n), cb_n)) / den)
    low = lax.broadcasted_iota(jnp.int32, (1, LANES), 1) < V_DIM
    zero = jnp.zeros((), BF16)
    for j in range(B_HEADS // 2):
        wvp = wv_ref[:, j * LANES:(j + 1) * LANES]
        o_ref[0, :, j * LANES:(j + 1) * LANES] = (
            _dot(lat[2 * j * t:(2 * j + 1) * t], jnp.where(low, wvp, zero))
            + _dot(lat[(2 * j + 1) * t:(2 * j + 2) * t], jnp.where(low, zero, wvp)))


def _attnc(q, cache_ckv, cache_kpe, ckv, kpe, w):
    nb, t, _ = q.shape
    past = cache_ckv.shape[1]
    return pl.pallas_call(
        functools.partial(_attnc_kernel, t=t),
        out_shape=jax.ShapeDtypeStruct((nb, t, D_B), F32),
        grid=(nb,),
        in_specs=[pl.BlockSpec((1, t, B_HEADS * QK_PAD), lambda b: (b, 0, 0)),
                  pl.BlockSpec((1, past, KV_LORA), lambda b: (b, 0, 0)),
                  pl.BlockSpec((1, past, ROPE_DIM), lambda b: (b, 0, 0)),
                  pl.BlockSpec((t, KV_LORA), lambda b: (b, 0)),
                  pl.BlockSpec((t, ROPE_DIM), lambda b: (b, 0)),
                  _full((KV_LORA, B_HEADS * QK_PAD)), _full((KV_LORA, D_B)),
                  _full((ROPE_DIM, B_HEADS * QK_PAD))],
        out_specs=pl.BlockSpec((1, t, D_B), lambda b: (b, 0, 0)),
        compiler_params=pltpu.CompilerParams(dimension_semantics=("arbitrary",)),
        name="attnc",
    )(q, cache_ckv, cache_kpe, ckv, kpe, w["wk"], w["wv"], w["place"])


def _wkv_kernel(rt_ref, kt_ref, bt_ref, at_ref, v_ref, g_ref, bonus_ref, wc_ref, lg_ref, lb_ref,
                eavg_ref, h0_ref, y_ref, hout_ref, h_ref, *, chunk, group, nsub, nsteps):
    c = pl.program_id(1)
    C2 = 2 * chunk
    npair = A_HEADS // 2

    @pl.when(c == 0)
    def _():
        h_ref[...] = h0_ref[...]

    low = lax.broadcasted_iota(jnp.int32, (chunk, LANES), 1) < A_HEAD_DIM
    ii = lax.broadcasted_iota(jnp.int32, (C2, C2), 0)
    jj = lax.broadcasted_iota(jnp.int32, (C2, C2), 1)
    strict = ii > jj
    incl = ii >= jj
    eye_c = (ii == jj).astype(F32)
    ki = lax.broadcasted_iota(jnp.int32, (LANES, LANES), 0)
    kj = lax.broadcasted_iota(jnp.int32, (LANES, LANES), 1)
    eye_k = (ki == kj).astype(F32)
    eavg = eavg_ref[...]

    def stack(t):
        return jnp.concatenate([jnp.where(low, t, 0.0), jnp.where(low, 0.0, t)], axis=0)

    units = [(ci, j) for ci in range(nsub * group) for j in range(npair)]
    rows = lambda ci: slice(ci * chunk, (ci + 1) * chunk)
    lanes = lambda j: slice(j * LANES, (j + 1) * LANES)
    ld = lambda ref: [stack(ref[rows(ci), lanes(j)]) for ci, j in units]
    At, Bt, Kt, Rt, Vs = ld(at_ref), ld(bt_ref), ld(kt_ref), ld(rt_ref), ld(v_ref)
    nu = range(len(units))
    Vb = [_bf(Vs[u]) for u in nu]
    g1 = [_dot_nt(_bf(jnp.concatenate([At[u], Rt[u]], axis=0)),
                  _bf(jnp.concatenate([Bt[u], Kt[u]], axis=0))) for u in nu]
    Aab = [jnp.where(strict, g1[u][:C2, :C2], 0.0) for u in nu]
    Aak = [_bf(jnp.where(strict, g1[u][:C2, C2:], 0.0)) for u in nu]
    Arb = [_bf(jnp.where(incl, g1[u][C2:, :C2], 0.0)) for u in nu]
    Ark = [_bf(jnp.where(incl, g1[u][C2:, C2:], 0.0)) for u in nu]
    rcat = lambda *t: jnp.concatenate(t, axis=0)
    Tm = [eye_c + Aab[u] for u in nu]
    Nb = [_bf(Aab[u]) for u in nu]
    Pw = [_dot(Nb[u], Nb[u]) for u in nu]
    for _ in range(chunk.bit_length() - 3):
        Pb = [_bf(Pw[u]) for u in nu]
        st = [_dot(rcat(_bf(Tm[u]), Pb[u]), Pb[u]) for u in nu]
        Tm = [Tm[u] + st[u][:C2] for u in nu]
        Pw = [st[u][C2:] for u in nu]
    Tm = [Tm[u] + _dot(_bf(Tm[u]), _bf(Pw[u])) for u in nu]
    wrow = [wc_ref[ci * 8:ci * 8 + 1, lanes(j)] for ci, j in units]
    BwT = [_bf((Bt[u] * wrow[u]).T) for u in nu]
    KwT = [_bf((Kt[u] * wrow[u]).T) for u in nu]
    sv = [_dot(rcat(Aak[u], Ark[u], KwT[u]), Vb[u]) for u in nu]
    PPb = [_bf(_dot(_bf(Tm[u]), _bf(jnp.concatenate([At[u], sv[u][:C2]], axis=1)))) for u in nu]
    sp = [_dot(rcat(Arb[u], BwT[u]), PPb[u]) for u in nu]
    Q1M1, Q2, M2 = [], [], []
    for u in nu:
        q1s = Rt[u] + sp[u][:C2, :LANES]
        q2s = sp[u][:C2, LANES:] + sv[u][C2:2 * C2]
        m1 = eye_k * wrow[u] + sp[u][C2:, :LANES]
        Q1M1.append(_bf(rcat(q1s[:chunk] + q1s[chunk:], m1)))
        Q2.append(q2s[:chunk] + q2s[chunk:])
        M2.append(sp[u][C2:, LANES:] + sv[u][2 * C2:])

    H = {(si, j): h_ref[si, j] for si in range(nsub) for j in range(npair)}
    Y = []
    for u, (ci, j) in enumerate(units):
        key = (ci // group, j)
        sh = _dot(Q1M1[u], _bf(H[key]))
        Y.append(sh[:chunk] + Q2[u])
        H[key] = sh[chunk:] + M2[u]
    for (si, j), val in H.items():
        h_ref[si, j] = val

    def headmean(t):
        hi, lo = _split2(t)
        m = _dot(rcat(hi, lo), eavg)
        return m[:chunk] + m[chunk:]

    mu = [headmean(Y[u]) for u in nu]
    dv = [Y[u] - mu[u] for u in nu]
    var = [headmean(dv[u] * dv[u]) for u in nu]
    for u, (ci, j) in enumerate(units):
        yn = dv[u] * lax.rsqrt(var[u] + LNX_EPS) * lg_ref[:, lanes(j)] + lb_ref[:, lanes(j)]
        y_ref[rows(ci), lanes(j)] = (yn + bonus_ref[rows(ci), lanes(j)]) * g_ref[rows(ci), lanes(j)]

    @pl.when(c == nsteps - 1)
    def _():
        hout_ref[...] = h_ref[...]


def _wkv(rt, kt, bt, at, v, g, bonus, wc, h0, w, *, nstreams, ncs, chunk):
    total = rt.shape[0]
    group = min(WKV_UNITS, ncs)
    nsub = min(WKV_UNITS // group, nstreams)
    nsteps = ncs // group
    blk = pl.BlockSpec((nsub * group * chunk, D_A), lambda s, c: (s * nsteps + c, 0))
    hspec = pl.BlockSpec((nsub, A_HEADS // 2, LANES, LANES), lambda s, c: (s, 0, 0, 0))
    return pl.pallas_call(
        functools.partial(_wkv_kernel, chunk=chunk, group=group, nsub=nsub, nsteps=nsteps),
        out_shape=[jax.ShapeDtypeStruct((total, D_A), F32),
                   jax.ShapeDtypeStruct((nstreams, A_HEADS // 2, LANES, LANES), F32)],
        grid=(nstreams // nsub, nsteps),
        in_specs=[blk] * 7 + [pl.BlockSpec((nsub * group * 8, D_A), lambda s, c: (s * nsteps + c, 0)),
                              _full((1, D_A)), _full((1, D_A)), _full((LANES, LANES)), hspec],
        out_specs=[blk, hspec],
        scratch_shapes=[pltpu.VMEM((nsub, A_HEADS // 2, LANES, LANES), F32)],
        compiler_params=pltpu.CompilerParams(
            dimension_semantics=("arbitrary", "arbitrary"), vmem_limit_bytes=VMEM_LIMIT),
        name="wkv",
    )(rt, kt, bt, at, v, g, bonus, wc, w["lnx_g"], w["lnx_b"], w["eavg"], h0)


def _layer_norm(t, g, b):
    mu = jnp.mean(t, axis=-1, keepdims=True)
    d = t - mu
    var = jnp.mean(d * d, axis=-1, keepdims=True)
    return d * lax.rsqrt(var + LN_EPS) * g + b


def _tail_kernel(x_ref, ya_ref, yb_ref, wg_ref, bg_ref, wpa_ref, wpb_ref, wo_ref, l1g_ref, l1b_ref,
                 wgu_ref, wdown_ref, l2g_ref, l2b_ref, o_ref):
    x = x_ref[...]
    gates = _sigmoid(_dot(_bf(x), wg_ref[...]) + bg_ref[...])
    m = (gates[:, :D_MODEL] * _dot(_bf(ya_ref[...]), wpa_ref[...])
         + gates[:, D_MODEL:] * _dot(_bf(yb_ref[...]), wpb_ref[...]))
    h = _layer_norm(DN_ALPHA * x + _dot(_bf(m), wo_ref[...]), l1g_ref[...], l1b_ref[...])
    hb = _bf(h)

    def gate_up(c):
        cols = slice(c * FF_CHUNK, (c + 1) * FF_CHUNK)
        ucols = slice(D_FF + c * FF_CHUNK, D_FF + (c + 1) * FF_CHUNK)
        return _dot(hb, wgu_ref[:, cols]), _dot(hb, wgu_ref[:, ucols])

    f = None
    nxt = gate_up(0)
    for c in range(N_FF):
        gate, up = nxt
        if c + 1 < N_FF:
            nxt = gate_up(c + 1)
        d = _dot(_bf(gate * _sigmoid(gate) * up), wdown_ref[c * FF_CHUNK:(c + 1) * FF_CHUNK, :])
        f = d if f is None else f + d
    o_ref[...] = _layer_norm(DN_ALPHA * h + f, l2g_ref[...], l2b_ref[...])


def _tail(x2, ya, yb, w, *, rows):
    total = x2.shape[0]
    rowblk = lambda n: pl.BlockSpec((rows, n), lambda i: (i, 0))

    def const(shape):
        n = len(shape)
        return pl.BlockSpec(shape, lambda i: (0,) * n, pipeline_mode=pl.Buffered(1))

    return pl.pallas_call(
        _tail_kernel,
        out_shape=jax.ShapeDtypeStruct((total, D_MODEL), F32),
        grid=(total // rows,),
        in_specs=[rowblk(D_MODEL), rowblk(D_A), rowblk(D_B),
                  const((D_MODEL, 2 * D_MODEL)), const((1, 2 * D_MODEL)),
                  const((D_A, D_MODEL)), const((D_B, D_MODEL)), const((D_MODEL, D_MODEL)),
                  const((1, D_MODEL)), const((1, D_MODEL)),
                  const((D_MODEL, 2 * D_FF)), const((D_FF, D_MODEL)),
                  const((1, D_MODEL)), const((1, D_MODEL))],
        out_specs=rowblk(D_MODEL),
        compiler_params=pltpu.CompilerParams(
            dimension_semantics=("arbitrary",), vmem_limit_bytes=VMEM_LIMIT),
        name="tail",
    )(x2, ya, yb, w["wg"], w["bg"], w["wpa"], w["wpb"], w["wo"], w["l1g"], w["l1b"],
      w["wgu"], w["wdown"], w["l2g"], w["l2b"])


def _prep_weights(w_in, mu_shift, w0, w_w2, a0, w_a2, w_g2, k_k, k_a, r_k, lnx_g, lnx_b, w_pa,
                  q_norm_g, w_uq, kv_norm_g, w_ukv, w_pb, b_gate, w_o, ln1_g, ln1_b, w_gu, w_down,
                  ln2_g, ln2_b):
    row = lambda t: t.reshape(1, -1).astype(F32)
    nb = A_COLS + B_COLS
    pe = w_in[:, nb - ROPE_DIM:nb]
    half = ROPE_DIM // 2
    pe_sw = jnp.concatenate([pe[:, half:], pe[:, :half]], axis=1)
    wall = jnp.concatenate([w_in[:, :nb - ROPE_DIM], pe, pe_sw,
                            jnp.zeros((D_MODEL, LANES - 2 * ROPE_DIM), F32)], axis=1)
    uq = w_uq.reshape(Q_LORA, B_HEADS, NOPE_DIM + ROPE_DIM)
    nope, r1, r2 = uq[..., :NOPE_DIM], uq[..., NOPE_DIM:NOPE_DIM + half], uq[..., NOPE_DIM + half:]
    zpad = jnp.zeros((Q_LORA, B_HEADS, QK_PAD - NOPE_DIM - ROPE_DIM), F32)
    wqa = jnp.concatenate([nope, r1, r2, zpad], axis=-1).reshape(Q_LORA, B_HEADS * QK_PAD)
    wqb = jnp.concatenate([jnp.zeros_like(nope), r2, r1, zpad], axis=-1).reshape(Q_LORA, B_HEADS * QK_PAD)
    ukv = w_ukv.reshape(KV_LORA, B_HEADS, NOPE_DIM + V_DIM)
    wk = jnp.concatenate([ukv[..., :NOPE_DIM], jnp.zeros((KV_LORA, B_HEADS, QK_PAD - NOPE_DIM), F32)],
                         axis=-1).reshape(KV_LORA, B_HEADS * QK_PAD)
    wv = ukv[..., NOPE_DIM:].reshape(KV_LORA, D_B)
    place = np.zeros((ROPE_DIM, B_HEADS * QK_PAD), np.float32)
    for h in range(B_HEADS):
        place[np.arange(ROPE_DIM), h * QK_PAD + NOPE_DIM + np.arange(ROPE_DIM)] = 1.0
    hid = np.arange(D_A) // A_HEAD_DIM
    esum = (hid[:, None] == hid[None, :]).astype(np.float32)
    lid = np.arange(LANES) // A_HEAD_DIM
    eavg = (lid[:, None] == lid[None, :]).astype(np.float32) / A_HEAD_DIM
    zl = jnp.zeros((LANES - DECAY_LORA, D_A), F32)
    return {
        "wall": _bf(wall), "mu": row(mu_shift), "w0": row(w0), "a0": row(a0), "k_k": row(k_k),
        "k_a": row(k_a), "r_k": row(r_k),
        "ww2": _bf(jnp.concatenate([w_w2, zl], axis=0)), "wa2": _bf(jnp.concatenate([zl, w_a2], axis=0)),
        "wg2": _bf(w_g2), "esum": jnp.asarray(esum, BF16), "eavg": jnp.asarray(eavg, BF16),
        "qg": row(q_norm_g), "kvg": row(kv_norm_g), "wqa": _bf(wqa), "wqb": _bf(wqb),
        "wk": _bf(wk), "wv": _bf(wv), "place": jnp.asarray(place, BF16),
        "lnx_g": row(lnx_g), "lnx_b": row(lnx_b),
        "wg": _bf(w_in[:, nb:]), "bg": row(b_gate), "wpa": _bf(w_pa), "wpb": _bf(w_pb), "wo": _bf(w_o),
        "l1g": row(ln1_g), "l1b": row(ln1_b), "l2g": row(ln2_g), "l2b": row(ln2_b),
        "wgu": _bf(w_gu), "wdown": _bf(w_down),
    }


def _rope_table(pos0, t, reps):
    half = ROPE_DIM // 2
    per = LANES // half
    inv = ROPE_BASE ** (-jnp.arange(half, dtype=F32) / half)
    lane = jnp.arange(LANES)[None, :]
    pos = (pos0 + jnp.arange(t // per)[:, None] * per + lane // half).astype(F32)
    ang = pos * jnp.tile(inv, per)[None, :]
    c, s = lax.optimization_barrier((jnp.cos(ang), jnp.sin(ang)))
    c = c.reshape(t, half)
    s = s.reshape(t, half)
    sc = np.float32(SCORE_SCALE)
    table = jnp.concatenate([c, c, -s, s, c * sc, c * sc, -s * sc, s * sc], axis=1)
    return jnp.tile(table, (reps, 1))


def _state_to_pairs(s):
    nb = s.shape[0]
    st = jnp.swapaxes(s, -1, -2).reshape(nb, A_HEADS // 2, 2, A_HEAD_DIM, A_HEAD_DIM)
    z = jnp.zeros_like(st[:, :, 0])
    top = jnp.concatenate([st[:, :, 0], z], axis=-1)
    bot = jnp.concatenate([z, st[:, :, 1]], axis=-1)
    return jnp.concatenate([top, bot], axis=-2)


def _pairs_to_state(hp):
    nb = hp.shape[0]
    diag = jnp.stack([hp[:, :, :A_HEAD_DIM, :A_HEAD_DIM], hp[:, :, A_HEAD_DIM:, A_HEAD_DIM:]], axis=2)
    return jnp.swapaxes(diag.reshape(nb, A_HEADS, A_HEAD_DIM, A_HEAD_DIM), -1, -2)


def _layer(x, pos0, shift0, wkv0, cache, w):
    nstreams, t, _ = x.shape
    total = nstreams * t
    x2 = x.reshape(total, D_MODEL)
    chunk = min(CHUNK, t)
    rope = _rope_table(pos0, t, nstreams)
    (rt, kt, bt, at, v, g, bonus, wc, shift, q, ckv, kpe, *kv) = _proj(
        x2, shift0, w, rope, nstreams=nstreams, seg=t, rows=min(PROJ_ROWS, total), chunk=chunk,
        expand_kv=cache is None)

    ya, hout = _wkv(rt, kt, bt, at, v, g, bonus, wc, _state_to_pairs(wkv0), w,
                    nstreams=nstreams, ncs=t // chunk, chunk=chunk)

    if cache is None:
        kk, vv = kv
        yb = _attn(q.reshape(nstreams, t, -1), kk.reshape(nstreams, t, -1), vv.reshape(nstreams, t, -1),
                   blk=min(ATTN_BLOCK, t))
    else:
        yb = _attnc(q.reshape(nstreams, t, -1), cache[0], cache[1], ckv, kpe, w)

    y = _tail(x2, ya, yb.reshape(total, D_B), w, rows=min(TAIL_ROWS, total))
    return (y.reshape(nstreams, t, D_MODEL), ckv.reshape(nstreams, t, KV_LORA),
            kpe.reshape(nstreams, t, ROPE_DIM), _pairs_to_state(hout), shift)


def kernel(x_prompt, x_sample, cache_ckv, cache_kpe, state_wkv, state_shift, w_in, mu_shift, w0, w_w2, a0,
           w_a2, w_g2, k_k, k_a, r_k, lnx_g, lnx_b, w_pa, q_norm_g, w_uq, kv_norm_g, w_ukv, w_pb, b_gate,
           w_o, ln1_g, ln1_b, w_gu, w_down, ln2_g, ln2_b):
    w = _prep_weights(w_in, mu_shift, w0, w_w2, a0, w_a2, w_g2, k_k, k_a, r_k, lnx_g, lnx_b, w_pa,
                      q_norm_g, w_uq, kv_norm_g, w_ukv, w_pb, b_gate, w_o, ln1_g, ln1_b, w_gu, w_down,
                      ln2_g, ln2_b)
    bp = x_prompt.shape[0]
    y_p, ckv_p, kpe_p, wkv_p, shift_p = _layer(
        x_prompt, 0, jnp.zeros((bp, 1, A_COLS), F32),
        jnp.zeros((bp, A_HEADS, A_HEAD_DIM, A_HEAD_DIM), F32), None, w)
    y_s, ckv_s, kpe_s, wkv_s, shift_s = _layer(
        x_sample, cache_ckv.shape[1], state_shift, state_wkv, (cache_ckv, cache_kpe), w)
    return (y_p, y_s, ckv_p, kpe_p, wkv_p, shift_p, ckv_s, kpe_s, wkv_s, shift_s)
```

```python
import functools

import numpy as np
import jax
import jax.numpy as jnp
from jax import lax
from jax.experimental import pallas as pl
from jax.experimental.pallas import tpu as pltpu

D_MODEL = 1024
CHUNK = 64
A_HEADS = 8
A_HEAD_DIM = 64
D_A = 512
DECAY_LORA = 64
AAA_LORA = 64
GATE_LORA = 128
A_COLS = 3 * D_A + DECAY_LORA + AAA_LORA + GATE_LORA
LNX_EPS = A_HEAD_DIM * 1e-5
B_HEADS = 8
Q_LORA = 256
KV_LORA = 128
NOPE_DIM = 64
ROPE_DIM = 32
V_DIM = 64
D_B = 512
B_COLS = Q_LORA + KV_LORA + ROPE_DIM
ROPE_BASE = 10000.0
ATTN_SCALE = (NOPE_DIM + ROPE_DIM) ** -0.5
SCORE_SCALE = ATTN_SCALE * float(np.log2(np.e))
RMS_EPS = 1e-6
D_FF = 2816
LN_EPS = 1e-5
DN_ALPHA = 2.0 ** 0.25

LANES = 128
QK_PAD = 128
PROJ_COLS = A_COLS + Q_LORA + KV_LORA + LANES
FF_CHUNK = 256
N_FF = D_FF // FF_CHUNK
TRI_ROWS = 256
PROJ_ROWS = 512
TAIL_ROWS = 512
ATTN_BLOCK = 1024
WKV_UNITS = 4
VMEM_LIMIT = 56 * 1024 * 1024

F32 = jnp.float32
BF16 = jnp.bfloat16


def _dot(a, b):
    return jnp.dot(a, b, preferred_element_type=F32)


def _dot_nt(a, b):
    return lax.dot_general(a, b, (((1,), (1,)), ((), ())), preferred_element_type=F32)


def _bf(x):
    return x.astype(BF16)


def _split2(x):
    hi = x.astype(BF16)
    lo = (x - hi.astype(F32)).astype(BF16)
    return hi, lo


def _split3(x):
    h1 = x.astype(BF16)
    r1 = x - h1.astype(F32)
    h2 = r1.astype(BF16)
    h3 = (r1 - h2.astype(F32)).astype(BF16)
    return h1, h2, h3


def _sigmoid(z):
    return 1.0 / (1.0 + jnp.exp(-z))


def _full(shape):
    n = len(shape)
    return pl.BlockSpec(shape, lambda *_: (0,) * n)


def _proj_kernel(x_ref, shift0_ref, wall_ref, mu_ref, w0_ref, a0_ref, kk_ref, ka_ref, rk_ref,
                 ww2_ref, wa2_ref, wg2_ref, esum_ref, ltri_ref, qg_ref, kvg_ref, wqa_ref, wqb_ref,
                 rope_ref, wk_ref, place_ref, wv_ref,
                 rt_ref, kt_ref, bt_ref, at_ref, v_ref, g_ref, bonus_ref, wc_ref, shift_ref,
                 q_ref, ckv_ref, kpe_ref, *rest, rows, chunk, seg):
    *kv_out, carry_ref = rest
    b = pl.program_id(1)
    xb = _bf(x_ref[...])

    pa = _dot(xb, wall_ref[:, :A_COLS])
    pq = _dot(xb, wall_ref[:, A_COLS:A_COLS + Q_LORA])
    pkv = _dot(xb, wall_ref[:, A_COLS + Q_LORA:A_COLS + Q_LORA + KV_LORA])
    ppe = _dot(xb, wall_ref[:, A_COLS + Q_LORA + KV_LORA:])

    cqn = _bf(pq * lax.rsqrt(jnp.mean(pq * pq, axis=-1, keepdims=True) + RMS_EPS) * qg_ref[...])
    qa = _dot(cqn, wqa_ref[...])
    qb = _dot(cqn, wqb_ref[...])
    rope = rope_ref[...]
    lane = lax.broadcasted_iota(jnp.int32, (1, QK_PAD), 1)
    cq = jnp.where(lane < NOPE_DIM, np.float32(SCORE_SCALE), rope)
    sq = pltpu.roll(rope, QK_PAD - ROPE_DIM, axis=1)
    for h in range(B_HEADS):
        sl = slice(h * QK_PAD, (h + 1) * QK_PAD)
        q_ref[:, sl] = _bf(qa[:, sl] * cq + qb[:, sl] * sq)
    ckv = pkv * lax.rsqrt(jnp.mean(pkv * pkv, axis=-1, keepdims=True) + RMS_EPS) * kvg_ref[...]
    ckv_ref[...] = ckv
    ppe = ppe * rope
    kpe = ppe[:, :ROPE_DIM] + ppe[:, ROPE_DIM:2 * ROPE_DIM]
    kpe_ref[...] = kpe
    if kv_out:
        kx_ref, vx_ref = kv_out
        cb = _bf(ckv)
        kx_ref[...] = _bf(_dot(cb, wk_ref[...]) + _dot(_bf(kpe), place_ref[...]))
        vx_ref[...] = _bf(_dot(cb, wv_ref[...]))

    row = lax.broadcasted_iota(jnp.int32, (rows, 1), 0)
    if seg >= rows:
        first = jnp.where(b == 0, shift0_ref[0], carry_ref[...])
        starts = row == 0
        last = pa[rows - 1:rows, :]
        carry_ref[...] = last
        shift_ref[0] = last
    else:
        first = jnp.broadcast_to(shift0_ref[...], (rows // seg, seg, A_COLS)).reshape(rows, A_COLS)
        starts = row % seg == 0
        for s in range(rows // seg):
            shift_ref[s] = pa[(s + 1) * seg - 1:(s + 1) * seg, :]
    prev = jnp.where(starts, first, pltpu.roll(pa, 1, axis=0))
    xs = pa + (prev - pa) * mu_ref[...]

    r = xs[:, :D_A]
    k = xs[:, D_A:2 * D_A]
    v = xs[:, 2 * D_A:3 * D_A]
    wa = xs[:, 3 * D_A:3 * D_A + LANES]
    gd = xs[:, 3 * D_A + LANES:]

    z = w0_ref[...] + _dot(_bf(jnp.tanh(wa)), ww2_ref[...])
    ld = -np.float32(np.exp(-0.5)) * _sigmoid(z)
    a = _sigmoid(a0_ref[...] + _dot(_bf(wa), wa2_ref[...]))
    g_ref[...] = _dot(_bf(_sigmoid(gd)), wg2_ref[...])

    kkr = k * kk_ref[...]
    kh = k * (1.0 + (a - 1.0) * ka_ref[...])
    hi, lo = _split2(kkr * kkr)
    hs = _dot(jnp.concatenate([hi, lo, _bf(r * kh * rk_ref[...])], axis=0), esum_ref[...])
    kk = kkr / jnp.maximum(jnp.sqrt(hs[:rows] + hs[rows:2 * rows]), 1e-12)
    bonus_ref[...] = hs[2 * rows:] * v
    v_ref[...] = v

    ltri = ltri_ref[...]
    tri = ltri.shape[0]
    h1, h2, h3 = _split3(ld)
    cum = jnp.concatenate(
        [_dot(ltri, h1[i:i + tri]) + _dot(ltri, h2[i:i + tri]) + _dot(ltri, h3[i:i + tri])
         for i in range(0, rows, tri)], axis=0)
    ep = jnp.exp(cum)
    em = jnp.exp(-cum)
    rt_ref[...] = r * ep
    kt_ref[...] = kh * em
    bt_ref[...] = (kk * a) * em
    at_ref[...] = -kk * jnp.exp(cum - ld)
    for c in range(rows // chunk):
        wc_ref[c * 8:(c + 1) * 8, :] = jnp.broadcast_to(ep[(c + 1) * chunk - 1:(c + 1) * chunk, :], (8, D_A))


def _proj(x2, shift0, w, rope, *, nstreams, seg, rows, chunk, expand_kv):
    total = nstreams * seg
    spb = max(1, rows // seg)
    bps = max(1, seg // rows)
    nck = rows // chunk
    tri = min(rows, TRI_ROWS)
    ltri = _chunk_tri(tri, chunk)
    rowblk = lambda n: pl.BlockSpec((rows, n), lambda s, b: (s * bps + b, 0))
    in_specs = [
        rowblk(D_MODEL),
        pl.BlockSpec((spb, 1, A_COLS), lambda s, b: (s, 0, 0)),
        _full((D_MODEL, PROJ_COLS)),
        _full((1, A_COLS)), _full((1, D_A)), _full((1, D_A)), _full((1, D_A)), _full((1, D_A)), _full((1, D_A)),
        _full((LANES, D_A)), _full((LANES, D_A)), _full((GATE_LORA, D_A)),
        _full((D_A, D_A)), _full((tri, tri)),
        _full((1, Q_LORA)), _full((1, KV_LORA)),
        _full((Q_LORA, B_HEADS * QK_PAD)), _full((Q_LORA, B_HEADS * QK_PAD)),
        rowblk(LANES),
        _full((KV_LORA, B_HEADS * QK_PAD)), _full((ROPE_DIM, B_HEADS * QK_PAD)), _full((KV_LORA, D_B)),
    ]
    f32o = lambda n: jax.ShapeDtypeStruct((total, n), F32)
    out_shape = [f32o(D_A)] * 7 + [
        jax.ShapeDtypeStruct((total // chunk * 8, D_A), F32),
        jax.ShapeDtypeStruct((nstreams, 1, A_COLS), F32),
        jax.ShapeDtypeStruct((total, B_HEADS * QK_PAD), BF16),
        f32o(KV_LORA), f32o(ROPE_DIM),
    ]
    out_specs = [rowblk(D_A)] * 7 + [
        pl.BlockSpec((nck * 8, D_A), lambda s, b: (s * bps + b, 0)),
        pl.BlockSpec((spb, 1, A_COLS), lambda s, b: (s, 0, 0)),
        rowblk(B_HEADS * QK_PAD), rowblk(KV_LORA), rowblk(ROPE_DIM),
    ]
    if expand_kv:
        out_shape += [jax.ShapeDtypeStruct((total, B_HEADS * QK_PAD), BF16),
                      jax.ShapeDtypeStruct((total, D_B), BF16)]
        out_specs += [rowblk(B_HEADS * QK_PAD), rowblk(D_B)]
    return pl.pallas_call(
        functools.partial(_proj_kernel, rows=rows, chunk=chunk, seg=seg),
        out_shape=out_shape,
        grid=(nstreams // spb, bps),
        in_specs=in_specs,
        out_specs=out_specs,
        scratch_shapes=[pltpu.VMEM((1, A_COLS), F32)],
        compiler_params=pltpu.CompilerParams(
            dimension_semantics=("arbitrary", "arbitrary"), vmem_limit_bytes=VMEM_LIMIT),
        name="proj",
    )(x2, shift0, w["wall"], w["mu"], w["w0"], w["a0"], w["k_k"], w["k_a"], w["r_k"],
      w["ww2"], w["wa2"], w["wg2"], w["esum"], ltri, w["qg"], w["kvg"], w["wqa"], w["wqb"],
      rope, w["wk"], w["place"], w["wv"])


def _chunk_tri(rows, chunk):
    i = np.arange(rows)
    m = (i[:, None] // chunk == i[None, :] // chunk) & (i[None, :] <= i[:, None])
    return jnp.asarray(m, BF16)


def _attn_kernel(qi_ref, ki_ref, q_ref, k_ref, v_ref, o_ref, m_ref, acc_ref, *, blk):
    s_id = pl.program_id(1)
    qi = qi_ref[s_id]
    ki = ki_ref[s_id]

    @pl.when(ki == 0)
    def _():
        m_ref[...] = jnp.full(m_ref.shape, -jnp.inf, F32)
        acc_ref[...] = jnp.zeros(acc_ref.shape, F32)

    low = lax.broadcasted_iota(jnp.int32, (1, LANES), 1) < V_DIM

    def step(r0, nr, nc, masked):
        rs = slice(r0, r0 + nr)

        def scores(h):
            sl = slice(h * QK_PAD, (h + 1) * QK_PAD)
            return _dot_nt(q_ref[0, rs, sl], k_ref[0, :nc, sl])

        if masked:
            rq = (lax.broadcasted_iota(jnp.int32, (nr, nc), 0) + r0) // CHUNK
            mask = lax.broadcasted_iota(jnp.int32, (nr, nc), 1) // CHUNK <= rq
        one = jnp.ones((), BF16)
        ahead = 2
        pending = [scores(h) for h in range(ahead)]
        for h in range(B_HEADS):
            s = pending.pop(0)
            if h + ahead < B_HEADS:
                pending.append(scores(h + ahead))
            vp = v_ref[0, :nc, (h // 2) * LANES:(h // 2 + 1) * LANES]
            vext = jnp.where(low, vp, one) if h % 2 == 0 else jnp.where(low, one, vp)
            if masked:
                s = jnp.where(mask, s, -jnp.inf)
            m_prev = m_ref[h, rs]
            m_new = jnp.maximum(m_prev, jnp.max(s, axis=-1, keepdims=True))
            p = jnp.exp2(s - m_new[:, :1])
            acc_ref[h, rs] = jnp.exp2(m_prev - m_new) * acc_ref[h, rs] + _dot(_bf(p), vext)
            m_ref[h, rs] = m_new

    @pl.when(ki < qi)
    def _():
        step(0, blk, blk, False)

    @pl.when(ki == qi)
    def _():
        half = blk // 2
        step(0, half, half, True)
        step(half, half, blk, True)
        for j in range(B_HEADS // 2):
            a0 = acc_ref[2 * j]
            a1 = acc_ref[2 * j + 1]
            num = jnp.where(low, a0, a1)
            den = jnp.where(low, pltpu.roll(a0, V_DIM, axis=1), pltpu.roll(a1, V_DIM, axis=1))
            o_ref[0, :, j * LANES:(j + 1) * LANES] = num / den


def _attn(q, k, v, *, blk):
    nb, t, _ = q.shape
    steps = [(i, j) for i in range(t // blk) for j in range(i + 1)]
    qi = jnp.asarray([s[0] for s in steps], jnp.int32)
    ki = jnp.asarray([s[1] for s in steps], jnp.int32)
    grid_spec = pltpu.PrefetchScalarGridSpec(
        num_scalar_prefetch=2,
        grid=(nb, len(steps)),
        in_specs=[pl.BlockSpec((1, blk, B_HEADS * QK_PAD), lambda b, s, qi, ki: (b, qi[s], 0)),
                  pl.BlockSpec((1, blk, B_HEADS * QK_PAD), lambda b, s, qi, ki: (b, ki[s], 0)),
                  pl.BlockSpec((1, blk, D_B), lambda b, s, qi, ki: (b, ki[s], 0))],
        out_specs=pl.BlockSpec((1, blk, D_B), lambda b, s, qi, ki: (b, qi[s], 0)),
        scratch_shapes=[pltpu.VMEM((B_HEADS, blk, LANES), F32),
                        pltpu.VMEM((B_HEADS, blk, LANES), F32)])
    return pl.pallas_call(
        functools.partial(_attn_kernel, blk=blk),
        out_shape=jax.ShapeDtypeStruct((nb, t, D_B), F32),
        grid_spec=grid_spec,
        compiler_params=pltpu.CompilerParams(
            dimension_semantics=("arbitrary", "arbitrary"), vmem_limit_bytes=VMEM_LIMIT),
        name="attn",
    )(qi, ki, q, k, v)


def _attnc_kernel(q_ref, cckv_ref, ckpe_ref, nckv_ref, nkpe_ref, wk_ref, wv_ref, place_ref, o_ref, *, t):
    q = q_ref[0]
    heads = [q[:, h * QK_PAD:(h + 1) * QK_PAD] for h in range(B_HEADS)]
    qf = jnp.concatenate(heads, axis=0)
    qa = jnp.concatenate([_bf(_dot_nt(heads[h], wk_ref[:, h * QK_PAD:(h + 1) * QK_PAD]))
                          for h in range(B_HEADS)], axis=0)
    place = place_ref[:, :QK_PAD]

    def scores(ckv, kpe):
        cb = _bf(ckv)
        return _dot_nt(qa, cb) + _dot_nt(qf, _bf(_dot(_bf(kpe), place))), cb

    s_c, cb_c = scores(cckv_ref[0], ckpe_ref[0])
    s_n, cb_n = scores(nckv_ref[...], nkpe_ref[...])
    m = jnp.maximum(jnp.max(s_c, axis=-1, keepdims=True), jnp.max(s_n, axis=-1, keepdims=True))
    p_c = jnp.exp2(s_c - m)
    p_n = jnp.exp2(s_n - m)
    den = jnp.sum(p_c, axis=-1, keepdims=True) + jnp.sum(p_n, axis=-1, keepdims=True)
    lat = _bf((_dot(_bf(p_c), cb_c) + _dot(_bf(p_n), cb_n)) / den)
    low = lax.broadcasted_iota(jnp.int32, (1, LANES), 1) < V_DIM
    zero = jnp.zeros((), BF16)
    for j in range(B_HEADS // 2):
        wvp = wv_ref[:, j * LANES:(j + 1) * LANES]
        o_ref[0, :, j * LANES:(j + 1) * LANES] = (
            _dot(lat[2 * j * t:(2 * j + 1) * t], jnp.where(low, wvp, zero))
            + _dot(lat[(2 * j + 1) * t:(2 * j + 2) * t], jnp.where(low, zero, wvp)))


def _attnc(q, cache_ckv, cache_kpe, ckv, kpe, w):
    nb, t, _ = q.shape
    past = cache_ckv.shape[1]
    return pl.pallas_call(
        functools.partial(_attnc_kernel, t=t),
        out_shape=jax.ShapeDtypeStruct((nb, t, D_B), F32),
        grid=(nb,),
        in_specs=[pl.BlockSpec((1, t, B_HEADS * QK_PAD), lambda b: (b, 0, 0)),
                  pl.BlockSpec((1, past, KV_LORA), lambda b: (b, 0, 0)),
                  pl.BlockSpec((1, past, ROPE_DIM), lambda b: (b, 0, 0)),
                  pl.BlockSpec((t, KV_LORA), lambda b: (b, 0)),
                  pl.BlockSpec((t, ROPE_DIM), lambda b: (b, 0)),
                  _full((KV_LORA, B_HEADS * QK_PAD)), _full((KV_LORA, D_B)),
                  _full((ROPE_DIM, B_HEADS * QK_PAD))],
        out_specs=pl.BlockSpec((1, t, D_B), lambda b: (b, 0, 0)),
        compiler_params=pltpu.CompilerParams(dimension_semantics=("arbitrary",)),
        name="attnc",
    )(q, cache_ckv, cache_kpe, ckv, kpe, w["wk"], w["wv"], w["place"])


def _wkv_kernel(rt_ref, kt_ref, bt_ref, at_ref, v_ref, g_ref, bonus_ref, wc_ref, lg_ref, lb_ref,
                eavg_ref, h0_ref, y_ref, hout_ref, h_ref, *, chunk, group, nsub, nsteps):
    c = pl.program_id(1)
    C2 = 2 * chunk
    npair = A_HEADS // 2

    @pl.when(c == 0)
    def _():
        h_ref[...] = h0_ref[...]

    low = lax.broadcasted_iota(jnp.int32, (chunk, LANES), 1) < A_HEAD_DIM
    ii = lax.broadcasted_iota(jnp.int32, (C2, C2), 0)
    jj = lax.broadcasted_iota(jnp.int32, (C2, C2), 1)
    strict = ii > jj
    incl = ii >= jj
    eye_c = (ii == jj).astype(F32)
    ki = lax.broadcasted_iota(jnp.int32, (LANES, LANES), 0)
    kj = lax.broadcasted_iota(jnp.int32, (LANES, LANES), 1)
    eye_k = (ki == kj).astype(F32)
    eavg = eavg_ref[...]

    def stack(t):
        return jnp.concatenate([jnp.where(low, t, 0.0), jnp.where(low, 0.0, t)], axis=0)

    units = [(ci, j) for ci in range(nsub * group) for j in range(npair)]
    rows = lambda ci: slice(ci * chunk, (ci + 1) * chunk)
    lanes = lambda j: slice(j * LANES, (j + 1) * LANES)
    ld = lambda ref: [stack(ref[rows(ci), lanes(j)]) for ci, j in units]
    At, Bt, Kt, Rt, Vs = ld(at_ref), ld(bt_ref), ld(kt_ref), ld(rt_ref), ld(v_ref)
    nu = range(len(units))
    Vb = [_bf(Vs[u]) for u in nu]
    g1 = [_dot_nt(_bf(jnp.concatenate([At[u], Rt[u]], axis=0)),
                  _bf(jnp.concatenate([Bt[u], Kt[u]], axis=0))) for u in nu]
    Aab = [jnp.where(strict, g1[u][:C2, :C2], 0.0) for u in nu]
    Aak = [_bf(jnp.where(strict, g1[u][:C2, C2:], 0.0)) for u in nu]
    Arb = [_bf(jnp.where(incl, g1[u][C2:, :C2], 0.0)) for u in nu]
    Ark = [_bf(jnp.where(incl, g1[u][C2:, C2:], 0.0)) for u in nu]
    rcat = lambda *t: jnp.concatenate(t, axis=0)
    Tm = [eye_c + Aab[u] for u in nu]
    Nb = [_bf(Aab[u]) for u in nu]
    Pw = [_dot(Nb[u], Nb[u]) for u in nu]
    for _ in range(chunk.bit_length() - 3):
        Pb = [_bf(Pw[u]) for u in nu]
        st = [_dot(rcat(_bf(Tm[u]), Pb[u]), Pb[u]) for u in nu]
        Tm = [Tm[u] + st[u][:C2] for u in nu]
        Pw = [st[u][C2:] for u in nu]
    Tm = [Tm[u] + _dot(_bf(Tm[u]), _bf(Pw[u])) for u in nu]
    wrow = [wc_ref[ci * 8:ci * 8 + 1, lanes(j)] for ci, j in units]
    BwT = [_bf((Bt[u] * wrow[u]).T) for u in nu]
    KwT = [_bf((Kt[u] * wrow[u]).T) for u in nu]
    sv = [_dot(rcat(Aak[u], Ark[u], KwT[u]), Vb[u]) for u in nu]
    PPb = [_bf(_dot(_bf(Tm[u]), _bf(jnp.concatenate([At[u], sv[u][:C2]], axis=1)))) for u in nu]
    sp = [_dot(rcat(Arb[u], BwT[u]), PPb[u]) for u in nu]
    Q1M1, Q2, M2 = [], [], []
    for u in nu:
        q1s = Rt[u] + sp[u][:C2, :LANES]
        q2s = sp[u][:C2, LANES:] + sv[u][C2:2 * C2]
        m1 = eye_k * wrow[u] + sp[u][C2:, :LANES]
        Q1M1.append(_bf(rcat(q1s[:chunk] + q1s[chunk:], m1)))
        Q2.append(q2s[:chunk] + q2s[chunk:])
        M2.append(sp[u][C2:, LANES:] + sv[u][2 * C2:])

    H = {(si, j): h_ref[si, j] for si in range(nsub) for j in range(npair)}
    Y = []
    for u, (ci, j) in enumerate(units):
        key = (ci // group, j)
        sh = _dot(Q1M1[u], _bf(H[key]))
        Y.append(sh[:chunk] + Q2[u])
        H[key] = sh[chunk:] + M2[u]
    for (si, j), val in H.items():
        h_ref[si, j] = val

    def headmean(t):
        hi, lo = _split2(t)
        m = _dot(rcat(hi, lo), eavg)
        return m[:chunk] + m[chunk:]

    mu = [headmean(Y[u]) for u in nu]
    dv = [Y[u] - mu[u] for u in nu]
    var = [headmean(dv[u] * dv[u]) for u in nu]
    for u, (ci, j) in enumerate(units):
        yn = dv[u] * lax.rsqrt(var[u] + LNX_EPS) * lg_ref[:, lanes(j)] + lb_ref[:, lanes(j)]
        y_ref[rows(ci), lanes(j)] = (yn + bonus_ref[rows(ci), lanes(j)]) * g_ref[rows(ci), lanes(j)]

    @pl.when(c == nsteps - 1)
    def _():
        hout_ref[...] = h_ref[...]


def _wkv(rt, kt, bt, at, v, g, bonus, wc, h0, w, *, nstreams, ncs, chunk):
    total = rt.shape[0]
    group = min(WKV_UNITS, ncs)
    nsub = min(WKV_UNITS // group, nstreams)
    nsteps = ncs // group
    blk = pl.BlockSpec((nsub * group * chunk, D_A), lambda s, c: (s * nsteps + c, 0))
    hspec = pl.BlockSpec((nsub, A_HEADS // 2, LANES, LANES), lambda s, c: (s, 0, 0, 0))
    return pl.pallas_call(
        functools.partial(_wkv_kernel, chunk=chunk, group=group, nsub=nsub, nsteps=nsteps),
        out_shape=[jax.ShapeDtypeStruct((total, D_A), F32),
                   jax.ShapeDtypeStruct((nstreams, A_HEADS // 2, LANES, LANES), F32)],
        grid=(nstreams // nsub, nsteps),
        in_specs=[blk] * 7 + [pl.BlockSpec((nsub * group * 8, D_A), lambda s, c: (s * nsteps + c, 0)),
                              _full((1, D_A)), _full((1, D_A)), _full((LANES, LANES)), hspec],
        out_specs=[blk, hspec],
        scratch_shapes=[pltpu.VMEM((nsub, A_HEADS // 2, LANES, LANES), F32)],
        compiler_params=pltpu.CompilerParams(
            dimension_semantics=("arbitrary", "arbitrary"), vmem_limit_bytes=VMEM_LIMIT),
        name="wkv",
    )(rt, kt, bt, at, v, g, bonus, wc, w["lnx_g"], w["lnx_b"], w["eavg"], h0)


def _layer_norm(t, g, b):
    mu = jnp.mean(t, axis=-1, keepdims=True)
    d = t - mu
    var = jnp.mean(d * d, axis=-1, keepdims=True)
    return d * lax.rsqrt(var + LN_EPS) * g + b


def _tail_kernel(x_ref, ya_ref, yb_ref, wg_ref, bg_ref, wpa_ref, wpb_ref, wo_ref, l1g_ref, l1b_ref,
                 wgu_ref, wdown_ref, l2g_ref, l2b_ref, o_ref):
    x = x_ref[...]
    gates = _sigmoid(_dot(_bf(x), wg_ref[...]) + bg_ref[...])
    m = (gates[:, :D_MODEL] * _dot(_bf(ya_ref[...]), wpa_ref[...])
         + gates[:, D_MODEL:] * _dot(_bf(yb_ref[...]), wpb_ref[...]))
    h = _layer_norm(DN_ALPHA * x + _dot(_bf(m), wo_ref[...]), l1g_ref[...], l1b_ref[...])
    hb = _bf(h)

    def gate_up(c):
        cols = slice(c * FF_CHUNK, (c + 1) * FF_CHUNK)
        ucols = slice(D_FF + c * FF_CHUNK, D_FF + (c + 1) * FF_CHUNK)
        return _dot(hb, wgu_ref[:, cols]), _dot(hb, wgu_ref[:, ucols])

    f = None
    nxt = gate_up(0)
    for c in range(N_FF):
        gate, up = nxt
        if c + 1 < N_FF:
            nxt = gate_up(c + 1)
        d = _dot(_bf(gate * _sigmoid(gate) * up), wdown_ref[c * FF_CHUNK:(c + 1) * FF_CHUNK, :])
        f = d if f is None else f + d
    o_ref[...] = _layer_norm(DN_ALPHA * h + f, l2g_ref[...], l2b_ref[...])


def _tail(x2, ya, yb, w, *, rows):
    total = x2.shape[0]
    rowblk = lambda n: pl.BlockSpec((rows, n), lambda i: (i, 0))

    def const(shape):
        n = len(shape)
        return pl.BlockSpec(shape, lambda i: (0,) * n, pipeline_mode=pl.Buffered(1))

    return pl.pallas_call(
        _tail_kernel,
        out_shape=jax.ShapeDtypeStruct((total, D_MODEL), F32),
        grid=(total // rows,),
        in_specs=[rowblk(D_MODEL), rowblk(D_A), rowblk(D_B),
                  const((D_MODEL, 2 * D_MODEL)), const((1, 2 * D_MODEL)),
                  const((D_A, D_MODEL)), const((D_B, D_MODEL)), const((D_MODEL, D_MODEL)),
                  const((1, D_MODEL)), const((1, D_MODEL)),
                  const((D_MODEL, 2 * D_FF)), const((D_FF, D_MODEL)),
                  const((1, D_MODEL)), const((1, D_MODEL))],
        out_specs=rowblk(D_MODEL),
        compiler_params=pltpu.CompilerParams(
            dimension_semantics=("arbitrary",), vmem_limit_bytes=VMEM_LIMIT),
        name="tail",
    )(x2, ya, yb, w["wg"], w["bg"], w["wpa"], w["wpb"], w["wo"], w["l1g"], w["l1b"],
      w["wgu"], w["wdown"], w["l2g"], w["l2b"])


def _prep_weights(w_in, mu_shift, w0, w_w2, a0, w_a2, w_g2, k_k, k_a, r_k, lnx_g, lnx_b, w_pa,
                  q_norm_g, w_uq, kv_norm_g, w_ukv, w_pb, b_gate, w_o, ln1_g, ln1_b, w_gu, w_down,
                  ln2_g, ln2_b):
    row = lambda t: t.reshape(1, -1).astype(F32)
    nb = A_COLS + B_COLS
    pe = w_in[:, nb - ROPE_DIM:nb]
    half = ROPE_DIM // 2
    pe_sw = jnp.concatenate([pe[:, half:], pe[:, :half]], axis=1)
    wall = jnp.concatenate([w_in[:, :nb - ROPE_DIM], pe, pe_sw,
                            jnp.zeros((D_MODEL, LANES - 2 * ROPE_DIM), F32)], axis=1)
    uq = w_uq.reshape(Q_LORA, B_HEADS, NOPE_DIM + ROPE_DIM)
    nope, r1, r2 = uq[..., :NOPE_DIM], uq[..., NOPE_DIM:NOPE_DIM + half], uq[..., NOPE_DIM + half:]
    zpad = jnp.zeros((Q_LORA, B_HEADS, QK_PAD - NOPE_DIM - ROPE_DIM), F32)
    wqa = jnp.concatenate([nope, r1, r2, zpad], axis=-1).reshape(Q_LORA, B_HEADS * QK_PAD)
    wqb = jnp.concatenate([jnp.zeros_like(nope), r2, r1, zpad], axis=-1).reshape(Q_LORA, B_HEADS * QK_PAD)
    ukv = w_ukv.reshape(KV_LORA, B_HEADS, NOPE_DIM + V_DIM)
    wk = jnp.concatenate([ukv[..., :NOPE_DIM], jnp.zeros((KV_LORA, B_HEADS, QK_PAD - NOPE_DIM), F32)],
                         axis=-1).reshape(KV_LORA, B_HEADS * QK_PAD)
    wv = ukv[..., NOPE_DIM:].reshape(KV_LORA, D_B)
    place = np.zeros((ROPE_DIM, B_HEADS * QK_PAD), np.float32)
    for h in range(B_HEADS):
        place[np.arange(ROPE_DIM), h * QK_PAD + NOPE_DIM + np.arange(ROPE_DIM)] = 1.0
    hid = np.arange(D_A) // A_HEAD_DIM
    esum = (hid[:, None] == hid[None, :]).astype(np.float32)
    lid = np.arange(LANES) // A_HEAD_DIM
    eavg = (lid[:, None] == lid[None, :]).astype(np.float32) / A_HEAD_DIM
    zl = jnp.zeros((LANES - DECAY_LORA, D_A), F32)
    return {
        "wall": _bf(wall), "mu": row(mu_shift), "w0": row(w0), "a0": row(a0), "k_k": row(k_k),
        "k_a": row(k_a), "r_k": row(r_k),
        "ww2": _bf(jnp.concatenate([w_w2, zl], axis=0)), "wa2": _bf(jnp.concatenate([zl, w_a2], axis=0)),
        "wg2": _bf(w_g2), "esum": jnp.asarray(esum, BF16), "eavg": jnp.asarray(eavg, BF16),
        "qg": row(q_norm_g), "kvg": row(kv_norm_g), "wqa": _bf(wqa), "wqb": _bf(wqb),
        "wk": _bf(wk), "wv": _bf(wv), "place": jnp.asarray(place, BF16),
        "lnx_g": row(lnx_g), "lnx_b": row(lnx_b),
        "wg": _bf(w_in[:, nb:]), "bg": row(b_gate), "wpa": _bf(w_pa), "wpb": _bf(w_pb), "wo": _bf(w_o),
        "l1g": row(ln1_g), "l1b": row(ln1_b), "l2g": row(ln2_g), "l2b": row(ln2_b),
        "wgu": _bf(w_gu), "wdown": _bf(w_down),
    }


def _rope_table(pos0, t, reps):
    half = ROPE_DIM // 2
    inv = ROPE_BASE ** (-jnp.arange(half, dtype=F32) / half)
    ang = (pos0 + jnp.arange(t)).astype(F32)[:, None] * jnp.tile(inv, LANES // half)[None, :]
    grp = np.arange(LANES) // half
    sc = np.where(grp >= 4, SCORE_SCALE, 1.0)
    mc = jnp.asarray(np.where(grp % 4 < 2, sc, 0.0), F32)
    ms = jnp.asarray(np.where(grp % 4 == 2, -sc, np.where(grp % 4 == 3, sc, 0.0)), F32)
    return jnp.tile(jnp.cos(ang) * mc + jnp.sin(ang) * ms, (reps, 1))


def _state_to_pairs(s):
    nb = s.shape[0]
    st = jnp.swapaxes(s, -1, -2).reshape(nb, A_HEADS // 2, 2, A_HEAD_DIM, A_HEAD_DIM)
    z = jnp.zeros_like(st[:, :, 0])
    top = jnp.concatenate([st[:, :, 0], z], axis=-1)
    bot = jnp.concatenate([z, st[:, :, 1]], axis=-1)
    return jnp.concatenate([top, bot], axis=-2)


def _pairs_to_state(hp):
    nb = hp.shape[0]
    diag = jnp.stack([hp[:, :, :A_HEAD_DIM, :A_HEAD_DIM], hp[:, :, A_HEAD_DIM:, A_HEAD_DIM:]], axis=2)
    return jnp.swapaxes(diag.reshape(nb, A_HEADS, A_HEAD_DIM, A_HEAD_DIM), -1, -2)


def _layer(x, pos0, shift0, wkv0, cache, w):
    nstreams, t, _ = x.shape
    total = nstreams * t
    x2 = x.reshape(total, D_MODEL)
    chunk = min(CHUNK, t)
    rope = _rope_table(pos0, t, nstreams)
    (rt, kt, bt, at, v, g, bonus, wc, shift, q, ckv, kpe, *kv) = _proj(
        x2, shift0, w, rope, nstreams=nstreams, seg=t, rows=min(PROJ_ROWS, total), chunk=chunk,
        expand_kv=cache is None)

    ya, hout = _wkv(rt, kt, bt, at, v, g, bonus, wc, _state_to_pairs(wkv0), w,
                    nstreams=nstreams, ncs=t // chunk, chunk=chunk)

    if cache is None:
        kk, vv = kv
        yb = _attn(q.reshape(nstreams, t, -1), kk.reshape(nstreams, t, -1), vv.reshape(nstreams, t, -1),
                   blk=min(ATTN_BLOCK, t))
    else:
        yb = _attnc(q.reshape(nstreams, t, -1), cache[0], cache[1], ckv, kpe, w)

    y = _tail(x2, ya, yb.reshape(total, D_B), w, rows=min(TAIL_ROWS, total))
    return (y.reshape(nstreams, t, D_MODEL), ckv.reshape(nstreams, t, KV_LORA),
            kpe.reshape(nstreams, t, ROPE_DIM), _pairs_to_state(hout), shift)


def kernel(x_prompt, x_sample, cache_ckv, cache_kpe, state_wkv, state_shift, w_in, mu_shift, w0, w_w2, a0,
           w_a2, w_g2, k_k, k_a, r_k, lnx_g, lnx_b, w_pa, q_norm_g, w_uq, kv_norm_g, w_ukv, w_pb, b_gate,
           w_o, ln1_g, ln1_b, w_gu, w_down, ln2_g, ln2_b):
    w = _prep_weights(w_in, mu_shift, w0, w_w2, a0, w_a2, w_g2, k_k, k_a, r_k, lnx_g, lnx_b, w_pa,
                      q_norm_g, w_uq, kv_norm_g, w_ukv, w_pb, b_gate, w_o, ln1_g, ln1_b, w_gu, w_down,
                      ln2_g, ln2_b)
    bp = x_prompt.shape[0]
    y_p, ckv_p, kpe_p, wkv_p, shift_p = _layer(
        x_prompt, 0, jnp.zeros((bp, 1, A_COLS), F32),
        jnp.zeros((bp, A_HEADS, A_HEAD_DIM, A_HEAD_DIM), F32), None, w)
    y_s, ckv_s, kpe_s, wkv_s, shift_s = _layer(
        x_sample, cache_ckv.shape[1], state_shift, state_wkv, (cache_ckv, cache_kpe), w)
    return (y_p, y_s, ckv_p, kpe_p, wkv_p, shift_p, ckv_s, kpe_s, wkv_s, shift_s)
```

```python
import functools

import numpy as np
import jax
import jax.numpy as jnp
from jax import lax
from jax.experimental import pallas as pl
from jax.experimental.pallas import tpu as pltpu

D_MODEL = 1024
CHUNK = 64
A_HEADS = 8
A_HEAD_DIM = 64
D_A = 512
DECAY_LORA = 64
AAA_LORA = 64
GATE_LORA = 128
A_COLS = 3 * D_A + DECAY_LORA + AAA_LORA + GATE_LORA
LNX_EPS = A_HEAD_DIM * 1e-5
B_HEADS = 8
Q_LORA = 256
KV_LORA = 128
NOPE_DIM = 64
ROPE_DIM = 32
V_DIM = 64
D_B = 512
B_COLS = Q_LORA + KV_LORA + ROPE_DIM
ROPE_BASE = 10000.0
ATTN_SCALE = (NOPE_DIM + ROPE_DIM) ** -0.5
SCORE_SCALE = ATTN_SCALE * float(np.log2(np.e))
RMS_EPS = 1e-6
D_FF = 2816
LN_EPS = 1e-5
DN_ALPHA = 2.0 ** 0.25

LANES = 128
QK_PAD = 128
PROJ_COLS = A_COLS + Q_LORA + KV_LORA + LANES
FF_CHUNK = 256
N_FF = D_FF // FF_CHUNK
TRI_ROWS = 256
PROJ_ROWS = 512
TAIL_ROWS = 512
ATTN_BLOCK = 1024
WKV_UNITS = 8
VMEM_LIMIT = 56 * 1024 * 1024

F32 = jnp.float32
BF16 = jnp.bfloat16


def _dot(a, b):
    return jnp.dot(a, b, preferred_element_type=F32)


def _dot_nt(a, b):
    return lax.dot_general(a, b, (((1,), (1,)), ((), ())), preferred_element_type=F32)


def _bf(x):
    return x.astype(BF16)


def _split2(x):
    hi = x.astype(BF16)
    lo = (x - hi.astype(F32)).astype(BF16)
    return hi, lo


def _split3(x):
    h1 = x.astype(BF16)
    r1 = x - h1.astype(F32)
    h2 = r1.astype(BF16)
    h3 = (r1 - h2.astype(F32)).astype(BF16)
    return h1, h2, h3


def _sigmoid(z):
    return 1.0 / (1.0 + jnp.exp(-z))


def _full(shape):
    n = len(shape)
    return pl.BlockSpec(shape, lambda *_: (0,) * n)


def _proj_kernel(x_ref, shift0_ref, wall_ref, mu_ref, w0_ref, a0_ref, kk_ref, ka_ref, rk_ref,
                 ww2_ref, wa2_ref, wg2_ref, esum_ref, ltri_ref, qg_ref, kvg_ref, wqa_ref, wqb_ref,
                 rope_ref, wk_ref, place_ref, wv_ref,
                 rt_ref, kt_ref, bt_ref, at_ref, v_ref, g_ref, bonus_ref, wc_ref, shift_ref,
                 q_ref, ckv_ref, kpe_ref, *rest, rows, chunk, seg):
    *kv_out, carry_ref = rest
    b = pl.program_id(1)
    xb = _bf(x_ref[...])

    pa = _dot(xb, wall_ref[:, :A_COLS])
    pq = _dot(xb, wall_ref[:, A_COLS:A_COLS + Q_LORA])
    pkv = _dot(xb, wall_ref[:, A_COLS + Q_LORA:A_COLS + Q_LORA + KV_LORA])
    ppe = _dot(xb, wall_ref[:, A_COLS + Q_LORA + KV_LORA:])

    cqn = _bf(pq * lax.rsqrt(jnp.mean(pq * pq, axis=-1, keepdims=True) + RMS_EPS) * qg_ref[...])
    qa = _dot(cqn, wqa_ref[...])
    qb = _dot(cqn, wqb_ref[...])
    rope = rope_ref[...]
    lane = lax.broadcasted_iota(jnp.int32, (1, QK_PAD), 1)
    cq = jnp.where(lane < NOPE_DIM, np.float32(SCORE_SCALE), rope)
    sq = pltpu.roll(rope, QK_PAD - ROPE_DIM, axis=1)
    for h in range(B_HEADS):
        sl = slice(h * QK_PAD, (h + 1) * QK_PAD)
        q_ref[:, sl] = _bf(qa[:, sl] * cq + qb[:, sl] * sq)
    ckv = pkv * lax.rsqrt(jnp.mean(pkv * pkv, axis=-1, keepdims=True) + RMS_EPS) * kvg_ref[...]
    ckv_ref[...] = ckv
    ppe = ppe * rope
    kpe = ppe[:, :ROPE_DIM] + ppe[:, ROPE_DIM:2 * ROPE_DIM]
    kpe_ref[...] = kpe

    row = lax.broadcasted_iota(jnp.int32, (rows, 1), 0)
    if seg >= rows:
        first = jnp.where(b == 0, shift0_ref[0], carry_ref[...])
        starts = row == 0
        last = pa[rows - 1:rows, :]
        carry_ref[...] = last
        shift_ref[0] = last
    else:
        first = jnp.broadcast_to(shift0_ref[...], (rows // seg, seg, A_COLS)).reshape(rows, A_COLS)
        starts = row % seg == 0
        for s in range(rows // seg):
            shift_ref[s] = pa[(s + 1) * seg - 1:(s + 1) * seg, :]
    prev = jnp.where(starts, first, pltpu.roll(pa, 1, axis=0))
    xs = pa + (prev - pa) * mu_ref[...]

    r = xs[:, :D_A]
    k = xs[:, D_A:2 * D_A]
    v = xs[:, 2 * D_A:3 * D_A]
    wa = xs[:, 3 * D_A:3 * D_A + LANES]
    gd = xs[:, 3 * D_A + LANES:]

    z = w0_ref[...] + _dot(_bf(jnp.tanh(wa)), ww2_ref[...])
    ld = -np.float32(np.exp(-0.5)) * _sigmoid(z)
    a = _sigmoid(a0_ref[...] + _dot(_bf(wa), wa2_ref[...]))
    g_ref[...] = _dot(_bf(_sigmoid(gd)), wg2_ref[...])

    kkr = k * kk_ref[...]
    kh = k * (1.0 + (a - 1.0) * ka_ref[...])
    hi, lo = _split2(kkr * kkr)
    hs = _dot(jnp.concatenate([hi, lo, _bf(r * kh * rk_ref[...])], axis=0), esum_ref[...])
    kk = kkr / jnp.maximum(jnp.sqrt(hs[:rows] + hs[rows:2 * rows]), 1e-12)
    bonus_ref[...] = hs[2 * rows:] * v
    v_ref[...] = v

    ltri = ltri_ref[...]
    tri = ltri.shape[0]
    h1, h2, h3 = _split3(ld)
    cum = jnp.concatenate(
        [_dot(ltri, h1[i:i + tri]) + _dot(ltri, h2[i:i + tri]) + _dot(ltri, h3[i:i + tri])
         for i in range(0, rows, tri)], axis=0)
    ep = jnp.exp(cum)
    em = jnp.exp(-cum)
    rt_ref[...] = r * ep
    kt_ref[...] = kh * em
    bt_ref[...] = (kk * a) * em
    at_ref[...] = -kk * jnp.exp(cum - ld)
    for c in range(rows // chunk):
        wc_ref[c * 8:(c + 1) * 8, :] = jnp.broadcast_to(ep[(c + 1) * chunk - 1:(c + 1) * chunk, :], (8, D_A))

    if kv_out:
        kx_ref, vx_ref = kv_out
        cb = _bf(ckv)
        kx_ref[...] = _bf(_dot(cb, wk_ref[...]) + _dot(_bf(kpe), place_ref[...]))
        vx_ref[...] = _bf(_dot(cb, wv_ref[...]))


def _proj(x2, shift0, w, rope, *, nstreams, seg, rows, chunk, expand_kv):
    total = nstreams * seg
    spb = max(1, rows // seg)
    bps = max(1, seg // rows)
    nck = rows // chunk
    tri = min(rows, TRI_ROWS)
    ltri = _chunk_tri(tri, chunk)
    rowblk = lambda n: pl.BlockSpec((rows, n), lambda s, b: (s * bps + b, 0))
    in_specs = [
        rowblk(D_MODEL),
        pl.BlockSpec((spb, 1, A_COLS), lambda s, b: (s, 0, 0)),
        _full((D_MODEL, PROJ_COLS)),
        _full((1, A_COLS)), _full((1, D_A)), _full((1, D_A)), _full((1, D_A)), _full((1, D_A)), _full((1, D_A)),
        _full((LANES, D_A)), _full((LANES, D_A)), _full((GATE_LORA, D_A)),
        _full((D_A, D_A)), _full((tri, tri)),
        _full((1, Q_LORA)), _full((1, KV_LORA)),
        _full((Q_LORA, B_HEADS * QK_PAD)), _full((Q_LORA, B_HEADS * QK_PAD)),
        rowblk(LANES),
        _full((KV_LORA, B_HEADS * QK_PAD)), _full((ROPE_DIM, B_HEADS * QK_PAD)), _full((KV_LORA, D_B)),
    ]
    f32o = lambda n: jax.ShapeDtypeStruct((total, n), F32)
    out_shape = [f32o(D_A)] * 7 + [
        jax.ShapeDtypeStruct((total // chunk * 8, D_A), F32),
        jax.ShapeDtypeStruct((nstreams, 1, A_COLS), F32),
        jax.ShapeDtypeStruct((total, B_HEADS * QK_PAD), BF16),
        f32o(KV_LORA), f32o(ROPE_DIM),
    ]
    out_specs = [rowblk(D_A)] * 7 + [
        pl.BlockSpec((nck * 8, D_A), lambda s, b: (s * bps + b, 0)),
        pl.BlockSpec((spb, 1, A_COLS), lambda s, b: (s, 0, 0)),
        rowblk(B_HEADS * QK_PAD), rowblk(KV_LORA), rowblk(ROPE_DIM),
    ]
    if expand_kv:
        out_shape += [jax.ShapeDtypeStruct((total, B_HEADS * QK_PAD), BF16),
                      jax.ShapeDtypeStruct((total, D_B), BF16)]
        out_specs += [rowblk(B_HEADS * QK_PAD), rowblk(D_B)]
    return pl.pallas_call(
        functools.partial(_proj_kernel, rows=rows, chunk=chunk, seg=seg),
        out_shape=out_shape,
        grid=(nstreams // spb, bps),
        in_specs=in_specs,
        out_specs=out_specs,
        scratch_shapes=[pltpu.VMEM((1, A_COLS), F32)],
        compiler_params=pltpu.CompilerParams(
            dimension_semantics=("arbitrary", "arbitrary"), vmem_limit_bytes=VMEM_LIMIT),
        name="proj",
    )(x2, shift0, w["wall"], w["mu"], w["w0"], w["a0"], w["k_k"], w["k_a"], w["r_k"],
      w["ww2"], w["wa2"], w["wg2"], w["esum"], ltri, w["qg"], w["kvg"], w["wqa"], w["wqb"],
      rope, w["wk"], w["place"], w["wv"])


def _chunk_tri(rows, chunk):
    i = np.arange(rows)
    m = (i[:, None] // chunk == i[None, :] // chunk) & (i[None, :] <= i[:, None])
    return jnp.asarray(m, BF16)


def _attn_kernel(qi_ref, ki_ref, q_ref, k_ref, v_ref, o_ref, m_ref, acc_ref, *, blk):
    s_id = pl.program_id(1)
    qi = qi_ref[s_id]
    ki = ki_ref[s_id]

    @pl.when(ki == 0)
    def _():
        m_ref[...] = jnp.full(m_ref.shape, -jnp.inf, F32)
        acc_ref[...] = jnp.zeros(acc_ref.shape, F32)

    low = lax.broadcasted_iota(jnp.int32, (1, LANES), 1) < V_DIM

    def step(r0, nr, nc, masked):
        rs = slice(r0, r0 + nr)

        def scores(h):
            sl = slice(h * QK_PAD, (h + 1) * QK_PAD)
            return _dot_nt(q_ref[0, rs, sl], k_ref[0, :nc, sl])

        if masked:
            rq = (lax.broadcasted_iota(jnp.int32, (nr, nc), 0) + r0) // CHUNK
            mask = lax.broadcasted_iota(jnp.int32, (nr, nc), 1) // CHUNK <= rq
        one = jnp.ones((), BF16)
        ahead = 2
        pending = [scores(h) for h in range(ahead)]
        for h in range(B_HEADS):
            s = pending.pop(0)
            if h + ahead < B_HEADS:
                pending.append(scores(h + ahead))
            vp = v_ref[0, :nc, (h // 2) * LANES:(h // 2 + 1) * LANES]
            vext = jnp.where(low, vp, one) if h % 2 == 0 else jnp.where(low, one, vp)
            if masked:
                s = jnp.where(mask, s, -jnp.inf)
            m_prev = m_ref[h, rs]
            m_new = jnp.maximum(m_prev, jnp.max(s, axis=-1, keepdims=True))
            p = jnp.exp2(s - m_new[:, :1])
            acc_ref[h, rs] = jnp.exp2(m_prev - m_new) * acc_ref[h, rs] + _dot(_bf(p), vext)
            m_ref[h, rs] = m_new

    @pl.when(ki < qi)
    def _():
        step(0, blk, blk, False)

    @pl.when(ki == qi)
    def _():
        half = blk // 2
        step(0, half, half, True)
        step(half, half, blk, True)
        for j in range(B_HEADS // 2):
            a0 = acc_ref[2 * j]
            a1 = acc_ref[2 * j + 1]
            num = jnp.where(low, a0, a1)
            den = jnp.where(low, pltpu.roll(a0, V_DIM, axis=1), pltpu.roll(a1, V_DIM, axis=1))
            o_ref[0, :, j * LANES:(j + 1) * LANES] = num / den


def _attn(q, k, v, *, blk):
    nb, t, _ = q.shape
    steps = [(i, j) for i in range(t // blk) for j in range(i + 1)]
    qi = jnp.asarray([s[0] for s in steps], jnp.int32)
    ki = jnp.asarray([s[1] for s in steps], jnp.int32)
    grid_spec = pltpu.PrefetchScalarGridSpec(
        num_scalar_prefetch=2,
        grid=(nb, len(steps)),
        in_specs=[pl.BlockSpec((1, blk, B_HEADS * QK_PAD), lambda b, s, qi, ki: (b, qi[s], 0)),
                  pl.BlockSpec((1, blk, B_HEADS * QK_PAD), lambda b, s, qi, ki: (b, ki[s], 0)),
                  pl.BlockSpec((1, blk, D_B), lambda b, s, qi, ki: (b, ki[s], 0))],
        out_specs=pl.BlockSpec((1, blk, D_B), lambda b, s, qi, ki: (b, qi[s], 0)),
        scratch_shapes=[pltpu.VMEM((B_HEADS, blk, LANES), F32),
                        pltpu.VMEM((B_HEADS, blk, LANES), F32)])
    return pl.pallas_call(
        functools.partial(_attn_kernel, blk=blk),
        out_shape=jax.ShapeDtypeStruct((nb, t, D_B), F32),
        grid_spec=grid_spec,
        compiler_params=pltpu.CompilerParams(
            dimension_semantics=("arbitrary", "arbitrary"), vmem_limit_bytes=VMEM_LIMIT),
        name="attn",
    )(qi, ki, q, k, v)


def _attnc_kernel(q_ref, cckv_ref, ckpe_ref, nckv_ref, nkpe_ref, wk_ref, wv_ref, place_ref, o_ref, *, t):
    q = q_ref[0]
    heads = [q[:, h * QK_PAD:(h + 1) * QK_PAD] for h in range(B_HEADS)]
    qf = jnp.concatenate(heads, axis=0)
    qa = jnp.concatenate([_bf(_dot_nt(heads[h], wk_ref[:, h * QK_PAD:(h + 1) * QK_PAD]))
                          for h in range(B_HEADS)], axis=0)
    place = place_ref[:, :QK_PAD]

    def scores(ckv, kpe):
        cb = _bf(ckv)
        return _dot_nt(qa, cb) + _dot_nt(qf, _bf(_dot(_bf(kpe), place))), cb

    s_c, cb_c = scores(cckv_ref[0], ckpe_ref[0])
    s_n, cb_n = scores(nckv_ref[...], nkpe_ref[...])
    m = jnp.maximum(jnp.max(s_c, axis=-1, keepdims=True), jnp.max(s_n, axis=-1, keepdims=True))
    p_c = jnp.exp2(s_c - m)
    p_n = jnp.exp2(s_n - m)
    den = jnp.sum(p_c, axis=-1, keepdims=True) + jnp.sum(p_n, axis=-1, keepdims=True)
    lat = _bf((_dot(_bf(p_c), cb_c) + _dot(_bf(p_n), cb_n)) / den)
    low = lax.broadcasted_iota(jnp.int32, (1, LANES), 1) < V_DIM
    zero = jnp.zeros((), BF16)
    for j in range(B_HEADS // 2):
        wvp = wv_ref[:, j * LANES:(j + 1) * LANES]
        o_ref[0, :, j * LANES:(j + 1) * LANES] = (
            _dot(lat[2 * j * t:(2 * j + 1) * t], jnp.where(low, wvp, zero))
            + _dot(lat[(2 * j + 1) * t:(2 * j + 2) * t], jnp.where(low, zero, wvp)))


def _attnc(q, cache_ckv, cache_kpe, ckv, kpe, w):
    nb, t, _ = q.shape
    past = cache_ckv.shape[1]
    return pl.pallas_call(
        functools.partial(_attnc_kernel, t=t),
        out_shape=jax.ShapeDtypeStruct((nb, t, D_B), F32),
        grid=(nb,),
        in_specs=[pl.BlockSpec((1, t, B_HEADS * QK_PAD), lambda b: (b, 0, 0)),
                  pl.BlockSpec((1, past, KV_LORA), lambda b: (b, 0, 0)),
                  pl.BlockSpec((1, past, ROPE_DIM), lambda b: (b, 0, 0)),
                  pl.BlockSpec((t, KV_LORA), lambda b: (b, 0)),
                  pl.BlockSpec((t, ROPE_DIM), lambda b: (b, 0)),
                  _full((KV_LORA, B_HEADS * QK_PAD)), _full((KV_LORA, D_B)),
                  _full((ROPE_DIM, B_HEADS * QK_PAD))],
        out_specs=pl.BlockSpec((1, t, D_B), lambda b: (b, 0, 0)),
        compiler_params=pltpu.CompilerParams(dimension_semantics=("arbitrary",)),
        name="attnc",
    )(q, cache_ckv, cache_kpe, ckv, kpe, w["wk"], w["wv"], w["place"])


def _wkv_kernel(rt_ref, kt_ref, bt_ref, at_ref, v_ref, g_ref, bonus_ref, wc_ref, lg_ref, lb_ref,
                eavg_ref, h0_ref, y_ref, hout_ref, h_ref, *, chunk, group, nsub, nsteps):
    c = pl.program_id(1)
    C2 = 2 * chunk
    npair = A_HEADS // 2

    @pl.when(c == 0)
    def _():
        h_ref[...] = h0_ref[...]

    low = lax.broadcasted_iota(jnp.int32, (chunk, LANES), 1) < A_HEAD_DIM
    ii = lax.broadcasted_iota(jnp.int32, (C2, C2), 0)
    jj = lax.broadcasted_iota(jnp.int32, (C2, C2), 1)
    strict = ii > jj
    incl = ii >= jj
    eye_c = (ii == jj).astype(F32)
    ki = lax.broadcasted_iota(jnp.int32, (LANES, LANES), 0)
    kj = lax.broadcasted_iota(jnp.int32, (LANES, LANES), 1)
    eye_k = (ki == kj).astype(F32)
    eavg = eavg_ref[...]

    def stack(t):
        return jnp.concatenate([jnp.where(low, t, 0.0), jnp.where(low, 0.0, t)], axis=0)

    units = [(ci, j) for ci in range(nsub * group) for j in range(npair)]
    rows = lambda ci: slice(ci * chunk, (ci + 1) * chunk)
    lanes = lambda j: slice(j * LANES, (j + 1) * LANES)
    ld = lambda ref: [stack(ref[rows(ci), lanes(j)]) for ci, j in units]
    At, Bt, Kt, Rt, Vs = ld(at_ref), ld(bt_ref), ld(kt_ref), ld(rt_ref), ld(v_ref)
    nu = range(len(units))
    Vb = [_bf(Vs[u]) for u in nu]
    g1 = [_dot_nt(_bf(jnp.concatenate([At[u], Rt[u]], axis=0)),
                  _bf(jnp.concatenate([Bt[u], Kt[u]], axis=0))) for u in nu]
    Aab = [jnp.where(strict, g1[u][:C2, :C2], 0.0) for u in nu]
    Aak = [_bf(jnp.where(strict, g1[u][:C2, C2:], 0.0)) for u in nu]
    Arb = [_bf(jnp.where(incl, g1[u][C2:, :C2], 0.0)) for u in nu]
    Ark = [_bf(jnp.where(incl, g1[u][C2:, C2:], 0.0)) for u in nu]
    rcat = lambda *t: jnp.concatenate(t, axis=0)
    Tm = [eye_c + Aab[u] for u in nu]
    Nb = [_bf(Aab[u]) for u in nu]
    Pw = [_dot(Nb[u], Nb[u]) for u in nu]
    for _ in range(chunk.bit_length() - 3):
        Pb = [_bf(Pw[u]) for u in nu]
        st = [_dot(rcat(_bf(Tm[u]), Pb[u]), Pb[u]) for u in nu]
        Tm = [Tm[u] + st[u][:C2] for u in nu]
        Pw = [st[u][C2:] for u in nu]
    Tm = [Tm[u] + _dot(_bf(Tm[u]), _bf(Pw[u])) for u in nu]
    wrow = [wc_ref[ci * 8:ci * 8 + 1, lanes(j)] for ci, j in units]
    BwT = [_bf((Bt[u] * wrow[u]).T) for u in nu]
    KwT = [_bf((Kt[u] * wrow[u]).T) for u in nu]
    sv = [_dot(rcat(Aak[u], Ark[u], KwT[u]), Vb[u]) for u in nu]
    PPb = [_bf(_dot(_bf(Tm[u]), _bf(jnp.concatenate([At[u], sv[u][:C2]], axis=1)))) for u in nu]
    sp = [_dot(rcat(Arb[u], BwT[u]), PPb[u]) for u in nu]
    Q1M1, Q2, M2 = [], [], []
    for u in nu:
        q1s = Rt[u] + sp[u][:C2, :LANES]
        q2s = sp[u][:C2, LANES:] + sv[u][C2:2 * C2]
        m1 = eye_k * wrow[u] + sp[u][C2:, :LANES]
        Q1M1.append(_bf(rcat(q1s[:chunk] + q1s[chunk:], m1)))
        Q2.append(q2s[:chunk] + q2s[chunk:])
        M2.append(sp[u][C2:, LANES:] + sv[u][2 * C2:])

    H = {(si, j): h_ref[si, j] for si in range(nsub) for j in range(npair)}
    Y = []
    for u, (ci, j) in enumerate(units):
        key = (ci // group, j)
        sh = _dot(Q1M1[u], _bf(H[key]))
        Y.append(sh[:chunk] + Q2[u])
        H[key] = sh[chunk:] + M2[u]
    for (si, j), val in H.items():
        h_ref[si, j] = val

    def headmean(t):
        hi, lo = _split2(t)
        m = _dot(rcat(hi, lo), eavg)
        return m[:chunk] + m[chunk:]

    mu = [headmean(Y[u]) for u in nu]
    dv = [Y[u] - mu[u] for u in nu]
    var = [headmean(dv[u] * dv[u]) for u in nu]
    for u, (ci, j) in enumerate(units):
        yn = dv[u] * lax.rsqrt(var[u] + LNX_EPS) * lg_ref[:, lanes(j)] + lb_ref[:, lanes(j)]
        y_ref[rows(ci), lanes(j)] = (yn + bonus_ref[rows(ci), lanes(j)]) * g_ref[rows(ci), lanes(j)]

    @pl.when(c == nsteps - 1)
    def _():
        hout_ref[...] = h_ref[...]


def _wkv(rt, kt, bt, at, v, g, bonus, wc, h0, w, *, nstreams, ncs, chunk):
    total = rt.shape[0]
    group = min(WKV_UNITS, ncs)
    nsub = min(WKV_UNITS // group, nstreams)
    nsteps = ncs // group
    blk = pl.BlockSpec((nsub * group * chunk, D_A), lambda s, c: (s * nsteps + c, 0))
    hspec = pl.BlockSpec((nsub, A_HEADS // 2, LANES, LANES), lambda s, c: (s, 0, 0, 0))
    return pl.pallas_call(
        functools.partial(_wkv_kernel, chunk=chunk, group=group, nsub=nsub, nsteps=nsteps),
        out_shape=[jax.ShapeDtypeStruct((total, D_A), F32),
                   jax.ShapeDtypeStruct((nstreams, A_HEADS // 2, LANES, LANES), F32)],
        grid=(nstreams // nsub, nsteps),
        in_specs=[blk] * 7 + [pl.BlockSpec((nsub * group * 8, D_A), lambda s, c: (s * nsteps + c, 0)),
                              _full((1, D_A)), _full((1, D_A)), _full((LANES, LANES)), hspec],
        out_specs=[blk, hspec],
        scratch_shapes=[pltpu.VMEM((nsub, A_HEADS // 2, LANES, LANES), F32)],
        compiler_params=pltpu.CompilerParams(
            dimension_semantics=("arbitrary", "arbitrary"), vmem_limit_bytes=VMEM_LIMIT),
        name="wkv",
    )(rt, kt, bt, at, v, g, bonus, wc, w["lnx_g"], w["lnx_b"], w["eavg"], h0)


def _layer_norm(t, g, b):
    mu = jnp.mean(t, axis=-1, keepdims=True)
    d = t - mu
    var = jnp.mean(d * d, axis=-1, keepdims=True)
    return d * lax.rsqrt(var + LN_EPS) * g + b


def _tail_kernel(x_ref, ya_ref, yb_ref, wg_ref, bg_ref, wpa_ref, wpb_ref, wo_ref, l1g_ref, l1b_ref,
                 wgu_ref, wdown_ref, l2g_ref, l2b_ref, o_ref):
    x = x_ref[...]
    gates = _sigmoid(_dot(_bf(x), wg_ref[...]) + bg_ref[...])
    m = (gates[:, :D_MODEL] * _dot(_bf(ya_ref[...]), wpa_ref[...])
         + gates[:, D_MODEL:] * _dot(_bf(yb_ref[...]), wpb_ref[...]))
    h = _layer_norm(DN_ALPHA * x + _dot(_bf(m), wo_ref[...]), l1g_ref[...], l1b_ref[...])
    hb = _bf(h)

    def gate_up(c):
        cols = slice(c * FF_CHUNK, (c + 1) * FF_CHUNK)
        ucols = slice(D_FF + c * FF_CHUNK, D_FF + (c + 1) * FF_CHUNK)
        return _dot(hb, wgu_ref[:, cols]), _dot(hb, wgu_ref[:, ucols])

    f = None
    nxt = gate_up(0)
    for c in range(N_FF):
        gate, up = nxt
        if c + 1 < N_FF:
            nxt = gate_up(c + 1)
        d = _dot(_bf(gate * _sigmoid(gate) * up), wdown_ref[c * FF_CHUNK:(c + 1) * FF_CHUNK, :])
        f = d if f is None else f + d
    o_ref[...] = _layer_norm(DN_ALPHA * h + f, l2g_ref[...], l2b_ref[...])


def _tail(x2, ya, yb, w, *, rows):
    total = x2.shape[0]
    rowblk = lambda n: pl.BlockSpec((rows, n), lambda i: (i, 0))

    def const(shape):
        n = len(shape)
        return pl.BlockSpec(shape, lambda i: (0,) * n, pipeline_mode=pl.Buffered(1))

    return pl.pallas_call(
        _tail_kernel,
        out_shape=jax.ShapeDtypeStruct((total, D_MODEL), F32),
        grid=(total // rows,),
        in_specs=[rowblk(D_MODEL), rowblk(D_A), rowblk(D_B),
                  const((D_MODEL, 2 * D_MODEL)), const((1, 2 * D_MODEL)),
                  const((D_A, D_MODEL)), const((D_B, D_MODEL)), const((D_MODEL, D_MODEL)),
                  const((1, D_MODEL)), const((1, D_MODEL)),
                  const((D_MODEL, 2 * D_FF)), const((D_FF, D_MODEL)),
                  const((1, D_MODEL)), const((1, D_MODEL))],
        out_specs=rowblk(D_MODEL),
        compiler_params=pltpu.CompilerParams(
            dimension_semantics=("arbitrary",), vmem_limit_bytes=VMEM_LIMIT),
        name="tail",
    )(x2, ya, yb, w["wg"], w["bg"], w["wpa"], w["wpb"], w["wo"], w["l1g"], w["l1b"],
      w["wgu"], w["wdown"], w["l2g"], w["l2b"])


def _prep_weights(w_in, mu_shift, w0, w_w2, a0, w_a2, w_g2, k_k, k_a, r_k, lnx_g, lnx_b, w_pa,
                  q_norm_g, w_uq, kv_norm_g, w_ukv, w_pb, b_gate, w_o, ln1_g, ln1_b, w_gu, w_down,
                  ln2_g, ln2_b):
    row = lambda t: t.reshape(1, -1).astype(F32)
    nb = A_COLS + B_COLS
    pe = w_in[:, nb - ROPE_DIM:nb]
    half = ROPE_DIM // 2
    pe_sw = jnp.concatenate([pe[:, half:], pe[:, :half]], axis=1)
    wall = jnp.concatenate([w_in[:, :nb - ROPE_DIM], pe, pe_sw,
                            jnp.zeros((D_MODEL, LANES - 2 * ROPE_DIM), F32)], axis=1)
    uq = w_uq.reshape(Q_LORA, B_HEADS, NOPE_DIM + ROPE_DIM)
    nope, r1, r2 = uq[..., :NOPE_DIM], uq[..., NOPE_DIM:NOPE_DIM + half], uq[..., NOPE_DIM + half:]
    zpad = jnp.zeros((Q_LORA, B_HEADS, QK_PAD - NOPE_DIM - ROPE_DIM), F32)
    wqa = jnp.concatenate([nope, r1, r2, zpad], axis=-1).reshape(Q_LORA, B_HEADS * QK_PAD)
    wqb = jnp.concatenate([jnp.zeros_like(nope), r2, r1, zpad], axis=-1).reshape(Q_LORA, B_HEADS * QK_PAD)
    ukv = w_ukv.reshape(KV_LORA, B_HEADS, NOPE_DIM + V_DIM)
    wk = jnp.concatenate([ukv[..., :NOPE_DIM], jnp.zeros((KV_LORA, B_HEADS, QK_PAD - NOPE_DIM), F32)],
                         axis=-1).reshape(KV_LORA, B_HEADS * QK_PAD)
    wv = ukv[..., NOPE_DIM:].reshape(KV_LORA, D_B)
    place = np.zeros((ROPE_DIM, B_HEADS * QK_PAD), np.float32)
    for h in range(B_HEADS):
        place[np.arange(ROPE_DIM), h * QK_PAD + NOPE_DIM + np.arange(ROPE_DIM)] = 1.0
    hid = np.arange(D_A) // A_HEAD_DIM
    esum = (hid[:, None] == hid[None, :]).astype(np.float32)
    lid = np.arange(LANES) // A_HEAD_DIM
    eavg = (lid[:, None] == lid[None, :]).astype(np.float32) / A_HEAD_DIM
    zl = jnp.zeros((LANES - DECAY_LORA, D_A), F32)
    return {
        "wall": _bf(wall), "mu": row(mu_shift), "w0": row(w0), "a0": row(a0), "k_k": row(k_k),
        "k_a": row(k_a), "r_k": row(r_k),
        "ww2": _bf(jnp.concatenate([w_w2, zl], axis=0)), "wa2": _bf(jnp.concatenate([zl, w_a2], axis=0)),
        "wg2": _bf(w_g2), "esum": jnp.asarray(esum, BF16), "eavg": jnp.asarray(eavg, BF16),
        "qg": row(q_norm_g), "kvg": row(kv_norm_g), "wqa": _bf(wqa), "wqb": _bf(wqb),
        "wk": _bf(wk), "wv": _bf(wv), "place": jnp.asarray(place, BF16),
        "lnx_g": row(lnx_g), "lnx_b": row(lnx_b),
        "wg": _bf(w_in[:, nb:]), "bg": row(b_gate), "wpa": _bf(w_pa), "wpb": _bf(w_pb), "wo": _bf(w_o),
        "l1g": row(ln1_g), "l1b": row(ln1_b), "l2g": row(ln2_g), "l2b": row(ln2_b),
        "wgu": _bf(w_gu), "wdown": _bf(w_down),
    }


def _rope_table(pos0, t, reps):
    half = ROPE_DIM // 2
    inv = ROPE_BASE ** (-jnp.arange(half, dtype=F32) / half)
    ang = (pos0 + jnp.arange(t)).astype(F32)[:, None] * jnp.tile(inv, LANES // half)[None, :]
    grp = np.arange(LANES) // half
    sc = np.where(grp >= 4, SCORE_SCALE, 1.0)
    mc = jnp.asarray(np.where(grp % 4 < 2, sc, 0.0), F32)
    ms = jnp.asarray(np.where(grp % 4 == 2, -sc, np.where(grp % 4 == 3, sc, 0.0)), F32)
    return jnp.tile(jnp.cos(ang) * mc + jnp.sin(ang) * ms, (reps, 1))


def _state_to_pairs(s):
    nb = s.shape[0]
    st = jnp.swapaxes(s, -1, -2).reshape(nb, A_HEADS // 2, 2, A_HEAD_DIM, A_HEAD_DIM)
    z = jnp.zeros_like(st[:, :, 0])
    top = jnp.concatenate([st[:, :, 0], z], axis=-1)
    bot = jnp.concatenate([z, st[:, :, 1]], axis=-1)
    return jnp.concatenate([top, bot], axis=-2)


def _pairs_to_state(hp):
    nb = hp.shape[0]
    diag = jnp.stack([hp[:, :, :A_HEAD_DIM, :A_HEAD_DIM], hp[:, :, A_HEAD_DIM:, A_HEAD_DIM:]], axis=2)
    return jnp.swapaxes(diag.reshape(nb, A_HEADS, A_HEAD_DIM, A_HEAD_DIM), -1, -2)


def _layer(x, pos0, shift0, wkv0, cache, w):
    nstreams, t, _ = x.shape
    total = nstreams * t
    x2 = x.reshape(total, D_MODEL)
    chunk = min(CHUNK, t)
    rope = _rope_table(pos0, t, nstreams)
    (rt, kt, bt, at, v, g, bonus, wc, shift, q, ckv, kpe, *kv) = _proj(
        x2, shift0, w, rope, nstreams=nstreams, seg=t, rows=min(PROJ_ROWS, total), chunk=chunk,
        expand_kv=cache is None)

    ya, hout = _wkv(rt, kt, bt, at, v, g, bonus, wc, _state_to_pairs(wkv0), w,
                    nstreams=nstreams, ncs=t // chunk, chunk=chunk)

    if cache is None:
        kk, vv = kv
        yb = _attn(q.reshape(nstreams, t, -1), kk.reshape(nstreams, t, -1), vv.reshape(nstreams, t, -1),
                   blk=min(ATTN_BLOCK, t))
    else:
        yb = _attnc(q.reshape(nstreams, t, -1), cache[0], cache[1], ckv, kpe, w)

    y = _tail(x2, ya, yb.reshape(total, D_B), w, rows=min(TAIL_ROWS, total))
    return (y.reshape(nstreams, t, D_MODEL), ckv.reshape(nstreams, t, KV_LORA),
            kpe.reshape(nstreams, t, ROPE_DIM), _pairs_to_state(hout), shift)


def kernel(x_prompt, x_sample, cache_ckv, cache_kpe, state_wkv, state_shift, w_in, mu_shift, w0, w_w2, a0,
           w_a2, w_g2, k_k, k_a, r_k, lnx_g, lnx_b, w_pa, q_norm_g, w_uq, kv_norm_g, w_ukv, w_pb, b_gate,
           w_o, ln1_g, ln1_b, w_gu, w_down, ln2_g, ln2_b):
    w = _prep_weights(w_in, mu_shift, w0, w_w2, a0, w_a2, w_g2, k_k, k_a, r_k, lnx_g, lnx_b, w_pa,
                      q_norm_g, w_uq, kv_norm_g, w_ukv, w_pb, b_gate, w_o, ln1_g, ln1_b, w_gu, w_down,
                      ln2_g, ln2_b)
    bp = x_prompt.shape[0]
    y_p, ckv_p, kpe_p, wkv_p, shift_p = _layer(
        x_prompt, 0, jnp.zeros((bp, 1, A_COLS), F32),
        jnp.zeros((bp, A_HEADS, A_HEAD_DIM, A_HEAD_DIM), F32), None, w)
    y_s, ckv_s, kpe_s, wkv_s, shift_s = _layer(
        x_sample, cache_ckv.shape[1], state_shift, state_wkv, (cache_ckv, cache_kpe), w)
    return (y_p, y_s, ckv_p, kpe_p, wkv_p, shift_p, ckv_s, kpe_s, wkv_s, shift_s)
```

```python
import functools

import numpy as np
import jax
import jax.numpy as jnp
from jax import lax
from jax.experimental import pallas as pl
from jax.experimental.pallas import tpu as pltpu

D_MODEL = 1024
CHUNK = 64
A_HEADS = 8
A_HEAD_DIM = 64
D_A = 512
DECAY_LORA = 64
AAA_LORA = 64
GATE_LORA = 128
A_COLS = 3 * D_A + DECAY_LORA + AAA_LORA + GATE_LORA
LNX_EPS = A_HEAD_DIM * 1e-5
B_HEADS = 8
Q_LORA = 256
KV_LORA = 128
NOPE_DIM = 64
ROPE_DIM = 32
V_DIM = 64
D_B = 512
B_COLS = Q_LORA + KV_LORA + ROPE_DIM
ROPE_BASE = 10000.0
ATTN_SCALE = (NOPE_DIM + ROPE_DIM) ** -0.5
SCORE_SCALE = ATTN_SCALE * float(np.log2(np.e))
RMS_EPS = 1e-6
D_FF = 2816
LN_EPS = 1e-5
DN_ALPHA = 2.0 ** 0.25

LANES = 128
SUBLANES = 8
QK_PAD = 128
PROJ_COLS = A_COLS + Q_LORA + KV_LORA + LANES
FF_CHUNK = 256
N_FF = D_FF // FF_CHUNK
TRI_ROWS = 256
PROJ_ROWS = 512
TAIL_ROWS = 512
ATTN_BLOCK = 1024
WKV_UNITS = 4
VMEM_LIMIT = 56 * 1024 * 1024

F32 = jnp.float32
BF16 = jnp.bfloat16


def _dot(a, b):
    return jnp.dot(a, b, preferred_element_type=F32)


def _dot_nt(a, b):
    return lax.dot_general(a, b, (((1,), (1,)), ((), ())), preferred_element_type=F32)


def _bf(x):
    return x.astype(BF16)


def _split2(x):
    hi = x.astype(BF16)
    lo = (x - hi.astype(F32)).astype(BF16)
    return hi, lo


def _split3(x):
    h1 = x.astype(BF16)
    r1 = x - h1.astype(F32)
    h2 = r1.astype(BF16)
    h3 = (r1 - h2.astype(F32)).astype(BF16)
    return h1, h2, h3


def _sigmoid(z):
    return 1.0 / (1.0 + jnp.exp(-z))


def _full(shape):
    n = len(shape)
    return pl.BlockSpec(shape, lambda *_: (0,) * n)


def _proj_kernel(x_ref, shift0_ref, wall_ref, mu_ref, w0_ref, a0_ref, kk_ref, ka_ref, rk_ref,
                 ww2_ref, wa2_ref, wg2_ref, esum_ref, ltri_ref, qg_ref, kvg_ref, wqa_ref, wqb_ref,
                 rope_ref, wk_ref, place_ref, wv_ref,
                 rt_ref, kt_ref, bt_ref, at_ref, v_ref, g_ref, bonus_ref, wc_ref, shift_ref,
                 q_ref, ckv_ref, kpe_ref, *rest, rows, chunk, seg):
    *kv_out, carry_ref = rest
    b = pl.program_id(1)
    xb = _bf(x_ref[...])

    pa = _dot(xb, wall_ref[:, :A_COLS])
    pq = _dot(xb, wall_ref[:, A_COLS:A_COLS + Q_LORA])
    pkv = _dot(xb, wall_ref[:, A_COLS + Q_LORA:A_COLS + Q_LORA + KV_LORA])
    ppe = _dot(xb, wall_ref[:, A_COLS + Q_LORA + KV_LORA:])

    cqn = _bf(pq * lax.rsqrt(jnp.mean(pq * pq, axis=-1, keepdims=True) + RMS_EPS) * qg_ref[...])
    qa = _dot(cqn, wqa_ref[...])
    qb = _dot(cqn, wqb_ref[...])
    rope = rope_ref[...]
    lane = lax.broadcasted_iota(jnp.int32, (1, QK_PAD), 1)
    cq = jnp.where(lane < NOPE_DIM, np.float32(SCORE_SCALE), rope)
    sq = pltpu.roll(rope, QK_PAD - ROPE_DIM, axis=1)
    for h in range(B_HEADS):
        sl = slice(h * QK_PAD, (h + 1) * QK_PAD)
        q_ref[:, sl] = _bf(qa[:, sl] * cq + qb[:, sl] * sq)
    ckv = pkv * lax.rsqrt(jnp.mean(pkv * pkv, axis=-1, keepdims=True) + RMS_EPS) * kvg_ref[...]
    ckv_ref[...] = ckv
    ppe = ppe * rope
    kpe = ppe[:, :ROPE_DIM] + ppe[:, ROPE_DIM:2 * ROPE_DIM]
    kpe_ref[...] = kpe
    if kv_out:
        kx_ref, vx_ref = kv_out
        cb = _bf(ckv)
        kx_ref[...] = _bf(_dot(cb, wk_ref[...]) + _dot(_bf(kpe), place_ref[...]))
        vx_ref[...] = _bf(_dot(cb, wv_ref[...]))

    row = lax.broadcasted_iota(jnp.int32, (rows, 1), 0)
    if seg >= rows:
        first = jnp.where(b == 0, shift0_ref[0], carry_ref[...])
        starts = row == 0
        last = pa[rows - 1:rows, :]
        carry_ref[...] = last
        shift_ref[0] = last
    else:
        first = jnp.broadcast_to(shift0_ref[...], (rows // seg, seg, A_COLS)).reshape(rows, A_COLS)
        starts = row % seg == 0
        for s in range(rows // seg):
            shift_ref[s] = pa[(s + 1) * seg - 1:(s + 1) * seg, :]
    prev = jnp.where(starts, first, pltpu.roll(pa, 1, axis=0))
    xs = pa + (prev - pa) * mu_ref[...]

    r = xs[:, :D_A]
    k = xs[:, D_A:2 * D_A]
    v = xs[:, 2 * D_A:3 * D_A]
    wa = xs[:, 3 * D_A:3 * D_A + LANES]
    gd = xs[:, 3 * D_A + LANES:]

    z = w0_ref[...] + _dot(_bf(jnp.tanh(wa)), ww2_ref[...])
    ld = -np.float32(np.exp(-0.5)) * _sigmoid(z)
    a = _sigmoid(a0_ref[...] + _dot(_bf(wa), wa2_ref[...]))
    g_ref[...] = _dot(_bf(_sigmoid(gd)), wg2_ref[...])

    kkr = k * kk_ref[...]
    kh = k * (1.0 + (a - 1.0) * ka_ref[...])
    hi, lo = _split2(kkr * kkr)
    hs = _dot(jnp.concatenate([hi, lo, _bf(r * kh * rk_ref[...])], axis=0), esum_ref[...])
    kk = kkr / jnp.maximum(jnp.sqrt(hs[:rows] + hs[rows:2 * rows]), 1e-12)
    bonus_ref[...] = hs[2 * rows:] * v
    v_ref[...] = v

    ltri = ltri_ref[...]
    tri = ltri.shape[0]
    h1, h2, h3 = _split3(ld)
    cum = jnp.concatenate(
        [_dot(ltri, h1[i:i + tri]) + _dot(ltri, h2[i:i + tri]) + _dot(ltri, h3[i:i + tri])
         for i in range(0, rows, tri)], axis=0)
    ep = jnp.exp(cum)
    em = jnp.exp(-cum)
    rt_ref[...] = r * ep
    kt_ref[...] = kh * em
    bt_ref[...] = (kk * a) * em
    at_ref[...] = -kk * jnp.exp(cum - ld)
    for c in range(rows // chunk):
        wc_ref[c * SUBLANES:(c + 1) * SUBLANES, :] = jnp.broadcast_to(
            ep[(c + 1) * chunk - 1:(c + 1) * chunk, :], (SUBLANES, D_A))


def _proj(x2, shift0, w, rope, *, nstreams, seg, rows, chunk, expand_kv):
    total = nstreams * seg
    spb = max(1, rows // seg)
    bps = max(1, seg // rows)
    nck = rows // chunk
    tri = min(rows, TRI_ROWS)
    ltri = _chunk_tri(tri, chunk)
    rowblk = lambda n: pl.BlockSpec((rows, n), lambda s, b: (s * bps + b, 0))
    in_specs = [
        rowblk(D_MODEL),
        pl.BlockSpec((spb, 1, A_COLS), lambda s, b: (s, 0, 0)),
        _full((D_MODEL, PROJ_COLS)),
        _full((1, A_COLS)), _full((1, D_A)), _full((1, D_A)), _full((1, D_A)), _full((1, D_A)), _full((1, D_A)),
        _full((LANES, D_A)), _full((LANES, D_A)), _full((GATE_LORA, D_A)),
        _full((D_A, D_A)), _full((tri, tri)),
        _full((1, Q_LORA)), _full((1, KV_LORA)),
        _full((Q_LORA, B_HEADS * QK_PAD)), _full((Q_LORA, B_HEADS * QK_PAD)),
        rowblk(LANES),
        _full((KV_LORA, B_HEADS * QK_PAD)), _full((ROPE_DIM, B_HEADS * QK_PAD)), _full((KV_LORA, D_B)),
    ]
    f32o = lambda n: jax.ShapeDtypeStruct((total, n), F32)
    out_shape = [f32o(D_A)] * 7 + [
        jax.ShapeDtypeStruct((total // chunk * SUBLANES, D_A), F32),
        jax.ShapeDtypeStruct((nstreams, 1, A_COLS), F32),
        jax.ShapeDtypeStruct((total, B_HEADS * QK_PAD), BF16),
        f32o(KV_LORA), f32o(ROPE_DIM),
    ]
    out_specs = [rowblk(D_A)] * 7 + [
        pl.BlockSpec((nck * SUBLANES, D_A), lambda s, b: (s * bps + b, 0)),
        pl.BlockSpec((spb, 1, A_COLS), lambda s, b: (s, 0, 0)),
        rowblk(B_HEADS * QK_PAD), rowblk(KV_LORA), rowblk(ROPE_DIM),
    ]
    if expand_kv:
        out_shape += [jax.ShapeDtypeStruct((total, B_HEADS * QK_PAD), BF16),
                      jax.ShapeDtypeStruct((total, D_B), BF16)]
        out_specs += [rowblk(B_HEADS * QK_PAD), rowblk(D_B)]
    return pl.pallas_call(
        functools.partial(_proj_kernel, rows=rows, chunk=chunk, seg=seg),
        out_shape=out_shape,
        grid=(nstreams // spb, bps),
        in_specs=in_specs,
        out_specs=out_specs,
        scratch_shapes=[pltpu.VMEM((1, A_COLS), F32)],
        compiler_params=pltpu.CompilerParams(
            dimension_semantics=("arbitrary", "arbitrary"), vmem_limit_bytes=VMEM_LIMIT),
        name="proj",
    )(x2, shift0, w["wall"], w["mu"], w["w0"], w["a0"], w["k_k"], w["k_a"], w["r_k"],
      w["ww2"], w["wa2"], w["wg2"], w["esum"], ltri, w["qg"], w["kvg"], w["wqa"], w["wqb"],
      rope, w["wk"], w["place"], w["wv"])


def _chunk_tri(rows, chunk):
    i = np.arange(rows)
    m = (i[:, None] // chunk == i[None, :] // chunk) & (i[None, :] <= i[:, None])
    return jnp.asarray(m, BF16)


def _attn_kernel(qi_ref, ki_ref, q_ref, k_ref, v_ref, o_ref, m_ref, acc_ref, *, blk):
    s_id = pl.program_id(1)
    qi = qi_ref[s_id]
    ki = ki_ref[s_id]

    @pl.when(ki == 0)
    def _():
        m_ref[...] = jnp.full(m_ref.shape, -jnp.inf, F32)
        acc_ref[...] = jnp.zeros(acc_ref.shape, F32)

    low = lax.broadcasted_iota(jnp.int32, (1, LANES), 1) < V_DIM

    def step(r0, nr, nc, masked):
        rs = slice(r0, r0 + nr)

        def scores(h):
            sl = slice(h * QK_PAD, (h + 1) * QK_PAD)
            return _dot_nt(q_ref[0, rs, sl], k_ref[0, :nc, sl])

        if masked:
            rq = (lax.broadcasted_iota(jnp.int32, (nr, nc), 0) + r0) // CHUNK
            mask = lax.broadcasted_iota(jnp.int32, (nr, nc), 1) // CHUNK <= rq
        one = jnp.ones((), BF16)
        ahead = 2
        pending = [scores(h) for h in range(ahead)]
        for h in range(B_HEADS):
            s = pending.pop(0)
            if h + ahead < B_HEADS:
                pending.append(scores(h + ahead))
            vp = v_ref[0, :nc, (h // 2) * LANES:(h // 2 + 1) * LANES]
            vext = jnp.where(low, vp, one) if h % 2 == 0 else jnp.where(low, one, vp)
            if masked:
                s = jnp.where(mask, s, -jnp.inf)
            m_prev = m_ref[h, rs]
            m_new = jnp.maximum(m_prev, jnp.max(s, axis=-1, keepdims=True))
            p = jnp.exp2(s - m_new[:, :1])
            acc_ref[h, rs] = jnp.exp2(m_prev - m_new) * acc_ref[h, rs] + _dot(_bf(p), vext)
            m_ref[h, rs] = m_new

    @pl.when(ki < qi)
    def _():
        step(0, blk, blk, False)

    @pl.when(ki == qi)
    def _():
        half = blk // 2
        step(0, half, half, True)
        step(half, half, blk, True)
        for j in range(B_HEADS // 2):
            a0 = acc_ref[2 * j]
            a1 = acc_ref[2 * j + 1]
            num = jnp.where(low, a0, a1)
            den = jnp.where(low, pltpu.roll(a0, V_DIM, axis=1), pltpu.roll(a1, V_DIM, axis=1))
            o_ref[0, :, j * LANES:(j + 1) * LANES] = num / den


def _attn(q, k, v, *, blk):
    nb, t, _ = q.shape
    steps = [(i, j) for i in range(t // blk) for j in range(i + 1)]
    qi = jnp.asarray([s[0] for s in steps], jnp.int32)
    ki = jnp.asarray([s[1] for s in steps], jnp.int32)
    grid_spec = pltpu.PrefetchScalarGridSpec(
        num_scalar_prefetch=2,
        grid=(nb, len(steps)),
        in_specs=[pl.BlockSpec((1, blk, B_HEADS * QK_PAD), lambda b, s, qi, ki: (b, qi[s], 0)),
                  pl.BlockSpec((1, blk, B_HEADS * QK_PAD), lambda b, s, qi, ki: (b, ki[s], 0)),
                  pl.BlockSpec((1, blk, D_B), lambda b, s, qi, ki: (b, ki[s], 0))],
        out_specs=pl.BlockSpec((1, blk, D_B), lambda b, s, qi, ki: (b, qi[s], 0)),
        scratch_shapes=[pltpu.VMEM((B_HEADS, blk, LANES), F32),
                        pltpu.VMEM((B_HEADS, blk, LANES), F32)])
    return pl.pallas_call(
        functools.partial(_attn_kernel, blk=blk),
        out_shape=jax.ShapeDtypeStruct((nb, t, D_B), F32),
        grid_spec=grid_spec,
        compiler_params=pltpu.CompilerParams(
            dimension_semantics=("arbitrary", "arbitrary"), vmem_limit_bytes=VMEM_LIMIT),
        name="attn",
    )(qi, ki, q, k, v)


def _attnc_kernel(q_ref, cckv_ref, ckpe_ref, nckv_ref, nkpe_ref, wk_ref, wv_ref, place_ref, o_ref, *, t):
    q = q_ref[0]
    heads = [q[:, h * QK_PAD:(h + 1) * QK_PAD] for h in range(B_HEADS)]
    qf = jnp.concatenate(heads, axis=0)
    qa = jnp.concatenate([_bf(_dot_nt(heads[h], wk_ref[:, h * QK_PAD:(h + 1) * QK_PAD]))
                          for h in range(B_HEADS)], axis=0)
    place = place_ref[:, :QK_PAD]

    def scores(ckv, kpe):
        cb = _bf(ckv)
        return _dot_nt(qa, cb) + _dot_nt(qf, _bf(_dot(_bf(kpe), place))), cb

    s_c, cb_c = scores(cckv_ref[0], ckpe_ref[0])
    s_n, cb_n = scores(nckv_ref[...], nkpe_ref[...])
    m = jnp.maximum(jnp.max(s_c, axis=-1, keepdims=True), jnp.max(s_n, axis=-1, keepdims=True))
    p_c = jnp.exp2(s_c - m)
    p_n = jnp.exp2(s_n - m)
    den = jnp.sum(p_c, axis=-1, keepdims=True) + jnp.sum(p_n, axis=-1, keepdims=True)
    lat = _bf((_dot(_bf(p_c), cb_c) + _dot(_bf(p_n), cb_n)) / den)
    low = lax.broadcasted_iota(jnp.int32, (1, LANES), 1) < V_DIM
    zero = jnp.zeros((), BF16)
    for j in range(B_HEADS // 2):
        wvp = wv_ref[:, j * LANES:(j + 1) * LANES]
        o_ref[0, :, j * LANES:(j + 1) * LANES] = (
            _dot(lat[2 * j * t:(2 * j + 1) * t], jnp.where(low, wvp, zero))
            + _dot(lat[(2 * j + 1) * t:(2 * j + 2) * t], jnp.where(low, zero, wvp)))


def _attnc(q, cache_ckv, cache_kpe, ckv, kpe, w):
    nb, t, _ = q.shape
    past = cache_ckv.shape[1]
    return pl.pallas_call(
        functools.partial(_attnc_kernel, t=t),
        out_shape=jax.ShapeDtypeStruct((nb, t, D_B), F32),
        grid=(nb,),
        in_specs=[pl.BlockSpec((1, t, B_HEADS * QK_PAD), lambda b: (b, 0, 0)),
                  pl.BlockSpec((1, past, KV_LORA), lambda b: (b, 0, 0)),
                  pl.BlockSpec((1, past, ROPE_DIM), lambda b: (b, 0, 0)),
                  pl.BlockSpec((t, KV_LORA), lambda b: (b, 0)),
                  pl.BlockSpec((t, ROPE_DIM), lambda b: (b, 0)),
                  _full((KV_LORA, B_HEADS * QK_PAD)), _full((KV_LORA, D_B)),
                  _full((ROPE_DIM, B_HEADS * QK_PAD))],
        out_specs=pl.BlockSpec((1, t, D_B), lambda b: (b, 0, 0)),
        compiler_params=pltpu.CompilerParams(dimension_semantics=("arbitrary",)),
        name="attnc",
    )(q, cache_ckv, cache_kpe, ckv, kpe, w["wk"], w["wv"], w["place"])


def _wkv_kernel(rt_ref, kt_ref, bt_ref, at_ref, v_ref, g_ref, bonus_ref, wc_ref, lg_ref, lb_ref,
                eavg_ref, h0_ref, y_ref, hout_ref, h_ref, *, chunk, group, nsub, nsteps):
    c = pl.program_id(1)
    C2 = 2 * chunk
    npair = A_HEADS // 2

    @pl.when(c == 0)
    def _():
        h_ref[...] = h0_ref[...]

    low = lax.broadcasted_iota(jnp.int32, (chunk, LANES), 1) < A_HEAD_DIM
    ii = lax.broadcasted_iota(jnp.int32, (C2, C2), 0)
    jj = lax.broadcasted_iota(jnp.int32, (C2, C2), 1)
    strict = ii > jj
    incl = ii >= jj
    eye_c = (ii == jj).astype(F32)
    ki = lax.broadcasted_iota(jnp.int32, (LANES, LANES), 0)
    kj = lax.broadcasted_iota(jnp.int32, (LANES, LANES), 1)
    eye_k = (ki == kj).astype(F32)
    eavg = eavg_ref[...]

    def stack(t):
        return jnp.concatenate([jnp.where(low, t, 0.0), jnp.where(low, 0.0, t)], axis=0)

    units = [(ci, j) for ci in range(nsub * group) for j in range(npair)]
    rows = lambda ci: slice(ci * chunk, (ci + 1) * chunk)
    lanes = lambda j: slice(j * LANES, (j + 1) * LANES)
    ld = lambda ref: [stack(ref[rows(ci), lanes(j)]) for ci, j in units]
    At, Bt, Kt, Rt, Vs = ld(at_ref), ld(bt_ref), ld(kt_ref), ld(rt_ref), ld(v_ref)
    nu = range(len(units))
    Vb = [_bf(Vs[u]) for u in nu]
    g1 = [_dot_nt(_bf(jnp.concatenate([At[u], Rt[u]], axis=0)),
                  _bf(jnp.concatenate([Bt[u], Kt[u]], axis=0))) for u in nu]
    Aab = [jnp.where(strict, g1[u][:C2, :C2], 0.0) for u in nu]
    Aak = [_bf(jnp.where(strict, g1[u][:C2, C2:], 0.0)) for u in nu]
    Arb = [_bf(jnp.where(incl, g1[u][C2:, :C2], 0.0)) for u in nu]
    Ark = [_bf(jnp.where(incl, g1[u][C2:, C2:], 0.0)) for u in nu]
    rcat = lambda *t: jnp.concatenate(t, axis=0)
    Tm = [eye_c + Aab[u] for u in nu]
    Nb = [_bf(Aab[u]) for u in nu]
    Pw = [_dot(Nb[u], Nb[u]) for u in nu]
    for _ in range(chunk.bit_length() - 3):
        Pb = [_bf(Pw[u]) for u in nu]
        st = [_dot(rcat(_bf(Tm[u]), Pb[u]), Pb[u]) for u in nu]
        Tm = [Tm[u] + st[u][:C2] for u in nu]
        Pw = [st[u][C2:] for u in nu]
    Tm = [Tm[u] + _dot(_bf(Tm[u]), _bf(Pw[u])) for u in nu]
    wrow = [wc_ref[ci * SUBLANES:ci * SUBLANES + 1, lanes(j)] for ci, j in units]
    BwT = [_bf((Bt[u] * wrow[u]).T) for u in nu]
    KwT = [_bf((Kt[u] * wrow[u]).T) for u in nu]
    sv = [_dot(rcat(Aak[u], Ark[u], KwT[u]), Vb[u]) for u in nu]
    PPb = [_bf(_dot(_bf(Tm[u]), _bf(jnp.concatenate([At[u], sv[u][:C2]], axis=1)))) for u in nu]
    sp = [_dot(rcat(Arb[u], BwT[u]), PPb[u]) for u in nu]
    Q1M1, Q2, M2 = [], [], []
    for u in nu:
        q1s = Rt[u] + sp[u][:C2, :LANES]
        q2s = sp[u][:C2, LANES:] + sv[u][C2:2 * C2]
        m1 = eye_k * wrow[u] + sp[u][C2:, :LANES]
        Q1M1.append(_bf(rcat(q1s[:chunk] + q1s[chunk:], m1)))
        Q2.append(q2s[:chunk] + q2s[chunk:])
        M2.append(sp[u][C2:, LANES:] + sv[u][2 * C2:])

    H = {(si, j): h_ref[si, j] for si in range(nsub) for j in range(npair)}
    Y = []
    for u, (ci, j) in enumerate(units):
        key = (ci // group, j)
        sh = _dot(Q1M1[u], _bf(H[key]))
        Y.append(sh[:chunk] + Q2[u])
        H[key] = sh[chunk:] + M2[u]
    for (si, j), val in H.items():
        h_ref[si, j] = val

    def headmean(t):
        hi, lo = _split2(t)
        m = _dot(rcat(hi, lo), eavg)
        return m[:chunk] + m[chunk:]

    mu = [headmean(Y[u]) for u in nu]
    dv = [Y[u] - mu[u] for u in nu]
    var = [headmean(dv[u] * dv[u]) for u in nu]
    for u, (ci, j) in enumerate(units):
        yn = dv[u] * lax.rsqrt(var[u] + LNX_EPS) * lg_ref[:, lanes(j)] + lb_ref[:, lanes(j)]
        y_ref[rows(ci), lanes(j)] = (yn + bonus_ref[rows(ci), lanes(j)]) * g_ref[rows(ci), lanes(j)]

    @pl.when(c == nsteps - 1)
    def _():
        hout_ref[...] = h_ref[...]


def _wkv(rt, kt, bt, at, v, g, bonus, wc, h0, w, *, nstreams, ncs, chunk):
    total = rt.shape[0]
    group = min(WKV_UNITS, ncs)
    nsub = min(WKV_UNITS // group, nstreams)
    nsteps = ncs // group
    blk = pl.BlockSpec((nsub * group * chunk, D_A), lambda s, c: (s * nsteps + c, 0))
    hspec = pl.BlockSpec((nsub, A_HEADS // 2, LANES, LANES), lambda s, c: (s, 0, 0, 0))
    return pl.pallas_call(
        functools.partial(_wkv_kernel, chunk=chunk, group=group, nsub=nsub, nsteps=nsteps),
        out_shape=[jax.ShapeDtypeStruct((total, D_A), F32),
                   jax.ShapeDtypeStruct((nstreams, A_HEADS // 2, LANES, LANES), F32)],
        grid=(nstreams // nsub, nsteps),
        in_specs=[blk] * 7 + [pl.BlockSpec((nsub * group * SUBLANES, D_A), lambda s, c: (s * nsteps + c, 0)),
                              _full((1, D_A)), _full((1, D_A)), _full((LANES, LANES)), hspec],
        out_specs=[blk, hspec],
        scratch_shapes=[pltpu.VMEM((nsub, A_HEADS // 2, LANES, LANES), F32)],
        compiler_params=pltpu.CompilerParams(
            dimension_semantics=("arbitrary", "arbitrary"), vmem_limit_bytes=VMEM_LIMIT),
        name="wkv",
    )(rt, kt, bt, at, v, g, bonus, wc, w["lnx_g"], w["lnx_b"], w["eavg"], h0)


def _layer_norm(t, g, b):
    mu = jnp.mean(t, axis=-1, keepdims=True)
    d = t - mu
    var = jnp.mean(d * d, axis=-1, keepdims=True)
    return d * lax.rsqrt(var + LN_EPS) * g + b


def _tail_kernel(x_ref, ya_ref, yb_ref, wg_ref, bg_ref, wpa_ref, wpb_ref, wo_ref, l1g_ref, l1b_ref,
                 wgu_ref, wdown_ref, l2g_ref, l2b_ref, o_ref):
    x = x_ref[...]
    gates = _sigmoid(_dot(_bf(x), wg_ref[...]) + bg_ref[...])
    m = (gates[:, :D_MODEL] * _dot(_bf(ya_ref[...]), wpa_ref[...])
         + gates[:, D_MODEL:] * _dot(_bf(yb_ref[...]), wpb_ref[...]))
    h = _layer_norm(DN_ALPHA * x + _dot(_bf(m), wo_ref[...]), l1g_ref[...], l1b_ref[...])
    hb = _bf(h)

    def gate_up(c):
        cols = slice(c * FF_CHUNK, (c + 1) * FF_CHUNK)
        ucols = slice(D_FF + c * FF_CHUNK, D_FF + (c + 1) * FF_CHUNK)
        return _dot(hb, wgu_ref[:, cols]), _dot(hb, wgu_ref[:, ucols])

    f = None
    nxt = gate_up(0)
    for c in range(N_FF):
        gate, up = nxt
        if c + 1 < N_FF:
            nxt = gate_up(c + 1)
        d = _dot(_bf(gate * _sigmoid(gate) * up), wdown_ref[c * FF_CHUNK:(c + 1) * FF_CHUNK, :])
        f = d if f is None else f + d
    o_ref[...] = _layer_norm(DN_ALPHA * h + f, l2g_ref[...], l2b_ref[...])


def _tail(x2, ya, yb, w, *, rows):
    total = x2.shape[0]
    rowblk = lambda n: pl.BlockSpec((rows, n), lambda i: (i, 0))

    def const(shape):
        n = len(shape)
        return pl.BlockSpec(shape, lambda i: (0,) * n, pipeline_mode=pl.Buffered(1))

    return pl.pallas_call(
        _tail_kernel,
        out_shape=jax.ShapeDtypeStruct((total, D_MODEL), F32),
        grid=(total // rows,),
        in_specs=[rowblk(D_MODEL), rowblk(D_A), rowblk(D_B),
                  const((D_MODEL, 2 * D_MODEL)), const((1, 2 * D_MODEL)),
                  const((D_A, D_MODEL)), const((D_B, D_MODEL)), const((D_MODEL, D_MODEL)),
                  const((1, D_MODEL)), const((1, D_MODEL)),
                  const((D_MODEL, 2 * D_FF)), const((D_FF, D_MODEL)),
                  const((1, D_MODEL)), const((1, D_MODEL))],
        out_specs=rowblk(D_MODEL),
        compiler_params=pltpu.CompilerParams(
            dimension_semantics=("arbitrary",), vmem_limit_bytes=VMEM_LIMIT),
        name="tail",
    )(x2, ya, yb, w["wg"], w["bg"], w["wpa"], w["wpb"], w["wo"], w["l1g"], w["l1b"],
      w["wgu"], w["wdown"], w["l2g"], w["l2b"])


def _prep_weights(w_in, mu_shift, w0, w_w2, a0, w_a2, w_g2, k_k, k_a, r_k, lnx_g, lnx_b, w_pa,
                  q_norm_g, w_uq, kv_norm_g, w_ukv, w_pb, b_gate, w_o, ln1_g, ln1_b, w_gu, w_down,
                  ln2_g, ln2_b):
    row = lambda t: t.reshape(1, -1).astype(F32)
    nb = A_COLS + B_COLS
    pe = w_in[:, nb - ROPE_DIM:nb]
    half = ROPE_DIM // 2
    pe_sw = jnp.concatenate([pe[:, half:], pe[:, :half]], axis=1)
    wall = jnp.concatenate([w_in[:, :nb - ROPE_DIM], pe, pe_sw,
                            jnp.zeros((D_MODEL, LANES - 2 * ROPE_DIM), F32)], axis=1)
    uq = w_uq.reshape(Q_LORA, B_HEADS, NOPE_DIM + ROPE_DIM)
    nope, r1, r2 = uq[..., :NOPE_DIM], uq[..., NOPE_DIM:NOPE_DIM + half], uq[..., NOPE_DIM + half:]
    zpad = jnp.zeros((Q_LORA, B_HEADS, QK_PAD - NOPE_DIM - ROPE_DIM), F32)
    wqa = jnp.concatenate([nope, r1, r2, zpad], axis=-1).reshape(Q_LORA, B_HEADS * QK_PAD)
    wqb = jnp.concatenate([jnp.zeros_like(nope), r2, r1, zpad], axis=-1).reshape(Q_LORA, B_HEADS * QK_PAD)
    ukv = w_ukv.reshape(KV_LORA, B_HEADS, NOPE_DIM + V_DIM)
    wk = jnp.concatenate([ukv[..., :NOPE_DIM], jnp.zeros((KV_LORA, B_HEADS, QK_PAD - NOPE_DIM), F32)],
                         axis=-1).reshape(KV_LORA, B_HEADS * QK_PAD)
    wv = ukv[..., NOPE_DIM:].reshape(KV_LORA, D_B)
    place = np.zeros((ROPE_DIM, B_HEADS * QK_PAD), np.float32)
    for h in range(B_HEADS):
        place[np.arange(ROPE_DIM), h * QK_PAD + NOPE_DIM + np.arange(ROPE_DIM)] = 1.0
    hid = np.arange(D_A) // A_HEAD_DIM
    esum = (hid[:, None] == hid[None, :]).astype(np.float32)
    lid = np.arange(LANES) // A_HEAD_DIM
    eavg = (lid[:, None] == lid[None, :]).astype(np.float32) / A_HEAD_DIM
    zl = jnp.zeros((LANES - DECAY_LORA, D_A), F32)
    return {
        "wall": _bf(wall), "mu": row(mu_shift), "w0": row(w0), "a0": row(a0), "k_k": row(k_k),
        "k_a": row(k_a), "r_k": row(r_k),
        "ww2": _bf(jnp.concatenate([w_w2, zl], axis=0)), "wa2": _bf(jnp.concatenate([zl, w_a2], axis=0)),
        "wg2": _bf(w_g2), "esum": jnp.asarray(esum, BF16), "eavg": jnp.asarray(eavg, BF16),
        "qg": row(q_norm_g), "kvg": row(kv_norm_g), "wqa": _bf(wqa), "wqb": _bf(wqb),
        "wk": _bf(wk), "wv": _bf(wv), "place": jnp.asarray(place, BF16),
        "lnx_g": row(lnx_g), "lnx_b": row(lnx_b),
        "wg": _bf(w_in[:, nb:]), "bg": row(b_gate), "wpa": _bf(w_pa), "wpb": _bf(w_pb), "wo": _bf(w_o),
        "l1g": row(ln1_g), "l1b": row(ln1_b), "l2g": row(ln2_g), "l2b": row(ln2_b),
        "wgu": _bf(w_gu), "wdown": _bf(w_down),
    }


def _rope_table(pos0, t, reps):
    half = ROPE_DIM // 2
    inv = ROPE_BASE ** (-jnp.arange(half, dtype=F32) / half)
    ang = (pos0 + jnp.arange(t)).astype(F32)[:, None] * jnp.tile(inv, LANES // half)[None, :]
    grp = np.arange(LANES) // half
    sc = np.where(grp >= 4, SCORE_SCALE, 1.0)
    mc = jnp.asarray(np.where(grp % 4 < 2, sc, 0.0), F32)
    ms = jnp.asarray(np.where(grp % 4 == 2, -sc, np.where(grp % 4 == 3, sc, 0.0)), F32)
    return jnp.tile(jnp.cos(ang) * mc + jnp.sin(ang) * ms, (reps, 1))


def _state_to_pairs(s):
    nb = s.shape[0]
    st = jnp.swapaxes(s, -1, -2).reshape(nb, A_HEADS // 2, 2, A_HEAD_DIM, A_HEAD_DIM)
    z = jnp.zeros_like(st[:, :, 0])
    top = jnp.concatenate([st[:, :, 0], z], axis=-1)
    bot = jnp.concatenate([z, st[:, :, 1]], axis=-1)
    return jnp.concatenate([top, bot], axis=-2)


def _pairs_to_state(hp):
    nb = hp.shape[0]
    diag = jnp.stack([hp[:, :, :A_HEAD_DIM, :A_HEAD_DIM], hp[:, :, A_HEAD_DIM:, A_HEAD_DIM:]], axis=2)
    return jnp.swapaxes(diag.reshape(nb, A_HEADS, A_HEAD_DIM, A_HEAD_DIM), -1, -2)


def _layer(x, pos0, shift0, wkv0, cache, w):
    nstreams, t, _ = x.shape
    total = nstreams * t
    x2 = x.reshape(total, D_MODEL)
    chunk = min(CHUNK, t)
    rope = _rope_table(pos0, t, nstreams)
    (rt, kt, bt, at, v, g, bonus, wc, shift, q, ckv, kpe, *kv) = _proj(
        x2, shift0, w, rope, nstreams=nstreams, seg=t, rows=min(PROJ_ROWS, total), chunk=chunk,
        expand_kv=cache is None)

    ya, hout = _wkv(rt, kt, bt, at, v, g, bonus, wc, _state_to_pairs(wkv0), w,
                    nstreams=nstreams, ncs=t // chunk, chunk=chunk)

    if cache is None:
        kk, vv = kv
        yb = _attn(q.reshape(nstreams, t, -1), kk.reshape(nstreams, t, -1), vv.reshape(nstreams, t, -1),
                   blk=min(ATTN_BLOCK, t))
    else:
        yb = _attnc(q.reshape(nstreams, t, -1), cache[0], cache[1], ckv, kpe, w)

    y = _tail(x2, ya, yb.reshape(total, D_B), w, rows=min(TAIL_ROWS, total))
    return (y.reshape(nstreams, t, D_MODEL), ckv.reshape(nstreams, t, KV_LORA),
            kpe.reshape(nstreams, t, ROPE_DIM), _pairs_to_state(hout), shift)


def kernel(x_prompt, x_sample, cache_ckv, cache_kpe, state_wkv, state_shift, w_in, mu_shift, w0, w_w2, a0,
           w_a2, w_g2, k_k, k_a, r_k, lnx_g, lnx_b, w_pa, q_norm_g, w_uq, kv_norm_g, w_ukv, w_pb, b_gate,
           w_o, ln1_g, ln1_b, w_gu, w_down, ln2_g, ln2_b):
    w = _prep_weights(w_in, mu_shift, w0, w_w2, a0, w_a2, w_g2, k_k, k_a, r_k, lnx_g, lnx_b, w_pa,
                      q_norm_g, w_uq, kv_norm_g, w_ukv, w_pb, b_gate, w_o, ln1_g, ln1_b, w_gu, w_down,
                      ln2_g, ln2_b)
    bp = x_prompt.shape[0]
    y_p, ckv_p, kpe_p, wkv_p, shift_p = _layer(
        x_prompt, 0, jnp.zeros((bp, 1, A_COLS), F32),
        jnp.zeros((bp, A_HEADS, A_HEAD_DIM, A_HEAD_DIM), F32), None, w)
    y_s, ckv_s, kpe_s, wkv_s, shift_s = _layer(
        x_sample, cache_ckv.shape[1], state_shift, state_wkv, (cache_ckv, cache_kpe), w)
    return (y_p, y_s, ckv_p, kpe_p, wkv_p, shift_p, ckv_s, kpe_s, wkv_s, shift_s)
```

```python
import functools

import numpy as np
import jax
import jax.numpy as jnp
from jax import lax
from jax.experimental import pallas as pl
from jax.experimental.pallas import tpu as pltpu

D_MODEL = 1024
CHUNK = 64
A_HEADS = 8
A_HEAD_DIM = 64
D_A = 512
DECAY_LORA = 64
AAA_LORA = 64
GATE_LORA = 128
A_COLS = 3 * D_A + DECAY_LORA + AAA_LORA + GATE_LORA
LNX_EPS = A_HEAD_DIM * 1e-5
B_HEADS = 8
Q_LORA = 256
KV_LORA = 128
NOPE_DIM = 64
ROPE_DIM = 32
V_DIM = 64
D_B = 512
B_COLS = Q_LORA + KV_LORA + ROPE_DIM
ROPE_BASE = 10000.0
ATTN_SCALE = (NOPE_DIM + ROPE_DIM) ** -0.5
SCORE_SCALE = ATTN_SCALE * float(np.log2(np.e))
RMS_EPS = 1e-6
D_FF = 2816
LN_EPS = 1e-5
DN_ALPHA = 2.0 ** 0.25

LANES = 128
SUBLANES = 8
QK_PAD = 128
PROJ_COLS = A_COLS + Q_LORA + KV_LORA + LANES
FF_CHUNK = 256
N_FF = D_FF // FF_CHUNK
TRI_ROWS = 256
PROJ_ROWS = 512
TAIL_ROWS = 512
ATTN_BLOCK = 1024
WKV_UNITS = 4
VMEM_LIMIT = 56 * 1024 * 1024

F32 = jnp.float32
BF16 = jnp.bfloat16


def _dot(a, b):
    return jnp.dot(a, b, preferred_element_type=F32)


def _dot_nt(a, b):
    return lax.dot_general(a, b, (((1,), (1,)), ((), ())), preferred_element_type=F32)


def _bf(x):
    return x.astype(BF16)


def _split2(x):
    hi = x.astype(BF16)
    lo = (x - hi.astype(F32)).astype(BF16)
    return hi, lo


def _split3(x):
    h1 = x.astype(BF16)
    r1 = x - h1.astype(F32)
    h2 = r1.astype(BF16)
    h3 = (r1 - h2.astype(F32)).astype(BF16)
    return h1, h2, h3


def _sigmoid(z):
    return 1.0 / (1.0 + jnp.exp(-z))


def _full(shape):
    n = len(shape)
    return pl.BlockSpec(shape, lambda *_: (0,) * n)


def _proj_kernel(x_ref, shift0_ref, wall_ref, mu_ref, w0_ref, a0_ref, kk_ref, ka_ref, rk_ref,
                 ww2_ref, wa2_ref, wg2_ref, esum_ref, ltri_ref, qg_ref, kvg_ref, wqa_ref, wqb_ref,
                 rope_ref, wk_ref, place_ref, wv_ref,
                 rt_ref, kt_ref, bt_ref, at_ref, v_ref, g_ref, bonus_ref, wc_ref, shift_ref,
                 q_ref, ckv_ref, kpe_ref, *rest, rows, chunk, seg):
    *kv_out, carry_ref = rest
    b = pl.program_id(1)
    xb = _bf(x_ref[...])

    pa = _dot(xb, wall_ref[:, :A_COLS])
    pq = _dot(xb, wall_ref[:, A_COLS:A_COLS + Q_LORA])
    pkv = _dot(xb, wall_ref[:, A_COLS + Q_LORA:A_COLS + Q_LORA + KV_LORA])
    ppe = _dot(xb, wall_ref[:, A_COLS + Q_LORA + KV_LORA:])

    cqn = _bf(pq * lax.rsqrt(jnp.mean(pq * pq, axis=-1, keepdims=True) + RMS_EPS) * qg_ref[...])
    qa = _dot(cqn, wqa_ref[...])
    qb = _dot(cqn, wqb_ref[...])
    rope = rope_ref[...]
    lane = lax.broadcasted_iota(jnp.int32, (1, QK_PAD), 1)
    cq = jnp.where(lane < NOPE_DIM, np.float32(SCORE_SCALE), rope)
    sq = pltpu.roll(rope, QK_PAD - ROPE_DIM, axis=1)
    for h in range(B_HEADS):
        sl = slice(h * QK_PAD, (h + 1) * QK_PAD)
        q_ref[:, sl] = _bf(qa[:, sl] * cq + qb[:, sl] * sq)
    ckv = pkv * lax.rsqrt(jnp.mean(pkv * pkv, axis=-1, keepdims=True) + RMS_EPS) * kvg_ref[...]
    ckv_ref[...] = ckv
    ppe = ppe * rope
    kpe = ppe[:, :ROPE_DIM] + ppe[:, ROPE_DIM:2 * ROPE_DIM]
    kpe_ref[...] = kpe
    if kv_out:
        kx_ref, vx_ref = kv_out
        cb = _bf(ckv)
        kx_ref[...] = _bf(_dot(cb, wk_ref[...]) + _dot(_bf(kpe), place_ref[...]))
        vx_ref[...] = _bf(_dot_nt(wv_ref[...], cb))

    row = lax.broadcasted_iota(jnp.int32, (rows, 1), 0)
    if seg >= rows:
        first = jnp.where(b == 0, shift0_ref[0], carry_ref[...])
        starts = row == 0
        last = pa[rows - 1:rows, :]
        carry_ref[...] = last
        shift_ref[0] = last
    else:
        first = jnp.broadcast_to(shift0_ref[...], (rows // seg, seg, A_COLS)).reshape(rows, A_COLS)
        starts = row % seg == 0
        for s in range(rows // seg):
            shift_ref[s] = pa[(s + 1) * seg - 1:(s + 1) * seg, :]
    prev = jnp.where(starts, first, pltpu.roll(pa, 1, axis=0))
    xs = pa + (prev - pa) * mu_ref[...]

    r = xs[:, :D_A]
    k = xs[:, D_A:2 * D_A]
    v = xs[:, 2 * D_A:3 * D_A]
    wa = xs[:, 3 * D_A:3 * D_A + LANES]
    gd = xs[:, 3 * D_A + LANES:]

    z = w0_ref[...] + _dot(_bf(jnp.tanh(wa)), ww2_ref[...])
    ld = -np.float32(np.exp(-0.5)) * _sigmoid(z)
    a = _sigmoid(a0_ref[...] + _dot(_bf(wa), wa2_ref[...]))
    g_ref[...] = _dot(_bf(_sigmoid(gd)), wg2_ref[...])

    kkr = k * kk_ref[...]
    kh = k * (1.0 + (a - 1.0) * ka_ref[...])
    hi, lo = _split2(kkr * kkr)
    hs = _dot(jnp.concatenate([hi, lo, _bf(r * kh * rk_ref[...])], axis=0), esum_ref[...])
    kk = kkr / jnp.maximum(jnp.sqrt(hs[:rows] + hs[rows:2 * rows]), 1e-12)
    bonus_ref[...] = hs[2 * rows:] * v
    v_ref[...] = v

    ltri = ltri_ref[...]
    tri = ltri.shape[0]
    h1, h2, h3 = _split3(ld)
    cum = jnp.concatenate(
        [_dot(ltri, h1[i:i + tri]) + _dot(ltri, h2[i:i + tri]) + _dot(ltri, h3[i:i + tri])
         for i in range(0, rows, tri)], axis=0)
    ep = jnp.exp(cum)
    em = jnp.exp(-cum)
    rt_ref[...] = r * ep
    kt_ref[...] = kh * em
    bt_ref[...] = (kk * a) * em
    at_ref[...] = -kk * jnp.exp(cum - ld)
    for c in range(rows // chunk):
        wc_ref[c * SUBLANES:(c + 1) * SUBLANES, :] = jnp.broadcast_to(
            ep[(c + 1) * chunk - 1:(c + 1) * chunk, :], (SUBLANES, D_A))


def _proj(x2, shift0, w, rope, *, nstreams, seg, rows, chunk, expand_kv):
    total = nstreams * seg
    spb = max(1, rows // seg)
    bps = max(1, seg // rows)
    nck = rows // chunk
    tri = min(rows, TRI_ROWS)
    ltri = _chunk_tri(tri, chunk)
    rowblk = lambda n: pl.BlockSpec((rows, n), lambda s, b: (s * bps + b, 0))
    in_specs = [
        rowblk(D_MODEL),
        pl.BlockSpec((spb, 1, A_COLS), lambda s, b: (s, 0, 0)),
        _full((D_MODEL, PROJ_COLS)),
        _full((1, A_COLS)), _full((1, D_A)), _full((1, D_A)), _full((1, D_A)), _full((1, D_A)), _full((1, D_A)),
        _full((LANES, D_A)), _full((LANES, D_A)), _full((GATE_LORA, D_A)),
        _full((D_A, D_A)), _full((tri, tri)),
        _full((1, Q_LORA)), _full((1, KV_LORA)),
        _full((Q_LORA, B_HEADS * QK_PAD)), _full((Q_LORA, B_HEADS * QK_PAD)),
        rowblk(LANES),
        _full((KV_LORA, B_HEADS * QK_PAD)), _full((ROPE_DIM, B_HEADS * QK_PAD)), _full((D_B, KV_LORA)),
    ]
    f32o = lambda n: jax.ShapeDtypeStruct((total, n), F32)
    out_shape = [f32o(D_A)] * 7 + [
        jax.ShapeDtypeStruct((total // chunk * SUBLANES, D_A), F32),
        jax.ShapeDtypeStruct((nstreams, 1, A_COLS), F32),
        jax.ShapeDtypeStruct((total, B_HEADS * QK_PAD), BF16),
        f32o(KV_LORA), f32o(ROPE_DIM),
    ]
    out_specs = [rowblk(D_A)] * 7 + [
        pl.BlockSpec((nck * SUBLANES, D_A), lambda s, b: (s * bps + b, 0)),
        pl.BlockSpec((spb, 1, A_COLS), lambda s, b: (s, 0, 0)),
        rowblk(B_HEADS * QK_PAD), rowblk(KV_LORA), rowblk(ROPE_DIM),
    ]
    if expand_kv:
        out_shape += [jax.ShapeDtypeStruct((total, B_HEADS * QK_PAD), BF16),
                      jax.ShapeDtypeStruct((D_B, total), BF16)]
        out_specs += [rowblk(B_HEADS * QK_PAD), pl.BlockSpec((D_B, rows), lambda s, b: (0, s * bps + b))]
    return pl.pallas_call(
        functools.partial(_proj_kernel, rows=rows, chunk=chunk, seg=seg),
        out_shape=out_shape,
        grid=(nstreams // spb, bps),
        in_specs=in_specs,
        out_specs=out_specs,
        scratch_shapes=[pltpu.VMEM((1, A_COLS), F32)],
        compiler_params=pltpu.CompilerParams(
            dimension_semantics=("arbitrary", "arbitrary"), vmem_limit_bytes=VMEM_LIMIT),
        name="proj",
    )(x2, shift0, w["wall"], w["mu"], w["w0"], w["a0"], w["k_k"], w["k_a"], w["r_k"],
      w["ww2"], w["wa2"], w["wg2"], w["esum"], ltri, w["qg"], w["kvg"], w["wqa"], w["wqb"],
      rope, w["wk"], w["place"], w["wvt"])


def _chunk_tri(rows, chunk):
    i = np.arange(rows)
    m = (i[:, None] // chunk == i[None, :] // chunk) & (i[None, :] <= i[:, None])
    return jnp.asarray(m, BF16)


def _attn_kernel(qi_ref, ki_ref, q_ref, k_ref, vt_ref, o_ref, m_ref, acc_ref, *, blk):
    s_id = pl.program_id(1)
    qi = qi_ref[s_id]
    ki = ki_ref[s_id]

    @pl.when(ki == 0)
    def _():
        m_ref[...] = jnp.full(m_ref.shape, -jnp.inf, F32)
        acc_ref[...] = jnp.zeros(acc_ref.shape, F32)

    def step(r0, nr, nc, masked):
        cs = slice(r0, r0 + nr)

        def scores(h):
            sl = slice(h * QK_PAD, (h + 1) * QK_PAD)
            return _dot_nt(k_ref[0, :nc, sl], q_ref[0, cs, sl])

        if masked:
            cq = (lax.broadcasted_iota(jnp.int32, (nc, nr), 1) + r0) // CHUNK
            mask = lax.broadcasted_iota(jnp.int32, (nc, nr), 0) // CHUNK <= cq
        ones = jnp.ones((V_DIM, nc), BF16)
        ahead = 2
        pending = [scores(h) for h in range(ahead)]
        for h in range(B_HEADS):
            s = pending.pop(0)
            if h + ahead < B_HEADS:
                pending.append(scores(h + ahead))
            vext = jnp.concatenate([vt_ref[0, h * V_DIM:(h + 1) * V_DIM, :nc], ones], axis=0)
            if masked:
                s = jnp.where(mask, s, -jnp.inf)
            m_prev = m_ref[h, :, cs]
            m_new = jnp.maximum(m_prev, jnp.max(s, axis=0, keepdims=True))
            p = jnp.exp2(s - m_new[:1])
            acc_ref[h, :, cs] = jnp.exp2(m_prev - m_new)[:1] * acc_ref[h, :, cs] + _dot(vext, _bf(p))
            m_ref[h, :, cs] = m_new

    @pl.when(ki < qi)
    def _():
        step(0, blk, blk, False)

    @pl.when(ki == qi)
    def _():
        half = blk // 2
        step(0, half, half, True)
        step(half, half, blk, True)
        for j in range(B_HEADS // 2):
            a0 = acc_ref[2 * j]
            a1 = acc_ref[2 * j + 1]
            out_t = jnp.concatenate([a0[:V_DIM] / a0[V_DIM:], a1[:V_DIM] / a1[V_DIM:]], axis=0)
            o_ref[0, :, j * LANES:(j + 1) * LANES] = out_t.T


def _attn(q, k, vt, *, blk):
    nb, t, _ = q.shape
    steps = [(i, j) for i in range(t // blk) for j in range(i + 1)]
    qi = jnp.asarray([s[0] for s in steps], jnp.int32)
    ki = jnp.asarray([s[1] for s in steps], jnp.int32)
    grid_spec = pltpu.PrefetchScalarGridSpec(
        num_scalar_prefetch=2,
        grid=(nb, len(steps)),
        in_specs=[pl.BlockSpec((1, blk, B_HEADS * QK_PAD), lambda b, s, qi, ki: (b, qi[s], 0)),
                  pl.BlockSpec((1, blk, B_HEADS * QK_PAD), lambda b, s, qi, ki: (b, ki[s], 0)),
                  pl.BlockSpec((1, D_B, blk), lambda b, s, qi, ki: (b, 0, ki[s]))],
        out_specs=pl.BlockSpec((1, blk, D_B), lambda b, s, qi, ki: (b, qi[s], 0)),
        scratch_shapes=[pltpu.VMEM((B_HEADS, SUBLANES, blk), F32),
                        pltpu.VMEM((B_HEADS, LANES, blk), F32)])
    return pl.pallas_call(
        functools.partial(_attn_kernel, blk=blk),
        out_shape=jax.ShapeDtypeStruct((nb, t, D_B), F32),
        grid_spec=grid_spec,
        compiler_params=pltpu.CompilerParams(
            dimension_semantics=("arbitrary", "arbitrary"), vmem_limit_bytes=VMEM_LIMIT),
        name="attn",
    )(qi, ki, q, k, vt)


def _attnc_kernel(q_ref, cckv_ref, ckpe_ref, nckv_ref, nkpe_ref, wk_ref, wv_ref, place_ref, o_ref, *, t):
    q = q_ref[0]
    heads = [q[:, h * QK_PAD:(h + 1) * QK_PAD] for h in range(B_HEADS)]
    qf = jnp.concatenate(heads, axis=0)
    qa = jnp.concatenate([_bf(_dot_nt(heads[h], wk_ref[:, h * QK_PAD:(h + 1) * QK_PAD]))
                          for h in range(B_HEADS)], axis=0)
    place = place_ref[:, :QK_PAD]

    def scores(ckv, kpe):
        cb = _bf(ckv)
        return _dot_nt(qa, cb) + _dot_nt(qf, _bf(_dot(_bf(kpe), place))), cb

    s_c, cb_c = scores(cckv_ref[0], ckpe_ref[0])
    s_n, cb_n = scores(nckv_ref[...], nkpe_ref[...])
    m = jnp.maximum(jnp.max(s_c, axis=-1, keepdims=True), jnp.max(s_n, axis=-1, keepdims=True))
    p_c = jnp.exp2(s_c - m)
    p_n = jnp.exp2(s_n - m)
    den = jnp.sum(p_c, axis=-1, keepdims=True) + jnp.sum(p_n, axis=-1, keepdims=True)
    lat = _bf((_dot(_bf(p_c), cb_c) + _dot(_bf(p_n), cb_n)) / den)
    low = lax.broadcasted_iota(jnp.int32, (1, LANES), 1) < V_DIM
    zero = jnp.zeros((), BF16)
    for j in range(B_HEADS // 2):
        wvp = wv_ref[:, j * LANES:(j + 1) * LANES]
        o_ref[0, :, j * LANES:(j + 1) * LANES] = (
            _dot(lat[2 * j * t:(2 * j + 1) * t], jnp.where(low, wvp, zero))
            + _dot(lat[(2 * j + 1) * t:(2 * j + 2) * t], jnp.where(low, zero, wvp)))


def _attnc(q, cache_ckv, cache_kpe, ckv, kpe, w):
    nb, t, _ = q.shape
    past = cache_ckv.shape[1]
    return pl.pallas_call(
        functools.partial(_attnc_kernel, t=t),
        out_shape=jax.ShapeDtypeStruct((nb, t, D_B), F32),
        grid=(nb,),
        in_specs=[pl.BlockSpec((1, t, B_HEADS * QK_PAD), lambda b: (b, 0, 0)),
                  pl.BlockSpec((1, past, KV_LORA), lambda b: (b, 0, 0)),
                  pl.BlockSpec((1, past, ROPE_DIM), lambda b: (b, 0, 0)),
                  pl.BlockSpec((t, KV_LORA), lambda b: (b, 0)),
                  pl.BlockSpec((t, ROPE_DIM), lambda b: (b, 0)),
                  _full((KV_LORA, B_HEADS * QK_PAD)), _full((KV_LORA, D_B)),
                  _full((ROPE_DIM, B_HEADS * QK_PAD))],
        out_specs=pl.BlockSpec((1, t, D_B), lambda b: (b, 0, 0)),
        compiler_params=pltpu.CompilerParams(dimension_semantics=("arbitrary",)),
        name="attnc",
    )(q, cache_ckv, cache_kpe, ckv, kpe, w["wk"], w["wv"], w["place"])


def _wkv_kernel(rt_ref, kt_ref, bt_ref, at_ref, v_ref, g_ref, bonus_ref, wc_ref, lg_ref, lb_ref,
                eavg_ref, h0_ref, y_ref, hout_ref, h_ref, *, chunk, group, nsub, nsteps):
    c = pl.program_id(1)
    C2 = 2 * chunk
    npair = A_HEADS // 2

    @pl.when(c == 0)
    def _():
        h_ref[...] = h0_ref[...]

    low = lax.broadcasted_iota(jnp.int32, (chunk, LANES), 1) < A_HEAD_DIM
    ii = lax.broadcasted_iota(jnp.int32, (C2, C2), 0)
    jj = lax.broadcasted_iota(jnp.int32, (C2, C2), 1)
    strict = ii > jj
    incl = ii >= jj
    eye_c = (ii == jj).astype(F32)
    ki = lax.broadcasted_iota(jnp.int32, (LANES, LANES), 0)
    kj = lax.broadcasted_iota(jnp.int32, (LANES, LANES), 1)
    eye_k = (ki == kj).astype(F32)
    eavg = eavg_ref[...]

    def stack(t):
        return jnp.concatenate([jnp.where(low, t, 0.0), jnp.where(low, 0.0, t)], axis=0)

    units = [(ci, j) for ci in range(nsub * group) for j in range(npair)]
    rows = lambda ci: slice(ci * chunk, (ci + 1) * chunk)
    lanes = lambda j: slice(j * LANES, (j + 1) * LANES)
    ld = lambda ref: [stack(ref[rows(ci), lanes(j)]) for ci, j in units]
    At, Bt, Kt, Rt, Vs = ld(at_ref), ld(bt_ref), ld(kt_ref), ld(rt_ref), ld(v_ref)
    nu = range(len(units))
    Vb = [_bf(Vs[u]) for u in nu]
    g1 = [_dot_nt(_bf(jnp.concatenate([At[u], Rt[u]], axis=0)),
                  _bf(jnp.concatenate([Bt[u], Kt[u]], axis=0))) for u in nu]
    Aab = [jnp.where(strict, g1[u][:C2, :C2], 0.0) for u in nu]
    Aak = [_bf(jnp.where(strict, g1[u][:C2, C2:], 0.0)) for u in nu]
    Arb = [_bf(jnp.where(incl, g1[u][C2:, :C2], 0.0)) for u in nu]
    Ark = [_bf(jnp.where(incl, g1[u][C2:, C2:], 0.0)) for u in nu]
    rcat = lambda *t: jnp.concatenate(t, axis=0)
    Tm = [eye_c + Aab[u] for u in nu]
    Nb = [_bf(Aab[u]) for u in nu]
    Pw = [_dot(Nb[u], Nb[u]) for u in nu]
    for _ in range(chunk.bit_length() - 3):
        Pb = [_bf(Pw[u]) for u in nu]
        st = [_dot(rcat(_bf(Tm[u]), Pb[u]), Pb[u]) for u in nu]
        Tm = [Tm[u] + st[u][:C2] for u in nu]
        Pw = [st[u][C2:] for u in nu]
    Tm = [Tm[u] + _dot(_bf(Tm[u]), _bf(Pw[u])) for u in nu]
    wrow = [wc_ref[ci * SUBLANES:ci * SUBLANES + 1, lanes(j)] for ci, j in units]
    BwT = [_bf((Bt[u] * wrow[u]).T) for u in nu]
    KwT = [_bf((Kt[u] * wrow[u]).T) for u in nu]
    sv = [_dot(rcat(Aak[u], Ark[u], KwT[u]), Vb[u]) for u in nu]
    PPb = [_bf(_dot(_bf(Tm[u]), _bf(jnp.concatenate([At[u], sv[u][:C2]], axis=1)))) for u in nu]
    sp = [_dot(rcat(Arb[u], BwT[u]), PPb[u]) for u in nu]
    Q1M1, Q2, M2 = [], [], []
    for u in nu:
        q1s = Rt[u] + sp[u][:C2, :LANES]
        q2s = sp[u][:C2, LANES:] + sv[u][C2:2 * C2]
        m1 = eye_k * wrow[u] + sp[u][C2:, :LANES]
        Q1M1.append(_bf(rcat(q1s[:chunk] + q1s[chunk:], m1)))
        Q2.append(q2s[:chunk] + q2s[chunk:])
        M2.append(sp[u][C2:, LANES:] + sv[u][2 * C2:])

    H = {(si, j): h_ref[si, j] for si in range(nsub) for j in range(npair)}
    Y = []
    for u, (ci, j) in enumerate(units):
        key = (ci // group, j)
        sh = _dot(Q1M1[u], _bf(H[key]))
        Y.append(sh[:chunk] + Q2[u])
        H[key] = sh[chunk:] + M2[u]
    for (si, j), val in H.items():
        h_ref[si, j] = val

    def headmean(t):
        hi, lo = _split2(t)
        m = _dot(rcat(hi, lo), eavg)
        return m[:chunk] + m[chunk:]

    mu = [headmean(Y[u]) for u in nu]
    dv = [Y[u] - mu[u] for u in nu]
    var = [headmean(dv[u] * dv[u]) for u in nu]
    for u, (ci, j) in enumerate(units):
        yn = dv[u] * lax.rsqrt(var[u] + LNX_EPS) * lg_ref[:, lanes(j)] + lb_ref[:, lanes(j)]
        y_ref[rows(ci), lanes(j)] = (yn + bonus_ref[rows(ci), lanes(j)]) * g_ref[rows(ci), lanes(j)]

    @pl.when(c == nsteps - 1)
    def _():
        hout_ref[...] = h_ref[...]


def _wkv(rt, kt, bt, at, v, g, bonus, wc, h0, w, *, nstreams, ncs, chunk):
    total = rt.shape[0]
    group = min(WKV_UNITS, ncs)
    nsub = min(WKV_UNITS // group, nstreams)
    nsteps = ncs // group
    blk = pl.BlockSpec((nsub * group * chunk, D_A), lambda s, c: (s * nsteps + c, 0))
    hspec = pl.BlockSpec((nsub, A_HEADS // 2, LANES, LANES), lambda s, c: (s, 0, 0, 0))
    return pl.pallas_call(
        functools.partial(_wkv_kernel, chunk=chunk, group=group, nsub=nsub, nsteps=nsteps),
        out_shape=[jax.ShapeDtypeStruct((total, D_A), F32),
                   jax.ShapeDtypeStruct((nstreams, A_HEADS // 2, LANES, LANES), F32)],
        grid=(nstreams // nsub, nsteps),
        in_specs=[blk] * 7 + [pl.BlockSpec((nsub * group * SUBLANES, D_A), lambda s, c: (s * nsteps + c, 0)),
                              _full((1, D_A)), _full((1, D_A)), _full((LANES, LANES)), hspec],
        out_specs=[blk, hspec],
        scratch_shapes=[pltpu.VMEM((nsub, A_HEADS // 2, LANES, LANES), F32)],
        compiler_params=pltpu.CompilerParams(
            dimension_semantics=("arbitrary", "arbitrary"), vmem_limit_bytes=VMEM_LIMIT),
        name="wkv",
    )(rt, kt, bt, at, v, g, bonus, wc, w["lnx_g"], w["lnx_b"], w["eavg"], h0)


def _layer_norm(t, g, b):
    mu = jnp.mean(t, axis=-1, keepdims=True)
    d = t - mu
    var = jnp.mean(d * d, axis=-1, keepdims=True)
    return d * lax.rsqrt(var + LN_EPS) * g + b


def _tail_kernel(x_ref, ya_ref, yb_ref, wg_ref, bg_ref, wpa_ref, wpb_ref, wo_ref, l1g_ref, l1b_ref,
                 wgu_ref, wdown_ref, l2g_ref, l2b_ref, o_ref):
    x = x_ref[...]
    gates = _sigmoid(_dot(_bf(x), wg_ref[...]) + bg_ref[...])
    m = (gates[:, :D_MODEL] * _dot(_bf(ya_ref[...]), wpa_ref[...])
         + gates[:, D_MODEL:] * _dot(_bf(yb_ref[...]), wpb_ref[...]))
    h = _layer_norm(DN_ALPHA * x + _dot(_bf(m), wo_ref[...]), l1g_ref[...], l1b_ref[...])
    hb = _bf(h)

    def gate_up(c):
        cols = slice(c * FF_CHUNK, (c + 1) * FF_CHUNK)
        ucols = slice(D_FF + c * FF_CHUNK, D_FF + (c + 1) * FF_CHUNK)
        return _dot(hb, wgu_ref[:, cols]), _dot(hb, wgu_ref[:, ucols])

    f = None
    nxt = gate_up(0)
    for c in range(N_FF):
        gate, up = nxt
        if c + 1 < N_FF:
            nxt = gate_up(c + 1)
        d = _dot(_bf(gate * _sigmoid(gate) * up), wdown_ref[c * FF_CHUNK:(c + 1) * FF_CHUNK, :])
        f = d if f is None else f + d
    o_ref[...] = _layer_norm(DN_ALPHA * h + f, l2g_ref[...], l2b_ref[...])


def _tail(x2, ya, yb, w, *, rows):
    total = x2.shape[0]
    rowblk = lambda n: pl.BlockSpec((rows, n), lambda i: (i, 0))

    def const(shape):
        n = len(shape)
        return pl.BlockSpec(shape, lambda i: (0,) * n, pipeline_mode=pl.Buffered(1))

    return pl.pallas_call(
        _tail_kernel,
        out_shape=jax.ShapeDtypeStruct((total, D_MODEL), F32),
        grid=(total // rows,),
        in_specs=[rowblk(D_MODEL), rowblk(D_A), rowblk(D_B),
                  const((D_MODEL, 2 * D_MODEL)), const((1, 2 * D_MODEL)),
                  const((D_A, D_MODEL)), const((D_B, D_MODEL)), const((D_MODEL, D_MODEL)),
                  const((1, D_MODEL)), const((1, D_MODEL)),
                  const((D_MODEL, 2 * D_FF)), const((D_FF, D_MODEL)),
                  const((1, D_MODEL)), const((1, D_MODEL))],
        out_specs=rowblk(D_MODEL),
        compiler_params=pltpu.CompilerParams(
            dimension_semantics=("arbitrary",), vmem_limit_bytes=VMEM_LIMIT),
        name="tail",
    )(x2, ya, yb, w["wg"], w["bg"], w["wpa"], w["wpb"], w["wo"], w["l1g"], w["l1b"],
      w["wgu"], w["wdown"], w["l2g"], w["l2b"])


def _prep_weights(w_in, mu_shift, w0, w_w2, a0, w_a2, w_g2, k_k, k_a, r_k, lnx_g, lnx_b, w_pa,
                  q_norm_g, w_uq, kv_norm_g, w_ukv, w_pb, b_gate, w_o, ln1_g, ln1_b, w_gu, w_down,
                  ln2_g, ln2_b):
    row = lambda t: t.reshape(1, -1).astype(F32)
    nb = A_COLS + B_COLS
    pe = w_in[:, nb - ROPE_DIM:nb]
    half = ROPE_DIM // 2
    pe_sw = jnp.concatenate([pe[:, half:], pe[:, :half]], axis=1)
    wall = jnp.concatenate([w_in[:, :nb - ROPE_DIM], pe, pe_sw,
                            jnp.zeros((D_MODEL, LANES - 2 * ROPE_DIM), F32)], axis=1)
    uq = w_uq.reshape(Q_LORA, B_HEADS, NOPE_DIM + ROPE_DIM)
    nope, r1, r2 = uq[..., :NOPE_DIM], uq[..., NOPE_DIM:NOPE_DIM + half], uq[..., NOPE_DIM + half:]
    zpad = jnp.zeros((Q_LORA, B_HEADS, QK_PAD - NOPE_DIM - ROPE_DIM), F32)
    wqa = jnp.concatenate([nope, r1, r2, zpad], axis=-1).reshape(Q_LORA, B_HEADS * QK_PAD)
    wqb = jnp.concatenate([jnp.zeros_like(nope), r2, r1, zpad], axis=-1).reshape(Q_LORA, B_HEADS * QK_PAD)
    ukv = w_ukv.reshape(KV_LORA, B_HEADS, NOPE_DIM + V_DIM)
    wk = jnp.concatenate([ukv[..., :NOPE_DIM], jnp.zeros((KV_LORA, B_HEADS, QK_PAD - NOPE_DIM), F32)],
                         axis=-1).reshape(KV_LORA, B_HEADS * QK_PAD)
    wv = ukv[..., NOPE_DIM:].reshape(KV_LORA, D_B)
    place = np.zeros((ROPE_DIM, B_HEADS * QK_PAD), np.float32)
    for h in range(B_HEADS):
        place[np.arange(ROPE_DIM), h * QK_PAD + NOPE_DIM + np.arange(ROPE_DIM)] = 1.0
    hid = np.arange(D_A) // A_HEAD_DIM
    esum = (hid[:, None] == hid[None, :]).astype(np.float32)
    lid = np.arange(LANES) // A_HEAD_DIM
    eavg = (lid[:, None] == lid[None, :]).astype(np.float32) / A_HEAD_DIM
    zl = jnp.zeros((LANES - DECAY_LORA, D_A), F32)
    return {
        "wall": _bf(wall), "mu": row(mu_shift), "w0": row(w0), "a0": row(a0), "k_k": row(k_k),
        "k_a": row(k_a), "r_k": row(r_k),
        "ww2": _bf(jnp.concatenate([w_w2, zl], axis=0)), "wa2": _bf(jnp.concatenate([zl, w_a2], axis=0)),
        "wg2": _bf(w_g2), "esum": jnp.asarray(esum, BF16), "eavg": jnp.asarray(eavg, BF16),
        "qg": row(q_norm_g), "kvg": row(kv_norm_g), "wqa": _bf(wqa), "wqb": _bf(wqb),
        "wk": _bf(wk), "wv": _bf(wv), "wvt": _bf(wv.T), "place": jnp.asarray(place, BF16),
        "lnx_g": row(lnx_g), "lnx_b": row(lnx_b),
        "wg": _bf(w_in[:, nb:]), "bg": row(b_gate), "wpa": _bf(w_pa), "wpb": _bf(w_pb), "wo": _bf(w_o),
        "l1g": row(ln1_g), "l1b": row(ln1_b), "l2g": row(ln2_g), "l2b": row(ln2_b),
        "wgu": _bf(w_gu), "wdown": _bf(w_down),
    }


def _rope_table(pos0, t, reps):
    half = ROPE_DIM // 2
    inv = ROPE_BASE ** (-jnp.arange(half, dtype=F32) / half)
    ang = (pos0 + jnp.arange(t)).astype(F32)[:, None] * jnp.tile(inv, LANES // half)[None, :]
    grp = np.arange(LANES) // half
    sc = np.where(grp >= 4, SCORE_SCALE, 1.0)
    mc = jnp.asarray(np.where(grp % 4 < 2, sc, 0.0), F32)
    ms = jnp.asarray(np.where(grp % 4 == 2, -sc, np.where(grp % 4 == 3, sc, 0.0)), F32)
    return jnp.tile(jnp.cos(ang) * mc + jnp.sin(ang) * ms, (reps, 1))


def _state_to_pairs(s):
    nb = s.shape[0]
    st = jnp.swapaxes(s, -1, -2).reshape(nb, A_HEADS // 2, 2, A_HEAD_DIM, A_HEAD_DIM)
    z = jnp.zeros_like(st[:, :, 0])
    top = jnp.concatenate([st[:, :, 0], z], axis=-1)
    bot = jnp.concatenate([z, st[:, :, 1]], axis=-1)
    return jnp.concatenate([top, bot], axis=-2)


def _pairs_to_state(hp):
    nb = hp.shape[0]
    diag = jnp.stack([hp[:, :, :A_HEAD_DIM, :A_HEAD_DIM], hp[:, :, A_HEAD_DIM:, A_HEAD_DIM:]], axis=2)
    return jnp.swapaxes(diag.reshape(nb, A_HEADS, A_HEAD_DIM, A_HEAD_DIM), -1, -2)


def _layer(x, pos0, shift0, wkv0, cache, w):
    nstreams, t, _ = x.shape
    total = nstreams * t
    x2 = x.reshape(total, D_MODEL)
    chunk = min(CHUNK, t)
    rope = _rope_table(pos0, t, nstreams)
    (rt, kt, bt, at, v, g, bonus, wc, shift, q, ckv, kpe, *kv) = _proj(
        x2, shift0, w, rope, nstreams=nstreams, seg=t, rows=min(PROJ_ROWS, total), chunk=chunk,
        expand_kv=cache is None)

    ya, hout = _wkv(rt, kt, bt, at, v, g, bonus, wc, _state_to_pairs(wkv0), w,
                    nstreams=nstreams, ncs=t // chunk, chunk=chunk)

    if cache is None:
        kk, vv = kv
        yb = _attn(q.reshape(nstreams, t, -1), kk.reshape(nstreams, t, -1), vv.reshape(nstreams, D_B, t),
                   blk=min(ATTN_BLOCK, t))
    else:
        yb = _attnc(q.reshape(nstreams, t, -1), cache[0], cache[1], ckv, kpe, w)

    y = _tail(x2, ya, yb.reshape(total, D_B), w, rows=min(TAIL_ROWS, total))
    return (y.reshape(nstreams, t, D_MODEL), ckv.reshape(nstreams, t, KV_LORA),
            kpe.reshape(nstreams, t, ROPE_DIM), _pairs_to_state(hout), shift)


def kernel(x_prompt, x_sample, cache_ckv, cache_kpe, state_wkv, state_shift, w_in, mu_shift, w0, w_w2, a0,
           w_a2, w_g2, k_k, k_a, r_k, lnx_g, lnx_b, w_pa, q_norm_g, w_uq, kv_norm_g, w_ukv, w_pb, b_gate,
           w_o, ln1_g, ln1_b, w_gu, w_down, ln2_g, ln2_b):
    w = _prep_weights(w_in, mu_shift, w0, w_w2, a0, w_a2, w_g2, k_k, k_a, r_k, lnx_g, lnx_b, w_pa,
                      q_norm_g, w_uq, kv_norm_g, w_ukv, w_pb, b_gate, w_o, ln1_g, ln1_b, w_gu, w_down,
                      ln2_g, ln2_b)
    bp = x_prompt.shape[0]
    y_p, ckv_p, kpe_p, wkv_p, shift_p = _layer(
        x_prompt, 0, jnp.zeros((bp, 1, A_COLS), F32),
        jnp.zeros((bp, A_HEADS, A_HEAD_DIM, A_HEAD_DIM), F32), None, w)
    y_s, ckv_s, kpe_s, wkv_s, shift_s = _layer(
        x_sample, cache_ckv.shape[1], state_shift, state_wkv, (cache_ckv, cache_kpe), w)
    return (y_p, y_s, ckv_p, kpe_p, wkv_p, shift_p, ckv_s, kpe_s, wkv_s, shift_s)
```

```python
import functools

import numpy as np
import jax
import jax.numpy as jnp
from jax import lax
from jax.experimental import pallas as pl
from jax.experimental.pallas import tpu as pltpu

D_MODEL = 1024
CHUNK = 64
A_HEADS = 8
A_HEAD_DIM = 64
D_A = 512
DECAY_LORA = 64
AAA_LORA = 64
GATE_LORA = 128
A_COLS = 3 * D_A + DECAY_LORA + AAA_LORA + GATE_LORA
LNX_EPS = A_HEAD_DIM * 1e-5
B_HEADS = 8
Q_LORA = 256
KV_LORA = 128
NOPE_DIM = 64
ROPE_DIM = 32
V_DIM = 64
D_B = 512
B_COLS = Q_LORA + KV_LORA + ROPE_DIM
ROPE_BASE = 10000.0
ATTN_SCALE = (NOPE_DIM + ROPE_DIM) ** -0.5
SCORE_SCALE = ATTN_SCALE * float(np.log2(np.e))
RMS_EPS = 1e-6
D_FF = 2816
LN_EPS = 1e-5
DN_ALPHA = 2.0 ** 0.25

LANES = 128
SUBLANES = 8
QK_PAD = 128
PROJ_COLS = A_COLS + Q_LORA + KV_LORA + LANES
FF_CHUNK = 256
N_FF = D_FF // FF_CHUNK
TRI_ROWS = 256
PROJ_ROWS = 512
TAIL_ROWS = 512
ATTN_BLOCK = 1024
WKV_UNITS = 4
VMEM_LIMIT = 56 * 1024 * 1024

F32 = jnp.float32
BF16 = jnp.bfloat16


def _dot(a, b):
    return jnp.dot(a, b, preferred_element_type=F32)


def _dot_nt(a, b):
    return lax.dot_general(a, b, (((1,), (1,)), ((), ())), preferred_element_type=F32)


def _bf(x):
    return x.astype(BF16)


def _split2(x):
    hi = x.astype(BF16)
    lo = (x - hi.astype(F32)).astype(BF16)
    return hi, lo


def _split3(x):
    h1 = x.astype(BF16)
    r1 = x - h1.astype(F32)
    h2 = r1.astype(BF16)
    h3 = (r1 - h2.astype(F32)).astype(BF16)
    return h1, h2, h3


def _sigmoid(z):
    return 1.0 / (1.0 + jnp.exp(-z))


def _full(shape):
    n = len(shape)
    return pl.BlockSpec(shape, lambda *_: (0,) * n)


def _proj_kernel(x_ref, shift0_ref, wall_ref, mu_ref, w0_ref, a0_ref, kk_ref, ka_ref, rk_ref,
                 ww2_ref, wa2_ref, wg2_ref, esum_ref, ltri_ref, qg_ref, kvg_ref, wqa_ref, wqb_ref,
                 rope_ref, wk_ref, place_ref, wv_ref,
                 rt_ref, kt_ref, bt_ref, at_ref, v_ref, g_ref, bonus_ref, wc_ref, shift_ref,
                 q_ref, ckv_ref, kpe_ref, *rest, rows, chunk, seg):
    *kv_out, carry_ref = rest
    b = pl.program_id(1)
    xb = _bf(x_ref[...])

    pa = _dot(xb, wall_ref[:, :A_COLS])
    pq = _dot(xb, wall_ref[:, A_COLS:A_COLS + Q_LORA])
    pkv = _dot(xb, wall_ref[:, A_COLS + Q_LORA:A_COLS + Q_LORA + KV_LORA])
    ppe = _dot(xb, wall_ref[:, A_COLS + Q_LORA + KV_LORA:])

    cqn = _bf(pq * lax.rsqrt(jnp.mean(pq * pq, axis=-1, keepdims=True) + RMS_EPS) * qg_ref[...])
    qa = _dot(cqn, wqa_ref[...])
    qb = _dot(cqn, wqb_ref[...])
    rope = rope_ref[...]
    lane = lax.broadcasted_iota(jnp.int32, (1, QK_PAD), 1)
    cq = jnp.where(lane < NOPE_DIM, np.float32(SCORE_SCALE), rope)
    sq = pltpu.roll(rope, QK_PAD - ROPE_DIM, axis=1)
    for h in range(B_HEADS):
        sl = slice(h * QK_PAD, (h + 1) * QK_PAD)
        q_ref[:, sl] = _bf(qa[:, sl] * cq + qb[:, sl] * sq)
    ckv = pkv * lax.rsqrt(jnp.mean(pkv * pkv, axis=-1, keepdims=True) + RMS_EPS) * kvg_ref[...]
    ckv_ref[...] = ckv
    ppe = ppe * rope
    kpe = ppe[:, :ROPE_DIM] + ppe[:, ROPE_DIM:2 * ROPE_DIM]
    kpe_ref[...] = kpe
    if kv_out:
        kx_ref, vx_ref = kv_out
        cb = _bf(ckv)
        kx_ref[...] = _bf(_dot_nt(wk_ref[...], cb) + _dot_nt(place_ref[...], _bf(kpe)))
        vx_ref[...] = _bf(_dot(cb, wv_ref[...]))

    row = lax.broadcasted_iota(jnp.int32, (rows, 1), 0)
    if seg >= rows:
        first = jnp.where(b == 0, shift0_ref[0], carry_ref[...])
        starts = row == 0
        last = pa[rows - 1:rows, :]
        carry_ref[...] = last
        shift_ref[0] = last
    else:
        first = jnp.broadcast_to(shift0_ref[...], (rows // seg, seg, A_COLS)).reshape(rows, A_COLS)
        starts = row % seg == 0
        for s in range(rows // seg):
            shift_ref[s] = pa[(s + 1) * seg - 1:(s + 1) * seg, :]
    prev = jnp.where(starts, first, pltpu.roll(pa, 1, axis=0))
    xs = pa + (prev - pa) * mu_ref[...]

    r = xs[:, :D_A]
    k = xs[:, D_A:2 * D_A]
    v = xs[:, 2 * D_A:3 * D_A]
    wa = xs[:, 3 * D_A:3 * D_A + LANES]
    gd = xs[:, 3 * D_A + LANES:]

    z = w0_ref[...] + _dot(_bf(jnp.tanh(wa)), ww2_ref[...])
    ld = -np.float32(np.exp(-0.5)) * _sigmoid(z)
    a = _sigmoid(a0_ref[...] + _dot(_bf(wa), wa2_ref[...]))
    g_ref[...] = _dot(_bf(_sigmoid(gd)), wg2_ref[...])

    kkr = k * kk_ref[...]
    kh = k * (1.0 + (a - 1.0) * ka_ref[...])
    hi, lo = _split2(kkr * kkr)
    hs = _dot(jnp.concatenate([hi, lo, _bf(r * kh * rk_ref[...])], axis=0), esum_ref[...])
    kk = kkr / jnp.maximum(jnp.sqrt(hs[:rows] + hs[rows:2 * rows]), 1e-12)
    bonus_ref[...] = hs[2 * rows:] * v
    v_ref[...] = v

    ltri = ltri_ref[...]
    tri = ltri.shape[0]
    h1, h2, h3 = _split3(ld)
    cum = jnp.concatenate(
        [_dot(ltri, h1[i:i + tri]) + _dot(ltri, h2[i:i + tri]) + _dot(ltri, h3[i:i + tri])
         for i in range(0, rows, tri)], axis=0)
    ep = jnp.exp(cum)
    em = jnp.exp(-cum)
    rt_ref[...] = r * ep
    kt_ref[...] = kh * em
    bt_ref[...] = (kk * a) * em
    at_ref[...] = -kk * jnp.exp(cum - ld)
    for c in range(rows // chunk):
        wc_ref[c * SUBLANES:(c + 1) * SUBLANES, :] = jnp.broadcast_to(
            ep[(c + 1) * chunk - 1:(c + 1) * chunk, :], (SUBLANES, D_A))


def _proj(x2, shift0, w, rope, *, nstreams, seg, rows, chunk, expand_kv):
    total = nstreams * seg
    spb = max(1, rows // seg)
    bps = max(1, seg // rows)
    nck = rows // chunk
    tri = min(rows, TRI_ROWS)
    ltri = _chunk_tri(tri, chunk)
    rowblk = lambda n: pl.BlockSpec((rows, n), lambda s, b: (s * bps + b, 0))
    in_specs = [
        rowblk(D_MODEL),
        pl.BlockSpec((spb, 1, A_COLS), lambda s, b: (s, 0, 0)),
        _full((D_MODEL, PROJ_COLS)),
        _full((1, A_COLS)), _full((1, D_A)), _full((1, D_A)), _full((1, D_A)), _full((1, D_A)), _full((1, D_A)),
        _full((LANES, D_A)), _full((LANES, D_A)), _full((GATE_LORA, D_A)),
        _full((D_A, D_A)), _full((tri, tri)),
        _full((1, Q_LORA)), _full((1, KV_LORA)),
        _full((Q_LORA, B_HEADS * QK_PAD)), _full((Q_LORA, B_HEADS * QK_PAD)),
        rowblk(LANES),
        _full((B_HEADS * QK_PAD, KV_LORA)), _full((B_HEADS * QK_PAD, ROPE_DIM)), _full((KV_LORA, D_B)),
    ]
    f32o = lambda n: jax.ShapeDtypeStruct((total, n), F32)
    out_shape = [f32o(D_A)] * 7 + [
        jax.ShapeDtypeStruct((total // chunk * SUBLANES, D_A), F32),
        jax.ShapeDtypeStruct((nstreams, 1, A_COLS), F32),
        jax.ShapeDtypeStruct((total, B_HEADS * QK_PAD), BF16),
        f32o(KV_LORA), f32o(ROPE_DIM),
    ]
    out_specs = [rowblk(D_A)] * 7 + [
        pl.BlockSpec((nck * SUBLANES, D_A), lambda s, b: (s * bps + b, 0)),
        pl.BlockSpec((spb, 1, A_COLS), lambda s, b: (s, 0, 0)),
        rowblk(B_HEADS * QK_PAD), rowblk(KV_LORA), rowblk(ROPE_DIM),
    ]
    if expand_kv:
        out_shape += [jax.ShapeDtypeStruct((B_HEADS * QK_PAD, total), BF16),
                      jax.ShapeDtypeStruct((total, D_B), BF16)]
        out_specs += [pl.BlockSpec((B_HEADS * QK_PAD, rows), lambda s, b: (0, s * bps + b)), rowblk(D_B)]
    return pl.pallas_call(
        functools.partial(_proj_kernel, rows=rows, chunk=chunk, seg=seg),
        out_shape=out_shape,
        grid=(nstreams // spb, bps),
        in_specs=in_specs,
        out_specs=out_specs,
        scratch_shapes=[pltpu.VMEM((1, A_COLS), F32)],
        compiler_params=pltpu.CompilerParams(
            dimension_semantics=("arbitrary", "arbitrary"), vmem_limit_bytes=VMEM_LIMIT),
        name="proj",
    )(x2, shift0, w["wall"], w["mu"], w["w0"], w["a0"], w["k_k"], w["k_a"], w["r_k"],
      w["ww2"], w["wa2"], w["wg2"], w["esum"], ltri, w["qg"], w["kvg"], w["wqa"], w["wqb"],
      rope, w["wkt"], w["placet"], w["wv"])


def _chunk_tri(rows, chunk):
    i = np.arange(rows)
    m = (i[:, None] // chunk == i[None, :] // chunk) & (i[None, :] <= i[:, None])
    return jnp.asarray(m, BF16)


def _attn_kernel(qi_ref, ki_ref, q_ref, kt_ref, v_ref, o_ref, m_ref, acc_ref, *, blk):
    s_id = pl.program_id(1)
    qi = qi_ref[s_id]
    ki = ki_ref[s_id]

    @pl.when(ki == 0)
    def _():
        m_ref[...] = jnp.full(m_ref.shape, -jnp.inf, F32)
        acc_ref[...] = jnp.zeros(acc_ref.shape, F32)

    low = lax.broadcasted_iota(jnp.int32, (1, LANES), 1) < V_DIM

    def step(r0, nr, nc, masked):
        rs = slice(r0, r0 + nr)

        def scores(h):
            sl = slice(h * QK_PAD, (h + 1) * QK_PAD)
            return _dot(q_ref[0, rs, sl], kt_ref[0, sl, :nc])

        if masked:
            rq = (lax.broadcasted_iota(jnp.int32, (nr, nc), 0) + r0) // CHUNK
            mask = lax.broadcasted_iota(jnp.int32, (nr, nc), 1) // CHUNK <= rq
        one = jnp.ones((), BF16)
        ahead = 2
        pending = [scores(h) for h in range(ahead)]
        for h in range(B_HEADS):
            s = pending.pop(0)
            if h + ahead < B_HEADS:
                pending.append(scores(h + ahead))
            vp = v_ref[0, :nc, (h // 2) * LANES:(h // 2 + 1) * LANES]
            vext = jnp.where(low, vp, one) if h % 2 == 0 else jnp.where(low, one, vp)
            if masked:
                s = jnp.where(mask, s, -jnp.inf)
            m_prev = m_ref[h, rs]
            m_new = jnp.maximum(m_prev, jnp.max(s, axis=-1, keepdims=True))
            p = jnp.exp2(s - m_new[:, :1])
            acc_ref[h, rs] = jnp.exp2(m_prev - m_new) * acc_ref[h, rs] + _dot(_bf(p), vext)
            m_ref[h, rs] = m_new

    @pl.when(ki < qi)
    def _():
        step(0, blk, blk, False)

    @pl.when(ki == qi)
    def _():
        half = blk // 2
        step(0, half, half, True)
        step(half, half, blk, True)
        for j in range(B_HEADS // 2):
            a0 = acc_ref[2 * j]
            a1 = acc_ref[2 * j + 1]
            num = jnp.where(low, a0, a1)
            den = jnp.where(low, pltpu.roll(a0, V_DIM, axis=1), pltpu.roll(a1, V_DIM, axis=1))
            o_ref[0, :, j * LANES:(j + 1) * LANES] = num / den


def _attn(q, kt, v, *, blk):
    nb, t, _ = q.shape
    steps = [(i, j) for i in range(t // blk) for j in range(i + 1)]
    qi = jnp.asarray([s[0] for s in steps], jnp.int32)
    ki = jnp.asarray([s[1] for s in steps], jnp.int32)
    grid_spec = pltpu.PrefetchScalarGridSpec(
        num_scalar_prefetch=2,
        grid=(nb, len(steps)),
        in_specs=[pl.BlockSpec((1, blk, B_HEADS * QK_PAD), lambda b, s, qi, ki: (b, qi[s], 0)),
                  pl.BlockSpec((1, B_HEADS * QK_PAD, blk), lambda b, s, qi, ki: (b, 0, ki[s])),
                  pl.BlockSpec((1, blk, D_B), lambda b, s, qi, ki: (b, ki[s], 0))],
        out_specs=pl.BlockSpec((1, blk, D_B), lambda b, s, qi, ki: (b, qi[s], 0)),
        scratch_shapes=[pltpu.VMEM((B_HEADS, blk, LANES), F32),
                        pltpu.VMEM((B_HEADS, blk, LANES), F32)])
    return pl.pallas_call(
        functools.partial(_attn_kernel, blk=blk),
        out_shape=jax.ShapeDtypeStruct((nb, t, D_B), F32),
        grid_spec=grid_spec,
        compiler_params=pltpu.CompilerParams(
            dimension_semantics=("arbitrary", "arbitrary"), vmem_limit_bytes=VMEM_LIMIT),
        name="attn",
    )(qi, ki, q, kt, v)


def _attnc_kernel(q_ref, cckv_ref, ckpe_ref, nckv_ref, nkpe_ref, wk_ref, wv_ref, place_ref, o_ref, *, t):
    q = q_ref[0]
    heads = [q[:, h * QK_PAD:(h + 1) * QK_PAD] for h in range(B_HEADS)]
    qf = jnp.concatenate(heads, axis=0)
    qa = jnp.concatenate([_bf(_dot_nt(heads[h], wk_ref[:, h * QK_PAD:(h + 1) * QK_PAD]))
                          for h in range(B_HEADS)], axis=0)
    place = place_ref[:, :QK_PAD]

    def scores(ckv, kpe):
        cb = _bf(ckv)
        return _dot_nt(qa, cb) + _dot_nt(qf, _bf(_dot(_bf(kpe), place))), cb

    s_c, cb_c = scores(cckv_ref[0], ckpe_ref[0])
    s_n, cb_n = scores(nckv_ref[...], nkpe_ref[...])
    m = jnp.maximum(jnp.max(s_c, axis=-1, keepdims=True), jnp.max(s_n, axis=-1, keepdims=True))
    p_c = jnp.exp2(s_c - m)
    p_n = jnp.exp2(s_n - m)
    den = jnp.sum(p_c, axis=-1, keepdims=True) + jnp.sum(p_n, axis=-1, keepdims=True)
    lat = _bf((_dot(_bf(p_c), cb_c) + _dot(_bf(p_n), cb_n)) / den)
    low = lax.broadcasted_iota(jnp.int32, (1, LANES), 1) < V_DIM
    zero = jnp.zeros((), BF16)
    for j in range(B_HEADS // 2):
        wvp = wv_ref[:, j * LANES:(j + 1) * LANES]
        o_ref[0, :, j * LANES:(j + 1) * LANES] = (
            _dot(lat[2 * j * t:(2 * j + 1) * t], jnp.where(low, wvp, zero))
            + _dot(lat[(2 * j + 1) * t:(2 * j + 2) * t], jnp.where(low, zero, wvp)))


def _attnc(q, cache_ckv, cache_kpe, ckv, kpe, w):
    nb, t, _ = q.shape
    past = cache_ckv.shape[1]
    return pl.pallas_call(
        functools.partial(_attnc_kernel, t=t),
        out_shape=jax.ShapeDtypeStruct((nb, t, D_B), F32),
        grid=(nb,),
        in_specs=[pl.BlockSpec((1, t, B_HEADS * QK_PAD), lambda b: (b, 0, 0)),
                  pl.BlockSpec((1, past, KV_LORA), lambda b: (b, 0, 0)),
                  pl.BlockSpec((1, past, ROPE_DIM), lambda b: (b, 0, 0)),
                  pl.BlockSpec((t, KV_LORA), lambda b: (b, 0)),
                  pl.BlockSpec((t, ROPE_DIM), lambda b: (b, 0)),
                  _full((KV_LORA, B_HEADS * QK_PAD)), _full((KV_LORA, D_B)),
                  _full((ROPE_DIM, B_HEADS * QK_PAD))],
        out_specs=pl.BlockSpec((1, t, D_B), lambda b: (b, 0, 0)),
        compiler_params=pltpu.CompilerParams(dimension_semantics=("arbitrary",)),
        name="attnc",
    )(q, cache_ckv, cache_kpe, ckv, kpe, w["wk"], w["wv"], w["place"])


def _wkv_kernel(rt_ref, kt_ref, bt_ref, at_ref, v_ref, g_ref, bonus_ref, wc_ref, lg_ref, lb_ref,
                eavg_ref, h0_ref, y_ref, hout_ref, h_ref, *, chunk, group, nsub, nsteps):
    c = pl.program_id(1)
    C2 = 2 * chunk
    npair = A_HEADS // 2

    @pl.when(c == 0)
    def _():
        h_ref[...] = h0_ref[...]

    low = lax.broadcasted_iota(jnp.int32, (chunk, LANES), 1) < A_HEAD_DIM
    ii = lax.broadcasted_iota(jnp.int32, (C2, C2), 0)
    jj = lax.broadcasted_iota(jnp.int32, (C2, C2), 1)
    strict = ii > jj
    incl = ii >= jj
    eye_c = (ii == jj).astype(F32)
    ki = lax.broadcasted_iota(jnp.int32, (LANES, LANES), 0)
    kj = lax.broadcasted_iota(jnp.int32, (LANES, LANES), 1)
    eye_k = (ki == kj).astype(F32)
    eavg = eavg_ref[...]

    def stack(t):
        return jnp.concatenate([jnp.where(low, t, 0.0), jnp.where(low, 0.0, t)], axis=0)

    units = [(ci, j) for ci in range(nsub * group) for j in range(npair)]
    rows = lambda ci: slice(ci * chunk, (ci + 1) * chunk)
    lanes = lambda j: slice(j * LANES, (j + 1) * LANES)
    ld = lambda ref: [stack(ref[rows(ci), lanes(j)]) for ci, j in units]
    At, Bt, Kt, Rt, Vs = ld(at_ref), ld(bt_ref), ld(kt_ref), ld(rt_ref), ld(v_ref)
    nu = range(len(units))
    Vb = [_bf(Vs[u]) for u in nu]
    g1 = [_dot_nt(_bf(jnp.concatenate([At[u], Rt[u]], axis=0)),
                  _bf(jnp.concatenate([Bt[u], Kt[u]], axis=0))) for u in nu]
    Aab = [jnp.where(strict, g1[u][:C2, :C2], 0.0) for u in nu]
    Aak = [_bf(jnp.where(strict, g1[u][:C2, C2:], 0.0)) for u in nu]
    Arb = [_bf(jnp.where(incl, g1[u][C2:, :C2], 0.0)) for u in nu]
    Ark = [_bf(jnp.where(incl, g1[u][C2:, C2:], 0.0)) for u in nu]
    rcat = lambda *t: jnp.concatenate(t, axis=0)
    Tm = [eye_c + Aab[u] for u in nu]
    Nb = [_bf(Aab[u]) for u in nu]
    Pw = [_dot(Nb[u], Nb[u]) for u in nu]
    for _ in range(chunk.bit_length() - 3):
        Pb = [_bf(Pw[u]) for u in nu]
        st = [_dot(rcat(_bf(Tm[u]), Pb[u]), Pb[u]) for u in nu]
        Tm = [Tm[u] + st[u][:C2] for u in nu]
        Pw = [st[u][C2:] for u in nu]
    Tm = [Tm[u] + _dot(_bf(Tm[u]), _bf(Pw[u])) for u in nu]
    wrow = [wc_ref[ci * SUBLANES:ci * SUBLANES + 1, lanes(j)] for ci, j in units]
    BwT = [_bf((Bt[u] * wrow[u]).T) for u in nu]
    KwT = [_bf((Kt[u] * wrow[u]).T) for u in nu]
    sv = [_dot(rcat(Aak[u], Ark[u], KwT[u]), Vb[u]) for u in nu]
    PPb = [_bf(_dot(_bf(Tm[u]), _bf(jnp.concatenate([At[u], sv[u][:C2]], axis=1)))) for u in nu]
    sp = [_dot(rcat(Arb[u], BwT[u]), PPb[u]) for u in nu]
    Q1M1, Q2, M2 = [], [], []
    for u in nu:
        q1s = Rt[u] + sp[u][:C2, :LANES]
        q2s = sp[u][:C2, LANES:] + sv[u][C2:2 * C2]
        m1 = eye_k * wrow[u] + sp[u][C2:, :LANES]
        Q1M1.append(_bf(rcat(q1s[:chunk] + q1s[chunk:], m1)))
        Q2.append(q2s[:chunk] + q2s[chunk:])
        M2.append(sp[u][C2:, LANES:] + sv[u][2 * C2:])

    H = {(si, j): h_ref[si, j] for si in range(nsub) for j in range(npair)}
    Y = []
    for u, (ci, j) in enumerate(units):
        key = (ci // group, j)
        sh = _dot(Q1M1[u], _bf(H[key]))
        Y.append(sh[:chunk] + Q2[u])
        H[key] = sh[chunk:] + M2[u]
    for (si, j), val in H.items():
        h_ref[si, j] = val

    def headmean(t):
        hi, lo = _split2(t)
        m = _dot(rcat(hi, lo), eavg)
        return m[:chunk] + m[chunk:]

    mu = [headmean(Y[u]) for u in nu]
    dv = [Y[u] - mu[u] for u in nu]
    var = [headmean(dv[u] * dv[u]) for u in nu]
    for u, (ci, j) in enumerate(units):
        yn = dv[u] * lax.rsqrt(var[u] + LNX_EPS) * lg_ref[:, lanes(j)] + lb_ref[:, lanes(j)]
        y_ref[rows(ci), lanes(j)] = (yn + bonus_ref[rows(ci), lanes(j)]) * g_ref[rows(ci), lanes(j)]

    @pl.when(c == nsteps - 1)
    def _():
        hout_ref[...] = h_ref[...]


def _wkv(rt, kt, bt, at, v, g, bonus, wc, h0, w, *, nstreams, ncs, chunk):
    total = rt.shape[0]
    group = min(WKV_UNITS, ncs)
    nsub = min(WKV_UNITS // group, nstreams)
    nsteps = ncs // group
    blk = pl.BlockSpec((nsub * group * chunk, D_A), lambda s, c: (s * nsteps + c, 0))
    hspec = pl.BlockSpec((nsub, A_HEADS // 2, LANES, LANES), lambda s, c: (s, 0, 0, 0))
    return pl.pallas_call(
        functools.partial(_wkv_kernel, chunk=chunk, group=group, nsub=nsub, nsteps=nsteps),
        out_shape=[jax.ShapeDtypeStruct((total, D_A), F32),
                   jax.ShapeDtypeStruct((nstreams, A_HEADS // 2, LANES, LANES), F32)],
        grid=(nstreams // nsub, nsteps),
        in_specs=[blk] * 7 + [pl.BlockSpec((nsub * group * SUBLANES, D_A), lambda s, c: (s * nsteps + c, 0)),
                              _full((1, D_A)), _full((1, D_A)), _full((LANES, LANES)), hspec],
        out_specs=[blk, hspec],
        scratch_shapes=[pltpu.VMEM((nsub, A_HEADS // 2, LANES, LANES), F32)],
        compiler_params=pltpu.CompilerParams(
            dimension_semantics=("arbitrary", "arbitrary"), vmem_limit_bytes=VMEM_LIMIT),
        name="wkv",
    )(rt, kt, bt, at, v, g, bonus, wc, w["lnx_g"], w["lnx_b"], w["eavg"], h0)


def _layer_norm(t, g, b):
    mu = jnp.mean(t, axis=-1, keepdims=True)
    d = t - mu
    var = jnp.mean(d * d, axis=-1, keepdims=True)
    return d * lax.rsqrt(var + LN_EPS) * g + b


def _tail_kernel(x_ref, ya_ref, yb_ref, wg_ref, bg_ref, wpa_ref, wpb_ref, wo_ref, l1g_ref, l1b_ref,
                 wgu_ref, wdown_ref, l2g_ref, l2b_ref, o_ref):
    x = x_ref[...]
    gates = _sigmoid(_dot(_bf(x), wg_ref[...]) + bg_ref[...])
    m = (gates[:, :D_MODEL] * _dot(_bf(ya_ref[...]), wpa_ref[...])
         + gates[:, D_MODEL:] * _dot(_bf(yb_ref[...]), wpb_ref[...]))
    h = _layer_norm(DN_ALPHA * x + _dot(_bf(m), wo_ref[...]), l1g_ref[...], l1b_ref[...])
    hb = _bf(h)

    def gate_up(c):
        cols = slice(c * FF_CHUNK, (c + 1) * FF_CHUNK)
        ucols = slice(D_FF + c * FF_CHUNK, D_FF + (c + 1) * FF_CHUNK)
        return _dot(hb, wgu_ref[:, cols]), _dot(hb, wgu_ref[:, ucols])

    f = None
    nxt = gate_up(0)
    for c in range(N_FF):
        gate, up = nxt
        if c + 1 < N_FF:
            nxt = gate_up(c + 1)
        d = _dot(_bf(gate * _sigmoid(gate) * up), wdown_ref[c * FF_CHUNK:(c + 1) * FF_CHUNK, :])
        f = d if f is None else f + d
    o_ref[...] = _layer_norm(DN_ALPHA * h + f, l2g_ref[...], l2b_ref[...])


def _tail(x2, ya, yb, w, *, rows):
    total = x2.shape[0]
    rowblk = lambda n: pl.BlockSpec((rows, n), lambda i: (i, 0))

    def const(shape):
        n = len(shape)
        return pl.BlockSpec(shape, lambda i: (0,) * n, pipeline_mode=pl.Buffered(1))

    return pl.pallas_call(
        _tail_kernel,
        out_shape=jax.ShapeDtypeStruct((total, D_MODEL), F32),
        grid=(total // rows,),
        in_specs=[rowblk(D_MODEL), rowblk(D_A), rowblk(D_B),
                  const((D_MODEL, 2 * D_MODEL)), const((1, 2 * D_MODEL)),
                  const((D_A, D_MODEL)), const((D_B, D_MODEL)), const((D_MODEL, D_MODEL)),
                  const((1, D_MODEL)), const((1, D_MODEL)),
                  const((D_MODEL, 2 * D_FF)), const((D_FF, D_MODEL)),
                  const((1, D_MODEL)), const((1, D_MODEL))],
        out_specs=rowblk(D_MODEL),
        compiler_params=pltpu.CompilerParams(
            dimension_semantics=("arbitrary",), vmem_limit_bytes=VMEM_LIMIT),
        name="tail",
    )(x2, ya, yb, w["wg"], w["bg"], w["wpa"], w["wpb"], w["wo"], w["l1g"], w["l1b"],
      w["wgu"], w["wdown"], w["l2g"], w["l2b"])


def _prep_weights(w_in, mu_shift, w0, w_w2, a0, w_a2, w_g2, k_k, k_a, r_k, lnx_g, lnx_b, w_pa,
                  q_norm_g, w_uq, kv_norm_g, w_ukv, w_pb, b_gate, w_o, ln1_g, ln1_b, w_gu, w_down,
                  ln2_g, ln2_b):
    row = lambda t: t.reshape(1, -1).astype(F32)
    nb = A_COLS + B_COLS
    pe = w_in[:, nb - ROPE_DIM:nb]
    half = ROPE_DIM // 2
    pe_sw = jnp.concatenate([pe[:, half:], pe[:, :half]], axis=1)
    wall = jnp.concatenate([w_in[:, :nb - ROPE_DIM], pe, pe_sw,
                            jnp.zeros((D_MODEL, LANES - 2 * ROPE_DIM), F32)], axis=1)
    uq = w_uq.reshape(Q_LORA, B_HEADS, NOPE_DIM + ROPE_DIM)
    nope, r1, r2 = uq[..., :NOPE_DIM], uq[..., NOPE_DIM:NOPE_DIM + half], uq[..., NOPE_DIM + half:]
    zpad = jnp.zeros((Q_LORA, B_HEADS, QK_PAD - NOPE_DIM - ROPE_DIM), F32)
    wqa = jnp.concatenate([nope, r1, r2, zpad], axis=-1).reshape(Q_LORA, B_HEADS * QK_PAD)
    wqb = jnp.concatenate([jnp.zeros_like(nope), r2, r1, zpad], axis=-1).reshape(Q_LORA, B_HEADS * QK_PAD)
    ukv = w_ukv.reshape(KV_LORA, B_HEADS, NOPE_DIM + V_DIM)
    wk = jnp.concatenate([ukv[..., :NOPE_DIM], jnp.zeros((KV_LORA, B_HEADS, QK_PAD - NOPE_DIM), F32)],
                         axis=-1).reshape(KV_LORA, B_HEADS * QK_PAD)
    wv = ukv[..., NOPE_DIM:].reshape(KV_LORA, D_B)
    place = np.zeros((ROPE_DIM, B_HEADS * QK_PAD), np.float32)
    for h in range(B_HEADS):
        place[np.arange(ROPE_DIM), h * QK_PAD + NOPE_DIM + np.arange(ROPE_DIM)] = 1.0
    hid = np.arange(D_A) // A_HEAD_DIM
    esum = (hid[:, None] == hid[None, :]).astype(np.float32)
    lid = np.arange(LANES) // A_HEAD_DIM
    eavg = (lid[:, None] == lid[None, :]).astype(np.float32) / A_HEAD_DIM
    zl = jnp.zeros((LANES - DECAY_LORA, D_A), F32)
    return {
        "wall": _bf(wall), "mu": row(mu_shift), "w0": row(w0), "a0": row(a0), "k_k": row(k_k),
        "k_a": row(k_a), "r_k": row(r_k),
        "ww2": _bf(jnp.concatenate([w_w2, zl], axis=0)), "wa2": _bf(jnp.concatenate([zl, w_a2], axis=0)),
        "wg2": _bf(w_g2), "esum": jnp.asarray(esum, BF16), "eavg": jnp.asarray(eavg, BF16),
        "qg": row(q_norm_g), "kvg": row(kv_norm_g), "wqa": _bf(wqa), "wqb": _bf(wqb),
        "wk": _bf(wk), "wv": _bf(wv), "place": jnp.asarray(place, BF16),
        "wkt": _bf(wk.T), "placet": jnp.asarray(place.T, BF16),
        "lnx_g": row(lnx_g), "lnx_b": row(lnx_b),
        "wg": _bf(w_in[:, nb:]), "bg": row(b_gate), "wpa": _bf(w_pa), "wpb": _bf(w_pb), "wo": _bf(w_o),
        "l1g": row(ln1_g), "l1b": row(ln1_b), "l2g": row(ln2_g), "l2b": row(ln2_b),
        "wgu": _bf(w_gu), "wdown": _bf(w_down),
    }


def _rope_table(pos0, t, reps):
    half = ROPE_DIM // 2
    inv = ROPE_BASE ** (-jnp.arange(half, dtype=F32) / half)
    ang = (pos0 + jnp.arange(t)).astype(F32)[:, None] * jnp.tile(inv, LANES // half)[None, :]
    grp = np.arange(LANES) // half
    sc = np.where(grp >= 4, SCORE_SCALE, 1.0)
    mc = jnp.asarray(np.where(grp % 4 < 2, sc, 0.0), F32)
    ms = jnp.asarray(np.where(grp % 4 == 2, -sc, np.where(grp % 4 == 3, sc, 0.0)), F32)
    return jnp.tile(jnp.cos(ang) * mc + jnp.sin(ang) * ms, (reps, 1))


def _state_to_pairs(s):
    nb = s.shape[0]
    st = jnp.swapaxes(s, -1, -2).reshape(nb, A_HEADS // 2, 2, A_HEAD_DIM, A_HEAD_DIM)
    z = jnp.zeros_like(st[:, :, 0])
    top = jnp.concatenate([st[:, :, 0], z], axis=-1)
    bot = jnp.concatenate([z, st[:, :, 1]], axis=-1)
    return jnp.concatenate([top, bot], axis=-2)


def _pairs_to_state(hp):
    nb = hp.shape[0]
    diag = jnp.stack([hp[:, :, :A_HEAD_DIM, :A_HEAD_DIM], hp[:, :, A_HEAD_DIM:, A_HEAD_DIM:]], axis=2)
    return jnp.swapaxes(diag.reshape(nb, A_HEADS, A_HEAD_DIM, A_HEAD_DIM), -1, -2)


def _layer(x, pos0, shift0, wkv0, cache, w):
    nstreams, t, _ = x.shape
    total = nstreams * t
    x2 = x.reshape(total, D_MODEL)
    chunk = min(CHUNK, t)
    rope = _rope_table(pos0, t, nstreams)
    (rt, kt, bt, at, v, g, bonus, wc, shift, q, ckv, kpe, *kv) = _proj(
        x2, shift0, w, rope, nstreams=nstreams, seg=t, rows=min(PROJ_ROWS, total), chunk=chunk,
        expand_kv=cache is None)

    ya, hout = _wkv(rt, kt, bt, at, v, g, bonus, wc, _state_to_pairs(wkv0), w,
                    nstreams=nstreams, ncs=t // chunk, chunk=chunk)

    if cache is None:
        kk, vv = kv
        yb = _attn(q.reshape(nstreams, t, -1), kk.reshape(nstreams, -1, t), vv.reshape(nstreams, t, -1),
                   blk=min(ATTN_BLOCK, t))
    else:
        yb = _attnc(q.reshape(nstreams, t, -1), cache[0], cache[1], ckv, kpe, w)

    y = _tail(x2, ya, yb.reshape(total, D_B), w, rows=min(TAIL_ROWS, total))
    return (y.reshape(nstreams, t, D_MODEL), ckv.reshape(nstreams, t, KV_LORA),
            kpe.reshape(nstreams, t, ROPE_DIM), _pairs_to_state(hout), shift)


def kernel(x_prompt, x_sample, cache_ckv, cache_kpe, state_wkv, state_shift, w_in, mu_shift, w0, w_w2, a0,
           w_a2, w_g2, k_k, k_a, r_k, lnx_g, lnx_b, w_pa, q_norm_g, w_uq, kv_norm_g, w_ukv, w_pb, b_gate,
           w_o, ln1_g, ln1_b, w_gu, w_down, ln2_g, ln2_b):
    w = _prep_weights(w_in, mu_shift, w0, w_w2, a0, w_a2, w_g2, k_k, k_a, r_k, lnx_g, lnx_b, w_pa,
                      q_norm_g, w_uq, kv_norm_g, w_ukv, w_pb, b_gate, w_o, ln1_g, ln1_b, w_gu, w_down,
                      ln2_g, ln2_b)
    bp = x_prompt.shape[0]
    y_p, ckv_p, kpe_p, wkv_p, shift_p = _layer(
        x_prompt, 0, jnp.zeros((bp, 1, A_COLS), F32),
        jnp.zeros((bp, A_HEADS, A_HEAD_DIM, A_HEAD_DIM), F32), None, w)
    y_s, ckv_s, kpe_s, wkv_s, shift_s = _layer(
        x_sample, cache_ckv.shape[1], state_shift, state_wkv, (cache_ckv, cache_kpe), w)
    return (y_p, y_s, ckv_p, kpe_p, wkv_p, shift_p, ckv_s, kpe_s, wkv_s, shift_s)
```

```python
import functools

import numpy as np
import jax
import jax.numpy as jnp
from jax import lax
from jax.experimental import pallas as pl
from jax.experimental.pallas import tpu as pltpu

D_MODEL = 1024
CHUNK = 64
A_HEADS = 8
A_HEAD_DIM = 64
D_A = 512
DECAY_LORA = 64
AAA_LORA = 64
GATE_LORA = 128
A_COLS = 3 * D_A + DECAY_LORA + AAA_LORA + GATE_LORA
LNX_EPS = A_HEAD_DIM * 1e-5
B_HEADS = 8
Q_LORA = 256
KV_LORA = 128
NOPE_DIM = 64
ROPE_DIM = 32
V_DIM = 64
D_B = 512
B_COLS = Q_LORA + KV_LORA + ROPE_DIM
ROPE_BASE = 10000.0
ATTN_SCALE = (NOPE_DIM + ROPE_DIM) ** -0.5
SCORE_SCALE = ATTN_SCALE * float(np.log2(np.e))
RMS_EPS = 1e-6
D_FF = 2816
LN_EPS = 1e-5
DN_ALPHA = 2.0 ** 0.25

LANES = 128
SUBLANES = 8
QK_PAD = 128
PROJ_COLS = A_COLS + Q_LORA + KV_LORA + LANES
FF_CHUNK = 256
N_FF = D_FF // FF_CHUNK
TRI_ROWS = 256
PROJ_ROWS = 512
TAIL_ROWS = 512
ATTN_BLOCK = 1024
WKV_UNITS = 4
VMEM_LIMIT = 56 * 1024 * 1024

F32 = jnp.float32
BF16 = jnp.bfloat16


def _dot(a, b):
    return jnp.dot(a, b, preferred_element_type=F32)


def _dot_nt(a, b):
    return lax.dot_general(a, b, (((1,), (1,)), ((), ())), preferred_element_type=F32)


def _bf(x):
    return x.astype(BF16)


def _split2(x):
    hi = x.astype(BF16)
    lo = (x - hi.astype(F32)).astype(BF16)
    return hi, lo


def _split3(x):
    h1 = x.astype(BF16)
    r1 = x - h1.astype(F32)
    h2 = r1.astype(BF16)
    h3 = (r1 - h2.astype(F32)).astype(BF16)
    return h1, h2, h3


def _sigmoid(z):
    return 1.0 / (1.0 + jnp.exp(-z))


def _full(shape):
    n = len(shape)
    return pl.BlockSpec(shape, lambda *_: (0,) * n)


def _proj_kernel(x_ref, shift0_ref, wall_ref, mu_ref, w0_ref, a0_ref, kk_ref, ka_ref, rk_ref,
                 ww2_ref, wa2_ref, wg2_ref, esum_ref, ltri_ref, qg_ref, kvg_ref, wqa_ref, wqb_ref,
                 rope_ref, wk_ref, place_ref, wv_ref,
                 rt_ref, kt_ref, bt_ref, at_ref, v_ref, g_ref, bonus_ref, wc_ref, shift_ref,
                 q_ref, ckv_ref, kpe_ref, *rest, rows, chunk, seg):
    *kv_out, carry_ref = rest
    b = pl.program_id(1)
    xb = _bf(x_ref[...])

    pa = _dot(xb, wall_ref[:, :A_COLS])
    pq = _dot(xb, wall_ref[:, A_COLS:A_COLS + Q_LORA])
    pkv = _dot(xb, wall_ref[:, A_COLS + Q_LORA:A_COLS + Q_LORA + KV_LORA])
    ppe = _dot(xb, wall_ref[:, A_COLS + Q_LORA + KV_LORA:])

    cqn = _bf(pq * lax.rsqrt(jnp.mean(pq * pq, axis=-1, keepdims=True) + RMS_EPS) * qg_ref[...])
    qa = _dot(cqn, wqa_ref[...])
    qb = _dot(cqn, wqb_ref[...])
    rope = rope_ref[...]
    lane = lax.broadcasted_iota(jnp.int32, (1, QK_PAD), 1)
    cq = jnp.where(lane < NOPE_DIM, np.float32(SCORE_SCALE), rope)
    sq = pltpu.roll(rope, QK_PAD - ROPE_DIM, axis=1)
    for h in range(B_HEADS):
        sl = slice(h * QK_PAD, (h + 1) * QK_PAD)
        q_ref[:, sl] = _bf(qa[:, sl] * cq + qb[:, sl] * sq)
    ckv = pkv * lax.rsqrt(jnp.mean(pkv * pkv, axis=-1, keepdims=True) + RMS_EPS) * kvg_ref[...]
    ckv_ref[...] = ckv
    ppe = ppe * rope
    kpe = ppe[:, :ROPE_DIM] + ppe[:, ROPE_DIM:2 * ROPE_DIM]
    kpe_ref[...] = kpe
    if kv_out:
        kx_ref, vx_ref = kv_out
        cb = _bf(ckv)
        kx_ref[...] = _bf(_dot_nt(wk_ref[...], cb) + _dot_nt(place_ref[...], _bf(kpe)))
        vx_ref[...] = _bf(_dot(cb, wv_ref[...]))

    row = lax.broadcasted_iota(jnp.int32, (rows, 1), 0)
    if seg >= rows:
        first = jnp.where(b == 0, shift0_ref[0], carry_ref[...])
        starts = row == 0
        last = pa[rows - 1:rows, :]
        carry_ref[...] = last
        shift_ref[0] = last
    else:
        first = jnp.broadcast_to(shift0_ref[...], (rows // seg, seg, A_COLS)).reshape(rows, A_COLS)
        starts = row % seg == 0
        for s in range(rows // seg):
            shift_ref[s] = pa[(s + 1) * seg - 1:(s + 1) * seg, :]
    prev = jnp.where(starts, first, pltpu.roll(pa, 1, axis=0))
    xs = pa + (prev - pa) * mu_ref[...]

    r = xs[:, :D_A]
    k = xs[:, D_A:2 * D_A]
    v = xs[:, 2 * D_A:3 * D_A]
    wa = xs[:, 3 * D_A:3 * D_A + LANES]
    gd = xs[:, 3 * D_A + LANES:]

    z = w0_ref[...] + _dot(_bf(jnp.tanh(wa)), ww2_ref[...])
    ld = -np.float32(np.exp(-0.5)) * _sigmoid(z)
    a = _sigmoid(a0_ref[...] + _dot(_bf(wa), wa2_ref[...]))
    g_ref[...] = _dot(_bf(_sigmoid(gd)), wg2_ref[...])

    kkr = k * kk_ref[...]
    kh = k * (1.0 + (a - 1.0) * ka_ref[...])
    hi, lo = _split2(kkr * kkr)
    hs = _dot(jnp.concatenate([hi, lo, _bf(r * kh * rk_ref[...])], axis=0), esum_ref[...])
    kk = kkr / jnp.maximum(jnp.sqrt(hs[:rows] + hs[rows:2 * rows]), 1e-12)
    bonus_ref[...] = hs[2 * rows:] * v
    v_ref[...] = v

    ltri = ltri_ref[...]
    tri = ltri.shape[0]
    h1, h2, h3 = _split3(ld)
    cum = jnp.concatenate(
        [_dot(ltri, h1[i:i + tri]) + _dot(ltri, h2[i:i + tri]) + _dot(ltri, h3[i:i + tri])
         for i in range(0, rows, tri)], axis=0)
    ep = jnp.exp(cum)
    em = jnp.exp(-cum)
    rt_ref[...] = r * ep
    kt_ref[...] = kh * em
    bt_ref[...] = (kk * a) * em
    at_ref[...] = -kk * jnp.exp(cum - ld)
    for c in range(rows // chunk):
        wc_ref[c * SUBLANES:(c + 1) * SUBLANES, :] = jnp.broadcast_to(
            ep[(c + 1) * chunk - 1:(c + 1) * chunk, :], (SUBLANES, D_A))


def _proj(x2, shift0, w, rope, *, nstreams, seg, rows, chunk, expand_kv):
    total = nstreams * seg
    spb = max(1, rows // seg)
    bps = max(1, seg // rows)
    nck = rows // chunk
    tri = min(rows, TRI_ROWS)
    ltri = _chunk_tri(tri, chunk)
    rowblk = lambda n: pl.BlockSpec((rows, n), lambda s, b: (s * bps + b, 0))
    in_specs = [
        rowblk(D_MODEL),
        pl.BlockSpec((spb, 1, A_COLS), lambda s, b: (s, 0, 0)),
        _full((D_MODEL, PROJ_COLS)),
        _full((1, A_COLS)), _full((1, D_A)), _full((1, D_A)), _full((1, D_A)), _full((1, D_A)), _full((1, D_A)),
        _full((LANES, D_A)), _full((LANES, D_A)), _full((GATE_LORA, D_A)),
        _full((D_A, D_A)), _full((tri, tri)),
        _full((1, Q_LORA)), _full((1, KV_LORA)),
        _full((Q_LORA, B_HEADS * QK_PAD)), _full((Q_LORA, B_HEADS * QK_PAD)),
        rowblk(LANES),
        _full((B_HEADS * QK_PAD, KV_LORA)), _full((B_HEADS * QK_PAD, ROPE_DIM)), _full((KV_LORA, D_B)),
    ]
    f32o = lambda n: jax.ShapeDtypeStruct((total, n), F32)
    out_shape = [f32o(D_A)] * 7 + [
        jax.ShapeDtypeStruct((total // chunk * SUBLANES, D_A), F32),
        jax.ShapeDtypeStruct((nstreams, 1, A_COLS), F32),
        jax.ShapeDtypeStruct((total, B_HEADS * QK_PAD), BF16),
        f32o(KV_LORA), f32o(ROPE_DIM),
    ]
    out_specs = [rowblk(D_A)] * 7 + [
        pl.BlockSpec((nck * SUBLANES, D_A), lambda s, b: (s * bps + b, 0)),
        pl.BlockSpec((spb, 1, A_COLS), lambda s, b: (s, 0, 0)),
        rowblk(B_HEADS * QK_PAD), rowblk(KV_LORA), rowblk(ROPE_DIM),
    ]
    if expand_kv:
        out_shape += [jax.ShapeDtypeStruct((B_HEADS * QK_PAD, total), BF16),
                      jax.ShapeDtypeStruct((total, D_B), BF16)]
        out_specs += [pl.BlockSpec((B_HEADS * QK_PAD, rows), lambda s, b: (0, s * bps + b)), rowblk(D_B)]
    return pl.pallas_call(
        functools.partial(_proj_kernel, rows=rows, chunk=chunk, seg=seg),
        out_shape=out_shape,
        grid=(nstreams // spb, bps),
        in_specs=in_specs,
        out_specs=out_specs,
        scratch_shapes=[pltpu.VMEM((1, A_COLS), F32)],
        compiler_params=pltpu.CompilerParams(
            dimension_semantics=("arbitrary", "arbitrary"), vmem_limit_bytes=VMEM_LIMIT),
        name="proj",
    )(x2, shift0, w["wall"], w["mu"], w["w0"], w["a0"], w["k_k"], w["k_a"], w["r_k"],
      w["ww2"], w["wa2"], w["wg2"], w["esum"], ltri, w["qg"], w["kvg"], w["wqa"], w["wqb"],
      rope, w["wkt"], w["placet"], w["wv"])


def _chunk_tri(rows, chunk):
    i = np.arange(rows)
    m = (i[:, None] // chunk == i[None, :] // chunk) & (i[None, :] <= i[:, None])
    return jnp.asarray(m, BF16)


def _attn_kernel(qi_ref, ki_ref, q_ref, kt_ref, v_ref, o_ref, m_ref, acc_ref, *, blk):
    s_id = pl.program_id(1)
    qi = qi_ref[s_id]
    ki = ki_ref[s_id]

    @pl.when(ki == 0)
    def _():
        m_ref[...] = jnp.full(m_ref.shape, -jnp.inf, F32)
        acc_ref[...] = jnp.zeros(acc_ref.shape, F32)

    low = lax.broadcasted_iota(jnp.int32, (1, LANES), 1) < V_DIM

    def step(r0, nr, nc, masked):
        rs = slice(r0, r0 + nr)

        def scores(h):
            sl = slice(h * QK_PAD, (h + 1) * QK_PAD)
            return _dot(q_ref[0, rs, sl], kt_ref[0, sl, :nc])

        if masked:
            rq = (lax.broadcasted_iota(jnp.int32, (nr, nc), 0) + r0) // CHUNK
            mask = lax.broadcasted_iota(jnp.int32, (nr, nc), 1) // CHUNK <= rq
        one = jnp.ones((), BF16)
        ahead = 2
        pending = [scores(h) for h in range(ahead)]
        for h in range(B_HEADS):
            s = pending.pop(0)
            if h + ahead < B_HEADS:
                pending.append(scores(h + ahead))
            vp = v_ref[0, :nc, (h // 2) * LANES:(h // 2 + 1) * LANES]
            vext = jnp.where(low, vp, one) if h % 2 == 0 else jnp.where(low, one, vp)
            if masked:
                s = jnp.where(mask, s, -jnp.inf)
            m_prev = m_ref[h, rs]
            m_new = jnp.maximum(m_prev, jnp.max(s, axis=-1, keepdims=True))
            p = jnp.exp2(s - m_new[:, :1])
            acc_ref[h, rs] = jnp.exp2(m_prev - m_new) * acc_ref[h, rs] + _dot(_bf(p), vext)
            m_ref[h, rs] = m_new

    @pl.when(ki < qi)
    def _():
        step(0, blk, blk, False)

    @pl.when(ki == qi)
    def _():
        half = blk // 2
        step(0, half, half, True)
        step(half, half, blk, True)
        for j in range(B_HEADS // 2):
            a0 = acc_ref[2 * j]
            a1 = acc_ref[2 * j + 1]
            num = jnp.where(low, a0, a1)
            den = jnp.where(low, pltpu.roll(a0, V_DIM, axis=1), pltpu.roll(a1, V_DIM, axis=1))
            o_ref[0, :, j * LANES:(j + 1) * LANES] = num / den


def _attn(q, kt, v, *, blk):
    nb, t, _ = q.shape
    steps = [(i, j) for i in range(t // blk) for j in range(i + 1)]
    qi = jnp.asarray([s[0] for s in steps], jnp.int32)
    ki = jnp.asarray([s[1] for s in steps], jnp.int32)
    grid_spec = pltpu.PrefetchScalarGridSpec(
        num_scalar_prefetch=2,
        grid=(nb, len(steps)),
        in_specs=[pl.BlockSpec((1, blk, B_HEADS * QK_PAD), lambda b, s, qi, ki: (b, qi[s], 0)),
                  pl.BlockSpec((1, B_HEADS * QK_PAD, blk), lambda b, s, qi, ki: (b, 0, ki[s])),
                  pl.BlockSpec((1, blk, D_B), lambda b, s, qi, ki: (b, ki[s], 0))],
        out_specs=pl.BlockSpec((1, blk, D_B), lambda b, s, qi, ki: (b, qi[s], 0)),
        scratch_shapes=[pltpu.VMEM((B_HEADS, blk, LANES), F32),
                        pltpu.VMEM((B_HEADS, blk, LANES), F32)])
    return pl.pallas_call(
        functools.partial(_attn_kernel, blk=blk),
        out_shape=jax.ShapeDtypeStruct((nb, t, D_B), F32),
        grid_spec=grid_spec,
        compiler_params=pltpu.CompilerParams(
            dimension_semantics=("arbitrary", "arbitrary"), vmem_limit_bytes=VMEM_LIMIT),
        name="attn",
    )(qi, ki, q, kt, v)


def _attnc_kernel(q_ref, cckv_ref, ckpe_ref, nckv_ref, nkpe_ref, wk_ref, wv_ref, place_ref, o_ref, *, t):
    q = q_ref[0]
    heads = [q[:, h * QK_PAD:(h + 1) * QK_PAD] for h in range(B_HEADS)]
    qf = jnp.concatenate(heads, axis=0)
    qa = jnp.concatenate([_bf(_dot_nt(heads[h], wk_ref[:, h * QK_PAD:(h + 1) * QK_PAD]))
                          for h in range(B_HEADS)], axis=0)
    place = place_ref[:, :QK_PAD]

    def scores(ckv, kpe):
        cb = _bf(ckv)
        return _dot_nt(qa, cb) + _dot_nt(qf, _bf(_dot(_bf(kpe), place))), cb

    s_c, cb_c = scores(cckv_ref[0], ckpe_ref[0])
    s_n, cb_n = scores(nckv_ref[...], nkpe_ref[...])
    m = jnp.maximum(jnp.max(s_c, axis=-1, keepdims=True), jnp.max(s_n, axis=-1, keepdims=True))
    p_c = jnp.exp2(s_c - m)
    p_n = jnp.exp2(s_n - m)
    den = jnp.sum(p_c, axis=-1, keepdims=True) + jnp.sum(p_n, axis=-1, keepdims=True)
    lat = _bf((_dot(_bf(p_c), cb_c) + _dot(_bf(p_n), cb_n)) / den)
    low = lax.broadcasted_iota(jnp.int32, (1, LANES), 1) < V_DIM
    zero = jnp.zeros((), BF16)
    for j in range(B_HEADS // 2):
        wvp = wv_ref[:, j * LANES:(j + 1) * LANES]
        o_ref[0, :, j * LANES:(j + 1) * LANES] = (
            _dot(lat[2 * j * t:(2 * j + 1) * t], jnp.where(low, wvp, zero))
            + _dot(lat[(2 * j + 1) * t:(2 * j + 2) * t], jnp.where(low, zero, wvp)))


def _attnc(q, cache_ckv, cache_kpe, ckv, kpe, w):
    nb, t, _ = q.shape
    past = cache_ckv.shape[1]
    return pl.pallas_call(
        functools.partial(_attnc_kernel, t=t),
        out_shape=jax.ShapeDtypeStruct((nb, t, D_B), F32),
        grid=(nb,),
        in_specs=[pl.BlockSpec((1, t, B_HEADS * QK_PAD), lambda b: (b, 0, 0)),
                  pl.BlockSpec((1, past, KV_LORA), lambda b: (b, 0, 0)),
                  pl.BlockSpec((1, past, ROPE_DIM), lambda b: (b, 0, 0)),
                  pl.BlockSpec((t, KV_LORA), lambda b: (b, 0)),
                  pl.BlockSpec((t, ROPE_DIM), lambda b: (b, 0)),
                  _full((KV_LORA, B_HEADS * QK_PAD)), _full((KV_LORA, D_B)),
                  _full((ROPE_DIM, B_HEADS * QK_PAD))],
        out_specs=pl.BlockSpec((1, t, D_B), lambda b: (b, 0, 0)),
        compiler_params=pltpu.CompilerParams(dimension_semantics=("arbitrary",)),
        name="attnc",
    )(q, cache_ckv, cache_kpe, ckv, kpe, w["wk"], w["wv"], w["place"])


def _wkv_kernel(rt_ref, kt_ref, bt_ref, at_ref, v_ref, g_ref, bonus_ref, wc_ref, lg_ref, lb_ref,
                eavg_ref, h0_ref, y_ref, hout_ref, h_ref, *, chunk, group, nsub, nsteps):
    c = pl.program_id(1)
    C2 = 2 * chunk
    npair = A_HEADS // 2

    @pl.when(c == 0)
    def _():
        h_ref[...] = h0_ref[...]

    low = lax.broadcasted_iota(jnp.int32, (chunk, LANES), 1) < A_HEAD_DIM
    ii = lax.broadcasted_iota(jnp.int32, (C2, C2), 0)
    jj = lax.broadcasted_iota(jnp.int32, (C2, C2), 1)
    strict = ii > jj
    incl = ii >= jj
    eye_c = (ii == jj).astype(F32)
    ki = lax.broadcasted_iota(jnp.int32, (LANES, LANES), 0)
    kj = lax.broadcasted_iota(jnp.int32, (LANES, LANES), 1)
    eye_k = (ki == kj).astype(F32)
    eavg = eavg_ref[...]

    def stack(t):
        return jnp.concatenate([jnp.where(low, t, 0.0), jnp.where(low, 0.0, t)], axis=0)

    units = [(ci, j) for ci in range(nsub * group) for j in range(npair)]
    rows = lambda ci: slice(ci * chunk, (ci + 1) * chunk)
    lanes = lambda j: slice(j * LANES, (j + 1) * LANES)
    ld = lambda ref: [stack(ref[rows(ci), lanes(j)]) for ci, j in units]
    At, Bt, Kt, Rt, Vs = ld(at_ref), ld(bt_ref), ld(kt_ref), ld(rt_ref), ld(v_ref)
    nu = range(len(units))
    Vb = [_bf(Vs[u]) for u in nu]
    g1 = [_dot(_bf(jnp.concatenate([At[u], Rt[u]], axis=0)),
               _bf(jnp.concatenate([Bt[u], Kt[u]], axis=0).T)) for u in nu]
    Aab = [jnp.where(strict, g1[u][:C2, :C2], 0.0) for u in nu]
    Aak = [_bf(jnp.where(strict, g1[u][:C2, C2:], 0.0)) for u in nu]
    Arb = [_bf(jnp.where(incl, g1[u][C2:, :C2], 0.0)) for u in nu]
    Ark = [_bf(jnp.where(incl, g1[u][C2:, C2:], 0.0)) for u in nu]
    rcat = lambda *t: jnp.concatenate(t, axis=0)
    Tm = [eye_c + Aab[u] for u in nu]
    Nb = [_bf(Aab[u]) for u in nu]
    Pw = [_dot(Nb[u], Nb[u]) for u in nu]
    for _ in range(chunk.bit_length() - 3):
        Pb = [_bf(Pw[u]) for u in nu]
        st = [_dot(rcat(_bf(Tm[u]), Pb[u]), Pb[u]) for u in nu]
        Tm = [Tm[u] + st[u][:C2] for u in nu]
        Pw = [st[u][C2:] for u in nu]
    Tm = [Tm[u] + _dot(_bf(Tm[u]), _bf(Pw[u])) for u in nu]
    wrow = [wc_ref[ci * SUBLANES:ci * SUBLANES + 1, lanes(j)] for ci, j in units]
    BwT = [_bf((Bt[u] * wrow[u]).T) for u in nu]
    KwT = [_bf((Kt[u] * wrow[u]).T) for u in nu]
    sv = [_dot(rcat(Aak[u], Ark[u], KwT[u]), Vb[u]) for u in nu]
    PPb = [_bf(_dot(_bf(Tm[u]), _bf(jnp.concatenate([At[u], sv[u][:C2]], axis=1)))) for u in nu]
    sp = [_dot(rcat(Arb[u], BwT[u]), PPb[u]) for u in nu]
    Q1M1, Q2, M2 = [], [], []
    for u in nu:
        q1s = Rt[u] + sp[u][:C2, :LANES]
        q2s = sp[u][:C2, LANES:] + sv[u][C2:2 * C2]
        m1 = eye_k * wrow[u] + sp[u][C2:, :LANES]
        Q1M1.append(_bf(rcat(q1s[:chunk] + q1s[chunk:], m1)))
        Q2.append(q2s[:chunk] + q2s[chunk:])
        M2.append(sp[u][C2:, LANES:] + sv[u][2 * C2:])

    H = {(si, j): h_ref[si, j] for si in range(nsub) for j in range(npair)}
    Y = []
    for u, (ci, j) in enumerate(units):
        key = (ci // group, j)
        sh = _dot(Q1M1[u], _bf(H[key]))
        Y.append(sh[:chunk] + Q2[u])
        H[key] = sh[chunk:] + M2[u]
    for (si, j), val in H.items():
        h_ref[si, j] = val

    def headmean(t):
        hi, lo = _split2(t)
        m = _dot(rcat(hi, lo), eavg)
        return m[:chunk] + m[chunk:]

    mu = [headmean(Y[u]) for u in nu]
    dv = [Y[u] - mu[u] for u in nu]
    var = [headmean(dv[u] * dv[u]) for u in nu]
    for u, (ci, j) in enumerate(units):
        yn = dv[u] * lax.rsqrt(var[u] + LNX_EPS) * lg_ref[:, lanes(j)] + lb_ref[:, lanes(j)]
        y_ref[rows(ci), lanes(j)] = (yn + bonus_ref[rows(ci), lanes(j)]) * g_ref[rows(ci), lanes(j)]

    @pl.when(c == nsteps - 1)
    def _():
        hout_ref[...] = h_ref[...]


def _wkv(rt, kt, bt, at, v, g, bonus, wc, h0, w, *, nstreams, ncs, chunk):
    total = rt.shape[0]
    group = min(WKV_UNITS, ncs)
    nsub = min(WKV_UNITS // group, nstreams)
    nsteps = ncs // group
    blk = pl.BlockSpec((nsub * group * chunk, D_A), lambda s, c: (s * nsteps + c, 0))
    hspec = pl.BlockSpec((nsub, A_HEADS // 2, LANES, LANES), lambda s, c: (s, 0, 0, 0))
    return pl.pallas_call(
        functools.partial(_wkv_kernel, chunk=chunk, group=group, nsub=nsub, nsteps=nsteps),
        out_shape=[jax.ShapeDtypeStruct((total, D_A), F32),
                   jax.ShapeDtypeStruct((nstreams, A_HEADS // 2, LANES, LANES), F32)],
        grid=(nstreams // nsub, nsteps),
        in_specs=[blk] * 7 + [pl.BlockSpec((nsub * group * SUBLANES, D_A), lambda s, c: (s * nsteps + c, 0)),
                              _full((1, D_A)), _full((1, D_A)), _full((LANES, LANES)), hspec],
        out_specs=[blk, hspec],
        scratch_shapes=[pltpu.VMEM((nsub, A_HEADS // 2, LANES, LANES), F32)],
        compiler_params=pltpu.CompilerParams(
            dimension_semantics=("arbitrary", "arbitrary"), vmem_limit_bytes=VMEM_LIMIT),
        name="wkv",
    )(rt, kt, bt, at, v, g, bonus, wc, w["lnx_g"], w["lnx_b"], w["eavg"], h0)


def _layer_norm(t, g, b):
    mu = jnp.mean(t, axis=-1, keepdims=True)
    d = t - mu
    var = jnp.mean(d * d, axis=-1, keepdims=True)
    return d * lax.rsqrt(var + LN_EPS) * g + b


def _tail_kernel(x_ref, ya_ref, yb_ref, wg_ref, bg_ref, wpa_ref, wpb_ref, wo_ref, l1g_ref, l1b_ref,
                 wgu_ref, wdown_ref, l2g_ref, l2b_ref, o_ref):
    x = x_ref[...]
    gates = _sigmoid(_dot(_bf(x), wg_ref[...]) + bg_ref[...])
    m = (gates[:, :D_MODEL] * _dot(_bf(ya_ref[...]), wpa_ref[...])
         + gates[:, D_MODEL:] * _dot(_bf(yb_ref[...]), wpb_ref[...]))
    h = _layer_norm(DN_ALPHA * x + _dot(_bf(m), wo_ref[...]), l1g_ref[...], l1b_ref[...])
    hb = _bf(h)

    def gate_up(c):
        cols = slice(c * FF_CHUNK, (c + 1) * FF_CHUNK)
        ucols = slice(D_FF + c * FF_CHUNK, D_FF + (c + 1) * FF_CHUNK)
        return _dot(hb, wgu_ref[:, cols]), _dot(hb, wgu_ref[:, ucols])

    f = None
    nxt = gate_up(0)
    for c in range(N_FF):
        gate, up = nxt
        if c + 1 < N_FF:
            nxt = gate_up(c + 1)
        d = _dot(_bf(gate * _sigmoid(gate) * up), wdown_ref[c * FF_CHUNK:(c + 1) * FF_CHUNK, :])
        f = d if f is None else f + d
    o_ref[...] = _layer_norm(DN_ALPHA * h + f, l2g_ref[...], l2b_ref[...])


def _tail(x2, ya, yb, w, *, rows):
    total = x2.shape[0]
    rowblk = lambda n: pl.BlockSpec((rows, n), lambda i: (i, 0))

    def const(shape):
        n = len(shape)
        return pl.BlockSpec(shape, lambda i: (0,) * n, pipeline_mode=pl.Buffered(1))

    return pl.pallas_call(
        _tail_kernel,
        out_shape=jax.ShapeDtypeStruct((total, D_MODEL), F32),
        grid=(total // rows,),
        in_specs=[rowblk(D_MODEL), rowblk(D_A), rowblk(D_B),
                  const((D_MODEL, 2 * D_MODEL)), const((1, 2 * D_MODEL)),
                  const((D_A, D_MODEL)), const((D_B, D_MODEL)), const((D_MODEL, D_MODEL)),
                  const((1, D_MODEL)), const((1, D_MODEL)),
                  const((D_MODEL, 2 * D_FF)), const((D_FF, D_MODEL)),
                  const((1, D_MODEL)), const((1, D_MODEL))],
        out_specs=rowblk(D_MODEL),
        compiler_params=pltpu.CompilerParams(
            dimension_semantics=("arbitrary",), vmem_limit_bytes=VMEM_LIMIT),
        name="tail",
    )(x2, ya, yb, w["wg"], w["bg"], w["wpa"], w["wpb"], w["wo"], w["l1g"], w["l1b"],
      w["wgu"], w["wdown"], w["l2g"], w["l2b"])


def _prep_weights(w_in, mu_shift, w0, w_w2, a0, w_a2, w_g2, k_k, k_a, r_k, lnx_g, lnx_b, w_pa,
                  q_norm_g, w_uq, kv_norm_g, w_ukv, w_pb, b_gate, w_o, ln1_g, ln1_b, w_gu, w_down,
                  ln2_g, ln2_b):
    row = lambda t: t.reshape(1, -1).astype(F32)
    nb = A_COLS + B_COLS
    pe = w_in[:, nb - ROPE_DIM:nb]
    half = ROPE_DIM // 2
    pe_sw = jnp.concatenate([pe[:, half:], pe[:, :half]], axis=1)
    wall = jnp.concatenate([w_in[:, :nb - ROPE_DIM], pe, pe_sw,
                            jnp.zeros((D_MODEL, LANES - 2 * ROPE_DIM), F32)], axis=1)
    uq = w_uq.reshape(Q_LORA, B_HEADS, NOPE_DIM + ROPE_DIM)
    nope, r1, r2 = uq[..., :NOPE_DIM], uq[..., NOPE_DIM:NOPE_DIM + half], uq[..., NOPE_DIM + half:]
    zpad = jnp.zeros((Q_LORA, B_HEADS, QK_PAD - NOPE_DIM - ROPE_DIM), F32)
    wqa = jnp.concatenate([nope, r1, r2, zpad], axis=-1).reshape(Q_LORA, B_HEADS * QK_PAD)
    wqb = jnp.concatenate([jnp.zeros_like(nope), r2, r1, zpad], axis=-1).reshape(Q_LORA, B_HEADS * QK_PAD)
    ukv = w_ukv.reshape(KV_LORA, B_HEADS, NOPE_DIM + V_DIM)
    wk = jnp.concatenate([ukv[..., :NOPE_DIM], jnp.zeros((KV_LORA, B_HEADS, QK_PAD - NOPE_DIM), F32)],
                         axis=-1).reshape(KV_LORA, B_HEADS * QK_PAD)
    wv = ukv[..., NOPE_DIM:].reshape(KV_LORA, D_B)
    place = np.zeros((ROPE_DIM, B_HEADS * QK_PAD), np.float32)
    for h in range(B_HEADS):
        place[np.arange(ROPE_DIM), h * QK_PAD + NOPE_DIM + np.arange(ROPE_DIM)] = 1.0
    hid = np.arange(D_A) // A_HEAD_DIM
    esum = (hid[:, None] == hid[None, :]).astype(np.float32)
    lid = np.arange(LANES) // A_HEAD_DIM
    eavg = (lid[:, None] == lid[None, :]).astype(np.float32) / A_HEAD_DIM
    zl = jnp.zeros((LANES - DECAY_LORA, D_A), F32)
    return {
        "wall": _bf(wall), "mu": row(mu_shift), "w0": row(w0), "a0": row(a0), "k_k": row(k_k),
        "k_a": row(k_a), "r_k": row(r_k),
        "ww2": _bf(jnp.concatenate([w_w2, zl], axis=0)), "wa2": _bf(jnp.concatenate([zl, w_a2], axis=0)),
        "wg2": _bf(w_g2), "esum": jnp.asarray(esum, BF16), "eavg": jnp.asarray(eavg, BF16),
        "qg": row(q_norm_g), "kvg": row(kv_norm_g), "wqa": _bf(wqa), "wqb": _bf(wqb),
        "wk": _bf(wk), "wv": _bf(wv), "place": jnp.asarray(place, BF16),
        "wkt": _bf(wk.T), "placet": jnp.asarray(place.T, BF16),
        "lnx_g": row(lnx_g), "lnx_b": row(lnx_b),
        "wg": _bf(w_in[:, nb:]), "bg": row(b_gate), "wpa": _bf(w_pa), "wpb": _bf(w_pb), "wo": _bf(w_o),
        "l1g": row(ln1_g), "l1b": row(ln1_b), "l2g": row(ln2_g), "l2b": row(ln2_b),
        "wgu": _bf(w_gu), "wdown": _bf(w_down),
    }


def _rope_table(pos0, t, reps):
    half = ROPE_DIM // 2
    inv = ROPE_BASE ** (-jnp.arange(half, dtype=F32) / half)
    ang = (pos0 + jnp.arange(t)).astype(F32)[:, None] * jnp.tile(inv, LANES // half)[None, :]
    grp = np.arange(LANES) // half
    sc = np.where(grp >= 4, SCORE_SCALE, 1.0)
    mc = jnp.asarray(np.where(grp % 4 < 2, sc, 0.0), F32)
    ms = jnp.asarray(np.where(grp % 4 == 2, -sc, np.where(grp % 4 == 3, sc, 0.0)), F32)
    return jnp.tile(jnp.cos(ang) * mc + jnp.sin(ang) * ms, (reps, 1))


def _state_to_pairs(s):
    nb = s.shape[0]
    st = jnp.swapaxes(s, -1, -2).reshape(nb, A_HEADS // 2, 2, A_HEAD_DIM, A_HEAD_DIM)
    z = jnp.zeros_like(st[:, :, 0])
    top = jnp.concatenate([st[:, :, 0], z], axis=-1)
    bot = jnp.concatenate([z, st[:, :, 1]], axis=-1)
    return jnp.concatenate([top, bot], axis=-2)


def _pairs_to_state(hp):
    nb = hp.shape[0]
    diag = jnp.stack([hp[:, :, :A_HEAD_DIM, :A_HEAD_DIM], hp[:, :, A_HEAD_DIM:, A_HEAD_DIM:]], axis=2)
    return jnp.swapaxes(diag.reshape(nb, A_HEADS, A_HEAD_DIM, A_HEAD_DIM), -1, -2)


def _layer(x, pos0, shift0, wkv0, cache, w):
    nstreams, t, _ = x.shape
    total = nstreams * t
    x2 = x.reshape(total, D_MODEL)
    chunk = min(CHUNK, t)
    rope = _rope_table(pos0, t, nstreams)
    (rt, kt, bt, at, v, g, bonus, wc, shift, q, ckv, kpe, *kv) = _proj(
        x2, shift0, w, rope, nstreams=nstreams, seg=t, rows=min(PROJ_ROWS, total), chunk=chunk,
        expand_kv=cache is None)

    ya, hout = _wkv(rt, kt, bt, at, v, g, bonus, wc, _state_to_pairs(wkv0), w,
                    nstreams=nstreams, ncs=t // chunk, chunk=chunk)

    if cache is None:
        kk, vv = kv
        yb = _attn(q.reshape(nstreams, t, -1), kk.reshape(nstreams, -1, t), vv.reshape(nstreams, t, -1),
                   blk=min(ATTN_BLOCK, t))
    else:
        yb = _attnc(q.reshape(nstreams, t, -1), cache[0], cache[1], ckv, kpe, w)

    y = _tail(x2, ya, yb.reshape(total, D_B), w, rows=min(TAIL_ROWS, total))
    return (y.reshape(nstreams, t, D_MODEL), ckv.reshape(nstreams, t, KV_LORA),
            kpe.reshape(nstreams, t, ROPE_DIM), _pairs_to_state(hout), shift)


def kernel(x_prompt, x_sample, cache_ckv, cache_kpe, state_wkv, state_shift, w_in, mu_shift, w0, w_w2, a0,
           w_a2, w_g2, k_k, k_a, r_k, lnx_g, lnx_b, w_pa, q_norm_g, w_uq, kv_norm_g, w_ukv, w_pb, b_gate,
           w_o, ln1_g, ln1_b, w_gu, w_down, ln2_g, ln2_b):
    w = _prep_weights(w_in, mu_shift, w0, w_w2, a0, w_a2, w_g2, k_k, k_a, r_k, lnx_g, lnx_b, w_pa,
                      q_norm_g, w_uq, kv_norm_g, w_ukv, w_pb, b_gate, w_o, ln1_g, ln1_b, w_gu, w_down,
                      ln2_g, ln2_b)
    bp = x_prompt.shape[0]
    y_p, ckv_p, kpe_p, wkv_p, shift_p = _layer(
        x_prompt, 0, jnp.zeros((bp, 1, A_COLS), F32),
        jnp.zeros((bp, A_HEADS, A_HEAD_DIM, A_HEAD_DIM), F32), None, w)
    y_s, ckv_s, kpe_s, wkv_s, shift_s = _layer(
        x_sample, cache_ckv.shape[1], state_shift, state_wkv, (cache_ckv, cache_kpe), w)
    return (y_p, y_s, ckv_p, kpe_p, wkv_p, shift_p, ckv_s, kpe_s, wkv_s, shift_s)
```
